```python
import math
import jax, jax.numpy as jnp
from jax import lax
import numpy as np

D_MODEL = 1024
BATCH = 2
SEQ = 8192
DEPTH = 1
DEC_BATCH = 128
DEC_SEQ = 8
PAST_LEN = 16384
PAGE_SIZE = 128

HEAD_DIM = 64
RWKV_HEADS = 8
RWKV_WIDTH = RWKV_HEADS * HEAD_DIM
ATT_HEADS = 8
ATT_KV_HEADS = 2
ATT_GROUPS = ATT_HEADS // ATT_KV_HEADS
ATT_WIDTH = ATT_HEADS * HEAD_DIM
ATT_KV_WIDTH = ATT_KV_HEADS * HEAD_DIM
MIX_WIDTH = RWKV_WIDTH + ATT_WIDTH
DECAY_LORA = 64
ICLR_LORA = 64
GATE_LORA = 128
RWKV_PROJ = 3 * RWKV_WIDTH + DECAY_LORA + ICLR_LORA + GATE_LORA
IN_PROJ = RWKV_PROJ + ATT_WIDTH + 2 * ATT_KV_WIDTH
WINDOW = 128
N_BUCKETS = 32
MAX_DISTANCE = 128
D_FF = 2816
N_MOD = 9
ALPHA = (2 * DEPTH) ** 0.25
BETA = (8 * DEPTH) ** -0.25
LN_EPS = 1e-5
GN_EPS = 64e-5
NEG_INF = -1e30

kernel_name = 'hymba_rwkv7_swa_sink_macaron_deepnorm_step'


def post_ln(h, g, b):
    hf = h.astype(jnp.float32)
    mu = jnp.mean(hf, axis=-1, keepdims=True)
    var = jnp.mean(jnp.square(hf - mu), axis=-1, keepdims=True)
    return ((hf - mu) * lax.rsqrt(var + LN_EPS) * g + b).astype(h.dtype)


def modulate(x, shift, scale):
    return x * (1 + scale) + shift


def swiglu(u, w_in, w_out):
    gt, up = jnp.split(u @ w_in, 2, axis=-1)
    return (jax.nn.silu(gt) * up) @ w_out


def rel_bucket(dist):
    max_exact = N_BUCKETS // 2
    n = jnp.maximum(dist, 0)
    n_f = jnp.maximum(n, max_exact).astype(jnp.float32)
    large = max_exact + (jnp.log(n_f / max_exact) / math.log(MAX_DISTANCE / max_exact)
                         * (N_BUCKETS - max_exact)).astype(jnp.int32)
    return jnp.where(n < max_exact, n, jnp.minimum(large, N_BUCKETS - 1))


def band_bias(n_q, n_prefix, n_keys, rel_bias):
    dist = jnp.arange(n_q)[:, None] + n_prefix - jnp.arange(n_keys)[None, :]
    mask = (dist >= 0) & (dist < WINDOW)
    bias = jnp.take(rel_bias, rel_bucket(dist), axis=0)
    bias = jnp.transpose(bias, (2, 0, 1)).reshape(ATT_KV_HEADS, ATT_GROUPS, n_q, n_keys)
    return bias.astype(jnp.float32), mask


def sink_attention(q, k, v, bias, mask, sinks):
    s = jnp.einsum('...qhgd,...shd->...hgqs', q, k).astype(jnp.float32) * (HEAD_DIM ** -0.5) + bias
    s = jnp.where(mask, s, NEG_INF)
    sink = jnp.broadcast_to(sinks.astype(jnp.float32).reshape(ATT_KV_HEADS, ATT_GROUPS, 1, 1), s.shape[:-1] + (1,))
    p = jax.nn.softmax(jnp.concatenate([s, sink], axis=-1), axis=-1)[..., :-1]
    return jnp.einsum('...hgqs,...shd->...qhgd', p.astype(v.dtype), v)


def swa_prompt(q, k, v, rel_bias, sinks):
    bsz, t = q.shape[:2]
    nb = t // WINDOW
    qb = q.reshape(bsz, nb, WINDOW, ATT_KV_HEADS, ATT_GROUPS, HEAD_DIM)

    def band(a):
        ap = jnp.pad(a, ((0, 0), (WINDOW, 0), (0, 0), (0, 0))).reshape(bsz, nb + 1, WINDOW, ATT_KV_HEADS, HEAD_DIM)
        return jnp.concatenate([ap[:, :-1], ap[:, 1:]], axis=2)

    bias, mask = band_bias(WINDOW, WINDOW, 2 * WINDOW, rel_bias)
    valid = (jnp.arange(nb)[:, None, None] > 0) | (jnp.arange(2 * WINDOW)[None, None, :] >= WINDOW)
    mask = mask[None] & valid
    o = sink_attention(qb, band(k), band(v), bias, mask[None, :, None, None], sinks)
    return o.reshape(bsz, t, ATT_WIDTH)


def swa_sample(q, k, v, kbuf, vbuf, rel_bias, sinks):
    bsz, t = q.shape[:2]
    wb = kbuf.shape[1]
    kc = jnp.concatenate([kbuf.astype(k.dtype), k], axis=1)
    vc = jnp.concatenate([vbuf.astype(v.dtype), v], axis=1)
    bias, mask = band_bias(t, wb, wb + t, rel_bias)
    o = sink_attention(q, kc, vc, bias, mask[None, None, None], sinks)
    return o.reshape(bsz, t, ATT_WIDTH), kc[:, -wb:], vc[:, -wb:]


def wkv7_scan(r, w, k, v, kk, a, s0):
    def step(s, inp):
        r_t, w_t, k_t, v_t, kk_t, a_t = inp
        sa = jnp.einsum('bhvk,bhk->bhv', s, kk_t)
        s = s * w_t[:, :, None, :] - sa[..., None] * (kk_t * a_t)[:, :, None, :] + v_t[..., None] * k_t[:, :, None, :]
        return s, jnp.einsum('bhvk,bhk->bhv', s, r_t)
    xs = tuple(jnp.moveaxis(z.astype(jnp.float32), 1, 0) for z in (r, w, k, v, kk, a))
    s_fin, ys = lax.scan(step, s0.astype(jnp.float32), xs)
    return jnp.moveaxis(ys, 0, 1), s_fin


def rwkv_mix(p, shift0, s0, lp):
    bsz, t = p.shape[:2]
    prev = jnp.concatenate([shift0[:, None].astype(p.dtype), p[:, :-1]], axis=1)
    xs = p + (prev - p) * lp['mu_shift']
    r, k, v, wl, al, gl = jnp.split(xs, [RWKV_WIDTH, 2 * RWKV_WIDTH, 3 * RWKV_WIDTH,
                                         3 * RWKV_WIDTH + DECAY_LORA,
                                         3 * RWKV_WIDTH + DECAY_LORA + ICLR_LORA], axis=-1)
    w = -jax.nn.softplus(-(lp['w0'] + jnp.tanh(wl) @ lp['w_decay']).astype(jnp.float32)) - 0.5
    decay = jnp.exp(-jnp.exp(w))
    a = jax.nn.sigmoid(lp['a0'] + al @ lp['w_iclr'])
    g = jax.nn.sigmoid(gl) @ lp['w_gate']
    heads = lambda z: z.reshape(bsz, t, RWKV_HEADS, HEAD_DIM)
    kk = heads(k * lp['k_k']).astype(jnp.float32)
    kk = kk / jnp.maximum(jnp.sqrt(jnp.sum(kk * kk, axis=-1, keepdims=True)), 1e-12)
    k = k * (1 + (a - 1) * lp['k_a'])
    rh, kh, vh = heads(r), heads(k), heads(v)
    y, s_fin = wkv7_scan(rh, heads(decay), kh, vh, kk, heads(a), s0)
    mu = jnp.mean(y, axis=-1, keepdims=True)
    var = jnp.mean(jnp.square(y - mu), axis=-1, keepdims=True)
    yn = ((y - mu) * lax.rsqrt(var + GN_EPS)).reshape(bsz, t, RWKV_WIDTH) * lp['gn_w'] + lp['gn_b']
    bonus = (jnp.sum(rh * kh * lp['r_k'], axis=-1, keepdims=True) * vh).reshape(bsz, t, RWKV_WIDTH)
    out = (yn + bonus.astype(jnp.float32)) * g
    return out.astype(p.dtype), p[:, -1], s_fin


def block(x, c, wkv0, shift0, kbuf, vbuf, w_buf, lp, rel_bias):
    bsz, t = x.shape[:2]
    mod = (jax.nn.silu(c) @ lp['w_ada'] + lp['b_ada'])[:, None, :]
    sh1, sc1, gt1, sh2, sc2, gt2, sh3, sc3, gt3 = jnp.split(mod, N_MOD, axis=-1)
    f1 = swiglu(modulate(x, sh1, sc1), lp['w_ffn1_in'], lp['w_ffn1_out'])
    x = post_ln(ALPHA * x + 0.5 * (1 + gt1) * f1, lp['ln_g'][0], lp['ln_b'][0])
    u = modulate(x, sh2, sc2)
    p = u @ lp['w_in']
    p_rwkv, q, k, v = jnp.split(p, [RWKV_PROJ, RWKV_PROJ + ATT_WIDTH, RWKV_PROJ + ATT_WIDTH + ATT_KV_WIDTH], axis=-1)
    y_rwkv, shift_new, wkv_new = rwkv_mix(p_rwkv, shift0, wkv0, lp)
    q = q.reshape(bsz, t, ATT_KV_HEADS, ATT_GROUPS, HEAD_DIM)
    k = k.reshape(bsz, t, ATT_KV_HEADS, HEAD_DIM)
    v = v.reshape(bsz, t, ATT_KV_HEADS, HEAD_DIM)
    if kbuf is None:
        y_att = swa_prompt(q, k, v, rel_bias, lp['sinks'])
        kb_new, vb_new = k[:, t - w_buf:], v[:, t - w_buf:]
    else:
        y_att, kb_new, vb_new = swa_sample(q, k, v, kbuf, vbuf, rel_bias, lp['sinks'])
    mix = jnp.concatenate([y_rwkv, y_att.astype(y_rwkv.dtype)], axis=-1) @ lp['w_out']
    x = post_ln(ALPHA * x + (1 + gt2) * mix, lp['ln_g'][1], lp['ln_b'][1])
    f2 = swiglu(modulate(x, sh3, sc3), lp['w_ffn2_in'], lp['w_ffn2_out'])
    x = post_ln(ALPHA * x + 0.5 * (1 + gt3) * f2, lp['ln_g'][2], lp['ln_b'][2])
    return x, wkv_new, shift_new, kb_new, vb_new


def setup_inputs(seed: int = 0) -> dict:
    key = jax.random.key(seed)
    ks = jax.random.split(key, 32)
    f32 = jnp.float32

    def nrm(i, shape, scale):
        return jax.random.normal(ks[i], shape, f32) * scale

    w_buf = min(WINDOW, PAST_LEN)
    L, D, F = DEPTH, D_MODEL, D_FF
    return {
        'x_prompt': nrm(0, (BATCH, SEQ, D), 1.0),
        'x_sample': nrm(1, (DEC_BATCH, DEC_SEQ, D), 1.0),
        'state_wkv': nrm(2, (L, DEC_BATCH, RWKV_HEADS, HEAD_DIM, HEAD_DIM), 0.1),
        'state_shift': nrm(3, (L, DEC_BATCH, RWKV_PROJ), 1.0),
        'cache_win_k': nrm(4, (L, DEC_BATCH, w_buf, ATT_KV_HEADS, HEAD_DIM), 1.0),
        'cache_win_v': nrm(5, (L, DEC_BATCH, w_buf, ATT_KV_HEADS, HEAD_DIM), 1.0),
        'c_prompt': nrm(6, (BATCH, D), 1.0),
        'c_sample': nrm(7, (DEC_BATCH, D), 1.0),
        'rel_bias': nrm(8, (N_BUCKETS, ATT_HEADS), 0.5),
        'w_ada': nrm(9, (L, D, N_MOD * D), 0.1 * D ** -0.5),
        'b_ada': nrm(10, (L, N_MOD * D), 0.01),
        'ln_g': 1.0 + nrm(11, (L, 3, D), 0.01),
        'ln_b': nrm(12, (L, 3, D), 0.01),
        'w_ffn1_in': nrm(13, (L, D, 2 * F), D ** -0.5),
        'w_ffn1_out': nrm(14, (L, F, D), BETA * F ** -0.5),
        'w_in': nrm(15, (L, D, IN_PROJ), D ** -0.5),
        'mu_shift': jax.random.uniform(ks[16], (L, RWKV_PROJ), f32),
        'w0': jax.random.uniform(ks[17], (L, RWKV_WIDTH), f32, -5.0, 0.5),
        'w_decay': nrm(18, (L, DECAY_LORA, RWKV_WIDTH), 0.1),
        'a0': nrm(19, (L, RWKV_WIDTH), 0.1),
        'w_iclr': nrm(20, (L, ICLR_LORA, RWKV_WIDTH), ICLR_LORA ** -0.5),
        'w_gate': nrm(21, (L, GATE_LORA, RWKV_WIDTH), GATE_LORA ** -0.5),
        'k_k': 0.85 + nrm(22, (L, RWKV_WIDTH), 0.02),
        'k_a': 1.0 + nrm(23, (L, RWKV_WIDTH), 0.02),
        'r_k': nrm(24, (L, RWKV_HEADS, HEAD_DIM), 0.1),
        'gn_w': 1.0 + nrm(25, (L, RWKV_WIDTH), 0.01),
        'gn_b': nrm(26, (L, RWKV_WIDTH), 0.01),
        'sinks': nrm(27, (L, ATT_HEADS), 0.5),
        'w_out': nrm(28, (L, MIX_WIDTH, D), BETA * MIX_WIDTH ** -0.5),
        'w_ffn2_in': nrm(29, (L, D, 2 * F), D ** -0.5),
        'w_ffn2_out': nrm(30, (L, F, D), BETA * F ** -0.5),
    }


def reference(x_prompt, x_sample, state_wkv, state_shift, cache_win_k, cache_win_v,
              c_prompt, c_sample, rel_bias, w_ada, b_ada, ln_g, ln_b,
              w_ffn1_in, w_ffn1_out, w_in, mu_shift, w0, w_decay, a0, w_iclr, w_gate,
              k_k, k_a, r_k, gn_w, gn_b, sinks, w_out, w_ffn2_in, w_ffn2_out):
    w_buf = cache_win_k.shape[2]
    y_prompt, y_sample = x_prompt, x_sample
    p_wkv, p_shift, p_k, p_v = [], [], [], []
    s_wkv, s_shift, s_k, s_v = [], [], [], []
    nbp = x_prompt.shape[0]
    for l in range(DEPTH):
        lp = {'w_ada': w_ada[l], 'b_ada': b_ada[l], 'ln_g': ln_g[l], 'ln_b': ln_b[l],
              'w_ffn1_in': w_ffn1_in[l], 'w_ffn1_out': w_ffn1_out[l], 'w_in': w_in[l],
              'mu_shift': mu_shift[l], 'w0': w0[l], 'w_decay': w_decay[l], 'a0': a0[l],
              'w_iclr': w_iclr[l], 'w_gate': w_gate[l], 'k_k': k_k[l], 'k_a': k_a[l],
              'r_k': r_k[l], 'gn_w': gn_w[l], 'gn_b': gn_b[l], 'sinks': sinks[l],
              'w_out': w_out[l], 'w_ffn2_in': w_ffn2_in[l], 'w_ffn2_out': w_ffn2_out[l]}
        wkv0 = jnp.zeros((nbp, RWKV_HEADS, HEAD_DIM, HEAD_DIM), jnp.float32)
        shift0 = jnp.zeros((nbp, RWKV_PROJ), x_prompt.dtype)
        y_prompt, wkv_p, sh_p, k_p, v_p = block(y_prompt, c_prompt, wkv0, shift0, None, None, w_buf, lp, rel_bias)
        y_sample, wkv_s, sh_s, k_s, v_s = block(y_sample, c_sample, state_wkv[l], state_shift[l],
                                                cache_win_k[l], cache_win_v[l], w_buf, lp, rel_bias)
        p_wkv.append(wkv_p); p_shift.append(sh_p); p_k.append(k_p); p_v.append(v_p)
        s_wkv.append(wkv_s); s_shift.append(sh_s); s_k.append(k_s); s_v.append(v_s)
    return (y_prompt, y_sample,
            jnp.stack(p_wkv).astype(state_wkv.dtype), jnp.stack(p_shift).astype(state_shift.dtype),
            jnp.stack(p_k).astype(cache_win_k.dtype), jnp.stack(p_v).astype(cache_win_v.dtype),
            jnp.stack(s_wkv).astype(state_wkv.dtype), jnp.stack(s_shift).astype(state_shift.dtype),
            jnp.stack(s_k).astype(cache_win_k.dtype), jnp.stack(s_v).astype(cache_win_v.dtype))
```

```python
import functools
import math

import numpy as np
import jax
import jax.numpy as jnp
from jax import lax
from jax.experimental import pallas as pl
from jax.experimental.pallas import tpu as pltpu

F32 = jnp.float32
BF16 = jnp.bfloat16

HEAD_DIM = 64
RWKV_HEADS = 8
RWKV_WIDTH = RWKV_HEADS * HEAD_DIM
ATT_HEADS = 8
ATT_KV_HEADS = 2
ATT_GROUPS = ATT_HEADS // ATT_KV_HEADS
ATT_WIDTH = ATT_HEADS * HEAD_DIM
ATT_KV_WIDTH = ATT_KV_HEADS * HEAD_DIM
DECAY_LORA = 64
ICLR_LORA = 64
GATE_LORA = 128
LORA_WIDTH = DECAY_LORA + ICLR_LORA + GATE_LORA
RWKV_PROJ = 3 * RWKV_WIDTH + LORA_WIDTH
WINDOW = 128
N_BUCKETS = 32
MAX_DISTANCE = 128
N_MOD = 9
DEPTH = 1
ALPHA = (2 * DEPTH) ** 0.25
LN_EPS = 1e-5
GN_EPS = 64e-5
NEG_INF = -1e30

VMEM_LIMIT_BYTES = 56 * 1024 * 1024
ROW_TILE = 256
WKV_CHUNK = 64


def _dot(a, b):
    return jnp.dot(a, b, preferred_element_type=F32)


def _dot_nt(a, b):
    return lax.dot_general(a, b, (((1,), (1,)), ((), ())), preferred_element_type=F32)


def _dot_tn(a, b):
    return lax.dot_general(a, b, (((0,), (0,)), ((), ())), preferred_element_type=F32)


def _split2(a):
    hi = a.astype(BF16)
    lo = (a - hi.astype(F32)).astype(BF16)
    return hi, lo


def _split3(a):
    hi = a.astype(BF16)
    r1 = a - hi.astype(F32)
    mid = r1.astype(BF16)
    lo = (r1 - mid.astype(F32)).astype(BF16)
    return hi, mid, lo


def _dot3(asp, bsp, dot):
    (ah, al), (bh, bl) = asp, bsp
    return dot(ah, bh) + dot(ah, bl) + dot(al, bh)


def _segsum(x, bd):
    hi, mid, lo = _split3(x)
    return _dot(hi, bd) + _dot(mid, bd) + _dot(lo, bd)


def _silu(x):
    return x * jax.nn.sigmoid(x)


def _post_ln(h, g, b):
    mu = jnp.mean(h, axis=-1, keepdims=True)
    d = h - mu
    var = jnp.mean(d * d, axis=-1, keepdims=True)
    return d * lax.rsqrt(var + LN_EPS) * g + b


def _const_spec(shape):
    nd = len(shape)
    return pl.BlockSpec(shape, lambda *_: (0,) * nd, pipeline_mode=pl.Buffered(1))


def _ada_kernel(c_ref, w_ref, b_ref, o_ref):
    s = _silu(c_ref[...]).astype(BF16)
    o_ref[...] = _dot(s, w_ref[...].astype(BF16)) + b_ref[...]


def _ada(c_all, w_ada, b_ada):
    nb, d = c_all.shape
    n = w_ada.shape[1]
    tn = d
    return pl.pallas_call(
        _ada_kernel,
        grid=(n // tn,),
        in_specs=[pl.BlockSpec((nb, d), lambda j: (0, 0)),
                  pl.BlockSpec((d, tn), lambda j: (0, j)),
                  pl.BlockSpec((1, tn), lambda j: (0, j))],
        out_specs=pl.BlockSpec((nb, tn), lambda j: (0, j)),
        out_shape=jax.ShapeDtypeStruct((nb, n), F32),
        name="ada_mod",
    )(c_all, w_ada, b_ada.reshape(1, n))


def _ffn1_kernel(x_ref, mod_ref, w1_ref, w2_ref, win_ref, lng_ref, lnb_ref,
                 x1_ref, pr_ref, q_ref, kv_ref, *, d_ff):
    bb, tt, d = x_ref.shape
    x = x_ref[...]
    sh1, sc1, gt1 = mod_ref[:, 0:1, :], mod_ref[:, 1:2, :], mod_ref[:, 2:3, :]
    sh2, sc2 = mod_ref[:, 3:4, :], mod_ref[:, 4:5, :]
    u = (x * (1.0 + sc1) + sh1).reshape(bb * tt, d).astype(BF16)
    h = _dot(u, w1_ref[...])
    act = (_silu(h[:, :d_ff]) * h[:, d_ff:]).astype(BF16)
    f1 = _dot(act, w2_ref[...]).reshape(bb, tt, d)
    x1 = _post_ln(ALPHA * x + 0.5 * (1.0 + gt1) * f1, lng_ref[0:1, :], lnb_ref[0:1, :])
    x1_ref[...] = x1
    u2 = (x1 * (1.0 + sc2) + sh2).reshape(bb * tt, d).astype(BF16)
    p = _dot(u2, win_ref[...])
    pr_ref[...] = p[:, :RWKV_PROJ].reshape(bb, tt, RWKV_PROJ)
    q_ref[...] = p[:, RWKV_PROJ:RWKV_PROJ + ATT_WIDTH].reshape(bb, tt, ATT_WIDTH)
    kv_ref[...] = p[:, RWKV_PROJ + ATT_WIDTH:].reshape(bb, tt, 2 * ATT_KV_WIDTH)


def _ffn1(x, mod, w1, w2, win, ln_g, ln_b, bb, tt):
    b, t, d = x.shape
    d_ff = w2.shape[0]
    tok = lambda w: pl.BlockSpec((bb, tt, w), lambda i, j: (i, j, 0))
    return pl.pallas_call(
        functools.partial(_ffn1_kernel, d_ff=d_ff),
        grid=(b // bb, t // tt),
        in_specs=[tok(d),
                  pl.BlockSpec((bb, N_MOD, d), lambda i, j: (i, 0, 0)),
                  _const_spec(w1.shape), _const_spec(w2.shape), _const_spec(win.shape),
                  _const_spec(ln_g.shape), _const_spec(ln_b.shape)],
        out_specs=[tok(d), tok(RWKV_PROJ), tok(ATT_WIDTH), tok(2 * ATT_KV_WIDTH)],
        out_shape=[jax.ShapeDtypeStruct((b, t, d), F32),
                   jax.ShapeDtypeStruct((b, t, RWKV_PROJ), F32),
                   jax.ShapeDtypeStruct((b, t, ATT_WIDTH), F32),
                   jax.ShapeDtypeStruct((b, t, 2 * ATT_KV_WIDTH), F32)],
        compiler_params=pltpu.CompilerParams(
            dimension_semantics=("arbitrary", "arbitrary"), vmem_limit_bytes=VMEM_LIMIT_BYTES),
        name="ffn1_ln_inproj",
    )(x, mod, w1, w2, win, ln_g, ln_b)


def _prep_kernel(p_ref, first_ref, mu_ref, w0_ref, a0_ref, kk_ref, ka_ref, rk_ref, wl_ref, bd_ref,
                 r_ref, lw_ref, kt_ref, v_ref, kap_ref, b_ref, g_ref, bonus_ref):
    bb, tt, pw = p_ref.shape
    w = RWKV_WIDTH
    p = p_ref[...]
    rolled = pltpu.roll(p, 1, axis=1)
    row = lax.broadcasted_iota(jnp.int32, p.shape, 1)
    prev = jnp.where(row == 0, first_ref[:, 0], rolled)
    xs = (p + (prev - p) * mu_ref[...]).reshape(bb * tt, pw)
    r, k, v = xs[:, :w], xs[:, w:2 * w], xs[:, 2 * w:3 * w]
    lo = xs[:, 3 * w:]
    lane = lax.broadcasted_iota(jnp.int32, lo.shape, 1)
    nl = jnp.where(lane < DECAY_LORA, jnp.tanh(lo),
                   jnp.where(lane < DECAY_LORA + ICLR_LORA, lo, jax.nn.sigmoid(lo)))
    lora = _dot(nl.astype(BF16), wl_ref[...])
    z = -(w0_ref[...] + lora[:, :w])
    softplus = jnp.maximum(z, 0.0) + jnp.log(1.0 + jnp.exp(-jnp.abs(z)))
    lw = -jnp.exp(-softplus - 0.5)
    a = jax.nn.sigmoid(a0_ref[...] + lora[:, w:2 * w])
    g = lora[:, 2 * w:]
    bd = bd_ref[...]
    kk = k * kk_ref[...]
    kap = kk / jnp.maximum(jnp.sqrt(_segsum(kk * kk, bd)), 1e-12)
    kt = k * (1.0 + (a - 1.0) * ka_ref[...])
    bonus = _segsum(r * kt * rk_ref[...], bd) * v
    shp = (bb, tt, w)
    r_ref[...] = r.reshape(shp)
    lw_ref[...] = lw.reshape(shp)
    kt_ref[...] = kt.reshape(shp)
    v_ref[...] = v.reshape(shp)
    kap_ref[...] = kap.reshape(shp)
    b_ref[...] = (kap * a).reshape(shp)
    g_ref[...] = g.reshape(shp)
    bonus_ref[...] = bonus.reshape(shp)


def _prep(p_rwkv, firsts, mu, w0, a0, k_k, k_a, r_k, w_lora, bd, bb, tt):
    b, t, pw = p_rwkv.shape
    w = RWKV_WIDTH
    tok = lambda width: pl.BlockSpec((bb, tt, width), lambda i, j: (i, j, 0))
    vec = lambda width: pl.BlockSpec((1, width), lambda i, j: (0, 0))
    return pl.pallas_call(
        _prep_kernel,
        grid=(b // bb, t // tt),
        in_specs=[tok(pw),
                  pl.BlockSpec((bb, 1, 1, pw), lambda i, j: (i, j, 0, 0)),
                  vec(pw), vec(w), vec(w), vec(w), vec(w), vec(w),
                  pl.BlockSpec(w_lora.shape, lambda i, j: (0, 0)),
                  pl.BlockSpec(bd.shape, lambda i, j: (0, 0))],
        out_specs=[tok(w)] * 8,
        out_shape=[jax.ShapeDtypeStruct((b, t, w), F32)] * 8,
        compiler_params=pltpu.CompilerParams(dimension_semantics=("arbitrary", "arbitrary")),
        name="rwkv_prep",
    )(p_rwkv, firsts, mu, w0, a0, k_k, k_a, r_k, w_lora, bd)


def _wkv_chunk_head(r, lw_last, kh, rh, bi, ki, ke, be, v, s, tri):
    strict, incl, levels, eye = tri
    khs, rhs_, bis, kis = _split2(kh), _split2(rh), _split2(bi), _split2(ki)
    ss, vs = _split2(s), _split2(v)
    mab = jnp.where(strict, _dot3(khs, bis, _dot_nt), 0.0)
    mak = jnp.where(strict, _dot3(khs, kis, _dot_nt), 0.0)
    arb = jnp.where(incl, _dot3(rhs_, bis, _dot_nt), 0.0)
    ark = jnp.where(incl, _dot3(rhs_, kis, _dot_nt), 0.0)
    x = eye - jnp.where(levels[0], mab, 0.0)
    for m in levels[1:]:
        xs = _split2(x)
        t = _dot3(xs, _split2(jnp.where(m, mab, 0.0)), _dot)
        x = x - _dot3(_split2(t), xs, _dot)
    rhs = _dot3(khs, ss, _dot_nt) + _dot3(_split2(mak), vs, _dot)
    sa = _dot3(_split2(x), _split2(rhs), _dot)
    sas = _split2(sa)
    y = _dot3(rhs_, ss, _dot_nt) + _dot3(_split2(ark), vs, _dot) - _dot3(_split2(arb), sas, _dot)
    s_new = s * jnp.exp(lw_last) + _dot3(vs, _split2(ke), _dot_tn) - _dot3(sas, _split2(be), _dot_tn)
    return y, s_new


def _wkv_kernel(r_ref, lw_ref, kt_ref, v_ref, kap_ref, b_ref, s0_ref, y_ref, sfin_ref, s_scr):
    bb, c, w = r_ref.shape
    n = HEAD_DIM
    j = pl.program_id(1)

    @pl.when(j == 0)
    def _():
        s_scr[...] = s0_ref[...]

    ti = lax.broadcasted_iota(jnp.int32, (c, c), 0)
    ji = lax.broadcasted_iota(jnp.int32, (c, c), 1)
    levels = []
    s = 1
    while s < c:
        same = ((ti ^ ji) & ~(2 * s - 1)) == 0
        levels.append(same & ((ti & s) != 0) & ((ji & s) == 0))
        s *= 2
    tri = (ti > ji, ti >= ji, levels, (ti == ji).astype(F32))
    row = lax.broadcasted_iota(jnp.int32, (c, w), 0)

    def body(bi_, carry):
        lw = lw_ref[bi_]
        cl = lw
        s_ = 1
        while s_ < c:
            cl = cl + jnp.where(row >= s_, pltpu.roll(cl, s_, axis=0), 0.0)
            s_ *= 2
        last = cl[c - 1:c, :]
        e_prev = jnp.exp(cl - lw)
        e_cur = jnp.exp(cl)
        e_inv = jnp.exp(-cl)
        e_end = jnp.exp(last - cl)
        kt = kt_ref[bi_]
        bv = b_ref[bi_]
        kh = kap_ref[bi_] * e_prev
        rh = r_ref[bi_] * e_cur
        bi = bv * e_inv
        ki = kt * e_inv
        ke = kt * e_end
        be = bv * e_end
        v = v_ref[bi_]
        for h in range(w // n):
            sl = slice(h * n, (h + 1) * n)
            y, s_new = _wkv_chunk_head(None, last[:, sl], kh[:, sl], rh[:, sl], bi[:, sl], ki[:, sl],
                                       ke[:, sl], be[:, sl], v[:, sl], s_scr[bi_, h], tri)
            s_scr[bi_, h] = s_new
            y_ref[bi_, :, sl] = y
        return carry

    if bb == 1:
        body(0, 0)
    else:
        lax.fori_loop(0, bb, body, 0)

    @pl.when(j == pl.num_programs(1) - 1)
    def _():
        sfin_ref[...] = s_scr[...]


def _wkv(r, lw, kt, v, kap, bvec, s0, bb, c):
    b, t, w = r.shape
    h = w // HEAD_DIM
    tok = pl.BlockSpec((bb, c, w), lambda i, j: (i, j, 0))
    st = pl.BlockSpec((bb, h, HEAD_DIM, HEAD_DIM), lambda i, j: (i, 0, 0, 0))
    return pl.pallas_call(
        _wkv_kernel,
        grid=(b // bb, t // c),
        in_specs=[tok] * 6 + [st],
        out_specs=[tok, st],
        out_shape=[jax.ShapeDtypeStruct((b, t, w), F32),
                   jax.ShapeDtypeStruct((b, h, HEAD_DIM, HEAD_DIM), F32)],
        scratch_shapes=[pltpu.VMEM((bb, h, HEAD_DIM, HEAD_DIM), F32)],
        compiler_params=pltpu.CompilerParams(dimension_semantics=("arbitrary", "arbitrary")),
        name="wkv_scan",
    )(r, lw, kt, v, kap, bvec, s0)


def _rel_bucket_np(dist):
    max_exact = N_BUCKETS // 2
    n = np.maximum(dist, 0)
    n_f = np.maximum(n, max_exact).astype(np.float32)
    large = max_exact + (np.log(n_f / np.float32(max_exact)) / np.float32(math.log(MAX_DISTANCE / max_exact))
                         * np.float32(N_BUCKETS - max_exact)).astype(np.int32)
    return np.where(n < max_exact, n, np.minimum(large, N_BUCKETS - 1)).astype(np.int32)


def _bias_table(idx, mask, rb_ref, h):
    acc = jnp.zeros(idx.shape, F32)
    for bk in range(N_BUCKETS):
        acc = jnp.where(idx == bk, rb_ref[bk, h], acc)
    return jnp.where(mask, acc, NEG_INF)


def _softmax_sink_pv(s_parts, v_parts, sink):
    m = sink
    for s in s_parts:
        m = jnp.maximum(m, jnp.max(s, axis=-1, keepdims=True))
    den = jnp.exp(sink - m)
    o = None
    for s, (v, dot) in zip(s_parts, v_parts):
        e = jnp.exp(s - m)
        den = den + jnp.sum(e, axis=-1, keepdims=True)
        pv = dot(e.astype(BF16), v)
        o = pv if o is None else o + pv
    return o / den


def _attn_prompt_kernel(q_ref, kvc_ref, kvp_ref, idx_ref, rb_ref, sink_ref, o_ref, tb_ref):
    n = HEAD_DIM
    wq = q_ref.shape[1]
    wk = 2 * wq
    first = (pl.program_id(0) == 0) & (pl.program_id(1) == 0)

    @pl.when(first)
    def _():
        qi = lax.broadcasted_iota(jnp.int32, (wq, wk), 0)
        kj = lax.broadcasted_iota(jnp.int32, (wq, wk), 1)
        dist = qi + wq - kj
        mask = (dist >= 0) & (dist < WINDOW)
        idx = idx_ref[...]
        for h in range(ATT_HEADS):
            tb_ref[h] = _bias_table(idx, mask, rb_ref, h)

    kj = lax.broadcasted_iota(jnp.int32, (wq, wk), 1)
    dead = (pl.program_id(1) == 0) & (kj < wq)
    q = q_ref[0]
    kvc = kvc_ref[0].astype(BF16)
    kvp = kvp_ref[0].astype(BF16)
    for h2 in range(ATT_KV_HEADS):
        k_all = jnp.concatenate([kvp[:, h2 * n:(h2 + 1) * n], kvc[:, h2 * n:(h2 + 1) * n]], axis=0)
        v_all = jnp.concatenate([kvp[:, ATT_KV_WIDTH + h2 * n:ATT_KV_WIDTH + (h2 + 1) * n],
                                 kvc[:, ATT_KV_WIDTH + h2 * n:ATT_KV_WIDTH + (h2 + 1) * n]], axis=0)
        for g in range(ATT_GROUPS):
            h = h2 * ATT_GROUPS + g
            qh = q[:, h * n:(h + 1) * n].astype(BF16)
            s = _dot_nt(qh, k_all) * (HEAD_DIM ** -0.5) + tb_ref[h]
            s = jnp.where(dead, NEG_INF, s)
            o_ref[0, :, h * n:(h + 1) * n] = _softmax_sink_pv([s], [(v_all, _dot)], sink_ref[0, h])


def _attn_prompt(q, kv, rel_bias, sinks):
    b, t, _ = q.shape
    wq = WINDOW
    dist = np.arange(wq)[:, None] + wq - np.arange(2 * wq)[None, :]
    idx = jnp.asarray(_rel_bucket_np(dist))
    smem = pl.BlockSpec(memory_space=pltpu.SMEM)
    return pl.pallas_call(
        _attn_prompt_kernel,
        grid=(b, t // wq),
        in_specs=[pl.BlockSpec((1, wq, ATT_WIDTH), lambda i, j: (i, j, 0)),
                  pl.BlockSpec((1, wq, 2 * ATT_KV_WIDTH), lambda i, j: (i, j, 0)),
                  pl.BlockSpec((1, wq, 2 * ATT_KV_WIDTH), lambda i, j: (i, jnp.maximum(j - 1, 0), 0)),
                  pl.BlockSpec(idx.shape, lambda i, j: (0, 0)),
                  smem, smem],
        out_specs=pl.BlockSpec((1, wq, ATT_WIDTH), lambda i, j: (i, j, 0)),
        out_shape=jax.ShapeDtypeStruct((b, t, ATT_WIDTH), F32),
        scratch_shapes=[pltpu.VMEM((ATT_HEADS, wq, 2 * wq), F32)],
        compiler_params=pltpu.CompilerParams(dimension_semantics=("arbitrary", "arbitrary")),
        name="attn_prompt",
    )(q, kv, kv, idx, rel_bias, sinks.reshape(1, ATT_HEADS))


def _attn_sample_kernel(q_ref, kvn_ref, kb_ref, vb_ref, idxb_ref, idxn_ref, rb_ref, sink_ref,
                        o_ref, kbo_ref, vbo_ref, tbb_ref, tbn_ref):
    n = HEAD_DIM
    bb, t, _ = q_ref.shape
    wb = kb_ref.shape[1]

    @pl.when(pl.program_id(0) == 0)
    def _():
        qi = lax.broadcasted_iota(jnp.int32, (t, wb), 0)
        kj = lax.broadcasted_iota(jnp.int32, (t, wb), 1)
        dist_b = qi + wb - kj
        mask_b = (dist_b >= 0) & (dist_b < WINDOW)
        qn = lax.broadcasted_iota(jnp.int32, (t, t), 0)
        kn = lax.broadcasted_iota(jnp.int32, (t, t), 1)
        dist_n = qn - kn
        mask_n = (dist_n >= 0) & (dist_n < WINDOW)
        for h in range(ATT_HEADS):
            tb = _bias_table(idxb_ref[...], mask_b, rb_ref, h)
            tn = _bias_table(idxn_ref[...], mask_n, rb_ref, h)
            for g4 in range(1):
                tbb_ref[h] = tb
                tbn_ref[h] = tn

    q = q_ref[...]
    kvn = kvn_ref[...]
    kb = kb_ref[...]
    vb = vb_ref[...]
    kbo_ref[:, :wb - t, :] = kb[:, t:, :]
    kbo_ref[:, wb - t:, :] = kvn[:, :, :ATT_KV_WIDTH]
    vbo_ref[:, :wb - t, :] = vb[:, t:, :]
    vbo_ref[:, wb - t:, :] = kvn[:, :, ATT_KV_WIDTH:]
    bnt = lambda a, b_: jnp.einsum("bqd,bkd->bqk", a, b_, preferred_element_type=F32)
    bnn = lambda a, b_: jnp.einsum("bqk,bkd->bqd", a, b_, preferred_element_type=F32)
    for h2 in range(ATT_KV_HEADS):
        ks = slice(h2 * n, (h2 + 1) * n)
        vs = slice(ATT_KV_WIDTH + h2 * n, ATT_KV_WIDTH + (h2 + 1) * n)
        k_buf = kb[:, :, ks].astype(BF16)
        v_buf = vb[:, :, ks].astype(BF16)
        k_new = kvn[:, :, ks].astype(BF16)
        v_new = kvn[:, :, vs].astype(BF16)
        for g in range(ATT_GROUPS):
            h = h2 * ATT_GROUPS + g
            qh = q[:, :, h * n:(h + 1) * n].astype(BF16)
            s_b = bnt(qh, k_buf) * (HEAD_DIM ** -0.5) + tbb_ref[h][None]
            s_n = bnt(qh, k_new) * (HEAD_DIM ** -0.5) + tbn_ref[h][None]
            o_ref[:, :, h * n:(h + 1) * n] = _softmax_sink_pv(
                [s_b, s_n], [(v_buf, bnn), (v_new, bnn)], sink_ref[0, h])


def _attn_sample(q, kv, kbuf, vbuf, rel_bias, sinks, bb):
    b, t, _ = q.shape
    wb = kbuf.shape[1]
    dist_b = np.arange(t)[:, None] + wb - np.arange(wb)[None, :]
    dist_n = np.arange(t)[:, None] - np.arange(t)[None, :]
    idx_b = jnp.asarray(_rel_bucket_np(dist_b))
    idx_n = jnp.asarray(_rel_bucket_np(dist_n))
    smem = pl.BlockSpec(memory_space=pltpu.SMEM)
    tok = lambda w: pl.BlockSpec((bb, t, w), lambda i: (i, 0, 0))
    buf = pl.BlockSpec((bb, wb, ATT_KV_WIDTH), lambda i: (i, 0, 0))
    return pl.pallas_call(
        _attn_sample_kernel,
        grid=(b // bb,),
        in_specs=[tok(ATT_WIDTH), tok(2 * ATT_KV_WIDTH), buf, buf,
                  pl.BlockSpec(idx_b.shape, lambda i: (0, 0)),
                  pl.BlockSpec(idx_n.shape, lambda i: (0, 0)),
                  smem, smem],
        out_specs=[tok(ATT_WIDTH), buf, buf],
        out_shape=[jax.ShapeDtypeStruct((b, t, ATT_WIDTH), F32),
                   jax.ShapeDtypeStruct(kbuf.shape, F32),
                   jax.ShapeDtypeStruct(vbuf.shape, F32)],
        scratch_shapes=[pltpu.VMEM((ATT_HEADS, t, wb), F32), pltpu.VMEM((ATT_HEADS, t, t), F32)],
        compiler_params=pltpu.CompilerParams(dimension_semantics=("arbitrary",)),
        name="attn_sample",
    )(q, kv, kbuf, vbuf, idx_b, idx_n, rel_bias, sinks.reshape(1, ATT_HEADS))


def _out_kernel(x1_ref, mod_ref, y_ref, g_ref, bonus_ref, att_ref, gnw_ref, gnb_ref, bd_ref,
                wo_ref, w1_ref, w2_ref, lng_ref, lnb_ref, o_ref, *, d_ff):
    bb, tt, d = x1_ref.shape
    w = RWKV_WIDTH
    m = bb * tt
    x1 = x1_ref[...]
    gt2 = mod_ref[:, 5:6, :]
    sh3, sc3, gt3 = mod_ref[:, 6:7, :], mod_ref[:, 7:8, :], mod_ref[:, 8:9, :]
    bd = bd_ref[...]
    y = y_ref[...].reshape(m, w)
    mu = _segsum(y, bd) * (1.0 / HEAD_DIM)
    dy = y - mu
    var = _segsum(dy * dy, bd) * (1.0 / HEAD_DIM)
    yn = dy * lax.rsqrt(var + GN_EPS) * gnw_ref[...] + gnb_ref[...]
    y_rwkv = (yn + bonus_ref[...].reshape(m, w)) * g_ref[...].reshape(m, w)
    mix = (_dot(y_rwkv.astype(BF16), wo_ref[:w, :])
           + _dot(att_ref[...].reshape(m, ATT_WIDTH).astype(BF16), wo_ref[w:, :])).reshape(bb, tt, d)
    x2 = _post_ln(ALPHA * x1 + (1.0 + gt2) * mix, lng_ref[1:2, :], lnb_ref[1:2, :])
    u = (x2 * (1.0 + sc3) + sh3).reshape(m, d).astype(BF16)
    h = _dot(u, w1_ref[...])
    act = (_silu(h[:, :d_ff]) * h[:, d_ff:]).astype(BF16)
    f2 = _dot(act, w2_ref[...]).reshape(bb, tt, d)
    o_ref[...] = _post_ln(ALPHA * x2 + 0.5 * (1.0 + gt3) * f2, lng_ref[2:3, :], lnb_ref[2:3, :])


def _out(x1, mod, y, g, bonus, att, gn_w, gn_b, bd, wo, w1, w2, ln_g, ln_b, bb, tt):
    b, t, d = x1.shape
    d_ff = w2.shape[0]
    tok = lambda w: pl.BlockSpec((bb, tt, w), lambda i, j: (i, j, 0))
    return pl.pallas_call(
        functools.partial(_out_kernel, d_ff=d_ff),
        grid=(b // bb, t // tt),
        in_specs=[tok(d),
                  pl.BlockSpec((bb, N_MOD, d), lambda i, j: (i, 0, 0)),
                  tok(RWKV_WIDTH), tok(RWKV_WIDTH), tok(RWKV_WIDTH), tok(ATT_WIDTH),
                  _const_spec(gn_w.shape), _const_spec(gn_b.shape), _const_spec(bd.shape),
                  _const_spec(wo.shape), _const_spec(w1.shape), _const_spec(w2.shape),
                  _const_spec(ln_g.shape), _const_spec(ln_b.shape)],
        out_specs=tok(d),
        out_shape=jax.ShapeDtypeStruct((b, t, d), F32),
        compiler_params=pltpu.CompilerParams(
            dimension_semantics=("arbitrary", "arbitrary"), vmem_limit_bytes=VMEM_LIMIT_BYTES),
        name="mix_ln_ffn2",
    )(x1, mod, y, g, bonus, att, gn_w, gn_b, bd, wo, w1, w2, ln_g, ln_b)


def _layer(x, mod, wkv0, shift0, kbuf, vbuf, wts, bb, tt, wkv_bb, wkv_c):
    b, t, d = x.shape
    x1, p_rwkv, q, kv = _ffn1(x, mod, wts["w1a"], wts["w2a"], wts["win"], wts["ln_g"], wts["ln_b"], bb, tt)
    nt = t // tt
    firsts = jnp.concatenate([shift0[:, None, :], p_rwkv[:, tt - 1:t - 1:tt, :]], axis=1)
    firsts = firsts.reshape(b, nt, 1, RWKV_PROJ)
    r, lw, kt, v, kap, bvec, g, bonus = _prep(
        p_rwkv, firsts, wts["mu"], wts["w0"], wts["a0"], wts["k_k"], wts["k_a"], wts["r_k"],
        wts["w_lora"], wts["bd"], bb, tt)
    y, s_fin = _wkv(r, lw, kt, v, kap, bvec, wkv0, wkv_bb, wkv_c)
    shift_new = p_rwkv[:, t - 1, :]
    if kbuf is None:
        att = _attn_prompt(q, kv, wts["rel_bias"], wts["sinks"])
        wb = WINDOW
        kb_new = kv[:, t - wb:, :ATT_KV_WIDTH]
        vb_new = kv[:, t - wb:, ATT_KV_WIDTH:]
    else:
        att, kb_new, vb_new = _attn_sample(q, kv, kbuf, vbuf, wts["rel_bias"], wts["sinks"], bb)
    out = _out(x1, mod, y, g, bonus, att, wts["gn_w"], wts["gn_b"], wts["bd"], wts["wo"],
               wts["w1b"], wts["w2b"], wts["ln_g"], wts["ln_b"], bb, tt)
    return out, s_fin, shift_new, kb_new, vb_new


def kernel(x_prompt, x_sample, state_wkv, state_shift, cache_win_k, cache_win_v, c_prompt, c_sample, rel_bias, w_ada, b_ada, ln_g, ln_b, w_ffn1_in, w_ffn1_out, w_in, mu_shift, w0, w_decay, a0, w_iclr, w_gate, k_k, k_a, r_k, gn_w, gn_b, sinks, w_out, w_ffn2_in, w_ffn2_out):
    bp, tp, d = x_prompt.shape
    bs, ts, _ = x_sample.shape
    depth = w_ada.shape[0]
    assert depth == 1
    l = 0
    w = RWKV_WIDTH
    wb = cache_win_k.shape[2]

    mod = _ada(jnp.concatenate([c_prompt, c_sample], axis=0), w_ada[l], b_ada[l])
    mod = mod.reshape(bp + bs, N_MOD, d)
    mod_p, mod_s = mod[:bp], mod[bp:]

    w_lora = jnp.zeros((LORA_WIDTH, 3 * w), F32)
    w_lora = w_lora.at[:DECAY_LORA, :w].set(w_decay[l])
    w_lora = w_lora.at[DECAY_LORA:DECAY_LORA + ICLR_LORA, w:2 * w].set(w_iclr[l])
    w_lora = w_lora.at[DECAY_LORA + ICLR_LORA:, 2 * w:].set(w_gate[l])
    seg = np.arange(w) // HEAD_DIM
    bd = jnp.asarray((seg[:, None] == seg[None, :]).astype(np.float32), dtype=BF16)
    row = lambda z: z.reshape(1, -1)
    wts = dict(
        w1a=w_ffn1_in[l].astype(BF16), w2a=w_ffn1_out[l].astype(BF16), win=w_in[l].astype(BF16),
        w1b=w_ffn2_in[l].astype(BF16), w2b=w_ffn2_out[l].astype(BF16), wo=w_out[l].astype(BF16),
        ln_g=ln_g[l], ln_b=ln_b[l], mu=row(mu_shift[l]), w0=row(w0[l]), a0=row(a0[l]),
        k_k=row(k_k[l]), k_a=row(k_a[l]), r_k=row(r_k[l]), w_lora=w_lora.astype(BF16), bd=bd,
        gn_w=row(gn_w[l]), gn_b=row(gn_b[l]), rel_bias=rel_bias, sinks=sinks[l])

    wkv0_p = jnp.zeros((bp, RWKV_HEADS, HEAD_DIM, HEAD_DIM), F32)
    shift0_p = jnp.zeros((bp, RWKV_PROJ), F32)
    y_p, wkv_p, sh_p, k_p, v_p = _layer(x_prompt, mod_p, wkv0_p, shift0_p, None, None, wts,
                                        1, ROW_TILE, 1, WKV_CHUNK)
    kbuf = cache_win_k[l].reshape(bs, wb, ATT_KV_WIDTH)
    vbuf = cache_win_v[l].reshape(bs, wb, ATT_KV_WIDTH)
    y_s, wkv_s, sh_s, k_s, v_s = _layer(x_sample, mod_s, state_wkv[l], state_shift[l], kbuf, vbuf, wts,
                                        ROW_TILE // ts, ts, 8, ts)
    kvshape = lambda z: z.reshape(1, z.shape[0], wb, ATT_KV_HEADS, HEAD_DIM)
    return (y_p, y_s, wkv_p[None], sh_p[None], kvshape(k_p), kvshape(v_p),
            wkv_s[None], sh_s[None], kvshape(k_s), kvshape(v_s))
```

```python
import functools
import math

import numpy as np
import jax
import jax.numpy as jnp
from jax import lax
from jax.experimental import pallas as pl
from jax.experimental.pallas import tpu as pltpu

F32 = jnp.float32
BF16 = jnp.bfloat16

HEAD_DIM = 64
RWKV_HEADS = 8
RWKV_WIDTH = RWKV_HEADS * HEAD_DIM
ATT_HEADS = 8
ATT_KV_HEADS = 2
ATT_GROUPS = ATT_HEADS // ATT_KV_HEADS
ATT_WIDTH = ATT_HEADS * HEAD_DIM
ATT_KV_WIDTH = ATT_KV_HEADS * HEAD_DIM
DECAY_LORA = 64
ICLR_LORA = 64
GATE_LORA = 128
LORA_WIDTH = DECAY_LORA + ICLR_LORA + GATE_LORA
RWKV_PROJ = 3 * RWKV_WIDTH + LORA_WIDTH
WINDOW = 128
N_BUCKETS = 32
MAX_DISTANCE = 128
N_MOD = 9
DEPTH = 1
ALPHA = (2 * DEPTH) ** 0.25
LN_EPS = 1e-5
GN_EPS = 64e-5
NEG_INF = -1e30

VMEM_LIMIT_BYTES = 56 * 1024 * 1024
ROW_TILE = 256
WKV_CHUNK = 64


def _dot(a, b):
    return jnp.dot(a, b, preferred_element_type=F32)


def _dot_nt(a, b):
    return lax.dot_general(a, b, (((1,), (1,)), ((), ())), preferred_element_type=F32)


def _dot_tn(a, b):
    return lax.dot_general(a, b, (((0,), (0,)), ((), ())), preferred_element_type=F32)


def _split2(a):
    hi = a.astype(BF16)
    lo = (a - hi.astype(F32)).astype(BF16)
    return hi, lo


def _split3(a):
    hi = a.astype(BF16)
    r1 = a - hi.astype(F32)
    mid = r1.astype(BF16)
    lo = (r1 - mid.astype(F32)).astype(BF16)
    return hi, mid, lo


def _dot3(asp, bsp, dot):
    (ah, al), (bh, bl) = asp, bsp
    return dot(ah, bh) + dot(ah, bl) + dot(al, bh)


def _segsum(x, bd):
    hi, mid, lo = _split3(x)
    return _dot(hi, bd) + _dot(mid, bd) + _dot(lo, bd)


def _silu(x):
    return x * jax.nn.sigmoid(x)


def _post_ln(h, g, b):
    mu = jnp.mean(h, axis=-1, keepdims=True)
    d = h - mu
    var = jnp.mean(d * d, axis=-1, keepdims=True)
    return d * lax.rsqrt(var + LN_EPS) * g + b


def _const_spec(shape):
    nd = len(shape)
    return pl.BlockSpec(shape, lambda *_: (0,) * nd, pipeline_mode=pl.Buffered(1))


def _ada_kernel(c_ref, w_ref, b_ref, o_ref):
    s = _silu(c_ref[...]).astype(BF16)
    o_ref[...] = _dot(s, w_ref[...].astype(BF16)) + b_ref[...]


def _ada(c_all, w_ada, b_ada):
    nb, d = c_all.shape
    n = w_ada.shape[1]
    tn = d
    return pl.pallas_call(
        _ada_kernel,
        grid=(n // tn,),
        in_specs=[pl.BlockSpec((nb, d), lambda j: (0, 0)),
                  pl.BlockSpec((d, tn), lambda j: (0, j)),
                  pl.BlockSpec((1, tn), lambda j: (0, j))],
        out_specs=pl.BlockSpec((nb, tn), lambda j: (0, j)),
        out_shape=jax.ShapeDtypeStruct((nb, n), F32),
        name="ada_mod",
    )(c_all, w_ada, b_ada.reshape(1, n))


def _ffn1_kernel(x_ref, mod_ref, w1_ref, w2_ref, win_ref, lng_ref, lnb_ref,
                 x1_ref, pr_ref, q_ref, kv_ref, *, d_ff):
    bb, tt, d = x_ref.shape
    x = x_ref[...]
    sh1, sc1, gt1 = mod_ref[:, 0:1, :], mod_ref[:, 1:2, :], mod_ref[:, 2:3, :]
    sh2, sc2 = mod_ref[:, 3:4, :], mod_ref[:, 4:5, :]
    u = (x * (1.0 + sc1) + sh1).reshape(bb * tt, d).astype(BF16)
    h = _dot(u, w1_ref[...])
    act = (_silu(h[:, :d_ff]) * h[:, d_ff:]).astype(BF16)
    f1 = _dot(act, w2_ref[...]).reshape(bb, tt, d)
    x1 = _post_ln(ALPHA * x + 0.5 * (1.0 + gt1) * f1, lng_ref[0:1, :], lnb_ref[0:1, :])
    x1_ref[...] = x1
    u2 = (x1 * (1.0 + sc2) + sh2).reshape(bb * tt, d).astype(BF16)
    p = _dot(u2, win_ref[...])
    pr_ref[...] = p[:, :RWKV_PROJ].reshape(bb, tt, RWKV_PROJ)
    q_ref[...] = p[:, RWKV_PROJ:RWKV_PROJ + ATT_WIDTH].reshape(bb, tt, ATT_WIDTH)
    kv_ref[...] = p[:, RWKV_PROJ + ATT_WIDTH:].reshape(bb, tt, 2 * ATT_KV_WIDTH)


def _ffn1(x, mod, w1, w2, win, ln_g, ln_b, bb, tt):
    b, t, d = x.shape
    d_ff = w2.shape[0]
    tok = lambda w: pl.BlockSpec((bb, tt, w), lambda i, j: (i, j, 0))
    return pl.pallas_call(
        functools.partial(_ffn1_kernel, d_ff=d_ff),
        grid=(b // bb, t // tt),
        in_specs=[tok(d),
                  pl.BlockSpec((bb, N_MOD, d), lambda i, j: (i, 0, 0)),
                  _const_spec(w1.shape), _const_spec(w2.shape), _const_spec(win.shape),
                  _const_spec(ln_g.shape), _const_spec(ln_b.shape)],
        out_specs=[tok(d), tok(RWKV_PROJ), tok(ATT_WIDTH), tok(2 * ATT_KV_WIDTH)],
        out_shape=[jax.ShapeDtypeStruct((b, t, d), F32),
                   jax.ShapeDtypeStruct((b, t, RWKV_PROJ), F32),
                   jax.ShapeDtypeStruct((b, t, ATT_WIDTH), F32),
                   jax.ShapeDtypeStruct((b, t, 2 * ATT_KV_WIDTH), F32)],
        compiler_params=pltpu.CompilerParams(
            dimension_semantics=("arbitrary", "arbitrary"), vmem_limit_bytes=VMEM_LIMIT_BYTES),
        name="ffn1_ln_inproj",
    )(x, mod, w1, w2, win, ln_g, ln_b)


def _prep_kernel(p_ref, first_ref, mu_ref, w0_ref, a0_ref, kk_ref, ka_ref, rk_ref, wl_ref, bd_ref,
                 r_ref, lw_ref, kt_ref, v_ref, kap_ref, b_ref, g_ref, bonus_ref):
    bb, tt, pw = p_ref.shape
    w = RWKV_WIDTH
    p = p_ref[...]
    rolled = pltpu.roll(p, 1, axis=1)
    row = lax.broadcasted_iota(jnp.int32, p.shape, 1)
    prev = jnp.where(row == 0, first_ref[:, 0], rolled)
    xs = (p + (prev - p) * mu_ref[...]).reshape(bb * tt, pw)
    r, k, v = xs[:, :w], xs[:, w:2 * w], xs[:, 2 * w:3 * w]
    lo = xs[:, 3 * w:]
    lane = lax.broadcasted_iota(jnp.int32, lo.shape, 1)
    nl = jnp.where(lane < DECAY_LORA, jnp.tanh(lo),
                   jnp.where(lane < DECAY_LORA + ICLR_LORA, lo, jax.nn.sigmoid(lo)))
    lora = _dot(nl.astype(BF16), wl_ref[...])
    z = -(w0_ref[...] + lora[:, :w])
    softplus = jnp.maximum(z, 0.0) + jnp.log(1.0 + jnp.exp(-jnp.abs(z)))
    lw = -jnp.exp(-softplus - 0.5)
    a = jax.nn.sigmoid(a0_ref[...] + lora[:, w:2 * w])
    g = lora[:, 2 * w:]
    bd = bd_ref[...]
    kk = k * kk_ref[...]
    kap = kk / jnp.maximum(jnp.sqrt(_segsum(kk * kk, bd)), 1e-12)
    kt = k * (1.0 + (a - 1.0) * ka_ref[...])
    bonus = _segsum(r * kt * rk_ref[...], bd) * v
    shp = (bb, tt, w)
    r_ref[...] = r.reshape(shp)
    lw_ref[...] = lw.reshape(shp)
    kt_ref[...] = kt.reshape(shp)
    v_ref[...] = v.reshape(shp)
    kap_ref[...] = kap.reshape(shp)
    b_ref[...] = (kap * a).reshape(shp)
    g_ref[...] = g.reshape(shp)
    bonus_ref[...] = bonus.reshape(shp)


def _prep(p_rwkv, firsts, mu, w0, a0, k_k, k_a, r_k, w_lora, bd, bb, tt):
    b, t, pw = p_rwkv.shape
    w = RWKV_WIDTH
    tok = lambda width: pl.BlockSpec((bb, tt, width), lambda i, j: (i, j, 0))
    vec = lambda width: pl.BlockSpec((1, width), lambda i, j: (0, 0))
    return pl.pallas_call(
        _prep_kernel,
        grid=(b // bb, t // tt),
        in_specs=[tok(pw),
                  pl.BlockSpec((bb, 1, 1, pw), lambda i, j: (i, j, 0, 0)),
                  vec(pw), vec(w), vec(w), vec(w), vec(w), vec(w),
                  pl.BlockSpec(w_lora.shape, lambda i, j: (0, 0)),
                  pl.BlockSpec(bd.shape, lambda i, j: (0, 0))],
        out_specs=[tok(w)] * 8,
        out_shape=[jax.ShapeDtypeStruct((b, t, w), F32)] * 8,
        compiler_params=pltpu.CompilerParams(dimension_semantics=("arbitrary", "arbitrary")),
        name="rwkv_prep",
    )(p_rwkv, firsts, mu, w0, a0, k_k, k_a, r_k, w_lora, bd)


def _wkv_units(full, s_list, decays, tri, c):
    strict, incl, levels, eye = tri
    rng = range(len(s_list))
    ss = [_split2(s) for s in s_list]
    gb = [_dot3(full[u]["kr"], full[u]["bi"], _dot_nt) for u in rng]
    gk = [_dot3(full[u]["kr"], full[u]["ki"], _dot_nt) for u in rng]
    st = [_dot3(full[u]["kr"], ss[u], _dot_nt) for u in rng]
    mab = [jnp.where(strict, g[:c], 0.0) for g in gb]
    arb = [_split2(jnp.where(incl, g[c:], 0.0)) for g in gb]
    mk = [_split2(jnp.concatenate([jnp.where(strict, g[:c], 0.0), jnp.where(incl, g[c:], 0.0)], axis=0))
          for g in gk]
    mv = [_dot3(mk[u], full[u]["v"], _dot) for u in rng]
    x = [eye - jnp.where(levels[0], m, 0.0) for m in mab]
    for lvl in levels[1:]:
        xs = [_split2(xi) for xi in x]
        bm = [_split2(jnp.where(lvl, m, 0.0)) for m in mab]
        t = [_split2(_dot3(xs[u], bm[u], _dot)) for u in rng]
        x = [x[u] - _dot3(t[u], xs[u], _dot) for u in rng]
    sa = [_dot3(_split2(x[u]), _split2(st[u][:c] + mv[u][:c]), _dot) for u in rng]
    sas = [_split2(s) for s in sa]
    y = [st[u][c:] + mv[u][c:] - _dot3(arb[u], sas[u], _dot) for u in rng]
    vsa = [_split2(jnp.concatenate([full[u]["vf"], -sa[u]], axis=0)) for u in rng]
    s_new = [s_list[u] * decays[u] + _dot3(vsa[u], full[u]["keb"], _dot_tn) for u in rng]
    return y, s_new


def _wkv_kernel(r_ref, lw_ref, kt_ref, v_ref, kap_ref, b_ref, s0_ref, y_ref, sfin_ref, s_scr, *, group):
    bb, c, w = r_ref.shape
    n = HEAD_DIM
    nh = w // n
    j = pl.program_id(1)

    @pl.when(j == 0)
    def _():
        s_scr[...] = s0_ref[...]

    ti = lax.broadcasted_iota(jnp.int32, (c, c), 0)
    ji = lax.broadcasted_iota(jnp.int32, (c, c), 1)
    levels = []
    s = 1
    while s < c:
        same = ((ti ^ ji) & ~(2 * s - 1)) == 0
        levels.append(same & ((ti & s) != 0) & ((ji & s) == 0))
        s *= 2
    tri = (ti > ji, ti >= ji, levels, (ti == ji).astype(F32))
    row = lax.broadcasted_iota(jnp.int32, (c, w), 0)

    def body(gi, carry):
        full, s_list, decays, where = [], [], [], []
        for k in range(group):
            bi_ = gi * group + k
            lw = lw_ref[bi_]
            cl = lw
            s_ = 1
            while s_ < c:
                cl = cl + jnp.where(row >= s_, pltpu.roll(cl, s_, axis=0), 0.0)
                s_ *= 2
            last = cl[c - 1:c, :]
            e_inv = jnp.exp(-cl)
            e_end = jnp.exp(last - cl)
            kt = kt_ref[bi_]
            bv = b_ref[bi_]
            v = v_ref[bi_]
            wide = dict(
                kr=_split2(jnp.concatenate([kap_ref[bi_] * jnp.exp(cl - lw), r_ref[bi_] * jnp.exp(cl)], axis=0)),
                bi=_split2(bv * e_inv), ki=_split2(kt * e_inv), v=_split2(v),
                keb=_split2(jnp.concatenate([kt * e_end, bv * e_end], axis=0)))
            decay = jnp.exp(last)
            for h in range(nh):
                sl = slice(h * n, (h + 1) * n)
                u = {key: (hi[:, sl], lo[:, sl]) for key, (hi, lo) in wide.items()}
                u["vf"] = v[:, sl]
                full.append(u)
                s_list.append(s_scr[bi_, h])
                decays.append(decay[:, sl])
                where.append((bi_, h, sl))
        ys, s_new = _wkv_units(full, s_list, decays, tri, c)
        for (bi_, h, sl), y, sn in zip(where, ys, s_new):
            s_scr[bi_, h] = sn
            y_ref[bi_, :, sl] = y
        return carry

    if bb == group:
        body(0, 0)
    else:
        lax.fori_loop(0, bb // group, body, 0)

    @pl.when(j == pl.num_programs(1) - 1)
    def _():
        sfin_ref[...] = s_scr[...]


def _wkv(r, lw, kt, v, kap, bvec, s0, bb, c, group):
    b, t, w = r.shape
    h = w // HEAD_DIM
    tok = pl.BlockSpec((bb, c, w), lambda i, j: (i, j, 0))
    st = pl.BlockSpec((bb, h, HEAD_DIM, HEAD_DIM), lambda i, j: (i, 0, 0, 0))
    return pl.pallas_call(
        functools.partial(_wkv_kernel, group=group),
        grid=(b // bb, t // c),
        in_specs=[tok] * 6 + [st],
        out_specs=[tok, st],
        out_shape=[jax.ShapeDtypeStruct((b, t, w), F32),
                   jax.ShapeDtypeStruct((b, h, HEAD_DIM, HEAD_DIM), F32)],
        scratch_shapes=[pltpu.VMEM((bb, h, HEAD_DIM, HEAD_DIM), F32)],
        compiler_params=pltpu.CompilerParams(dimension_semantics=("arbitrary", "arbitrary")),
        name="wkv_scan",
    )(r, lw, kt, v, kap, bvec, s0)


def _rel_bucket_np(dist):
    max_exact = N_BUCKETS // 2
    n = np.maximum(dist, 0)
    n_f = np.maximum(n, max_exact).astype(np.float32)
    large = max_exact + (np.log(n_f / np.float32(max_exact)) / np.float32(math.log(MAX_DISTANCE / max_exact))
                         * np.float32(N_BUCKETS - max_exact)).astype(np.int32)
    return np.where(n < max_exact, n, np.minimum(large, N_BUCKETS - 1)).astype(np.int32)


def _bias_table(idx, mask, rb_ref, h):
    acc = jnp.zeros(idx.shape, F32)
    for bk in range(N_BUCKETS):
        acc = jnp.where(idx == bk, rb_ref[bk, h], acc)
    return jnp.where(mask, acc, NEG_INF)


def _softmax_sink_pv(s_parts, v_parts, sink):
    m = sink
    for s in s_parts:
        m = jnp.maximum(m, jnp.max(s, axis=-1, keepdims=True))
    den = jnp.exp(sink - m)
    o = None
    for s, (v, dot) in zip(s_parts, v_parts):
        e = jnp.exp(s - m)
        den = den + jnp.sum(e, axis=-1, keepdims=True)
        pv = dot(e.astype(BF16), v)
        o = pv if o is None else o + pv
    return o / den


def _attn_prompt_kernel(q_ref, kvc_ref, kvp_ref, idx_ref, rb_ref, sink_ref, o_ref, tb_ref):
    n = HEAD_DIM
    wq = q_ref.shape[1]
    wk = 2 * wq
    first = (pl.program_id(0) == 0) & (pl.program_id(1) == 0)

    @pl.when(first)
    def _():
        qi = lax.broadcasted_iota(jnp.int32, (wq, wk), 0)
        kj = lax.broadcasted_iota(jnp.int32, (wq, wk), 1)
        dist = qi + wq - kj
        mask = (dist >= 0) & (dist < WINDOW)
        idx = idx_ref[...]
        for h in range(ATT_HEADS):
            tb_ref[h] = _bias_table(idx, mask, rb_ref, h)

    kj = lax.broadcasted_iota(jnp.int32, (wq, wk), 1)
    dead = (pl.program_id(1) == 0) & (kj < wq)
    q = q_ref[0]
    kvc = kvc_ref[0].astype(BF16)
    kvp = kvp_ref[0].astype(BF16)
    for h2 in range(ATT_KV_HEADS):
        k_all = jnp.concatenate([kvp[:, h2 * n:(h2 + 1) * n], kvc[:, h2 * n:(h2 + 1) * n]], axis=0)
        v_all = jnp.concatenate([kvp[:, ATT_KV_WIDTH + h2 * n:ATT_KV_WIDTH + (h2 + 1) * n],
                                 kvc[:, ATT_KV_WIDTH + h2 * n:ATT_KV_WIDTH + (h2 + 1) * n]], axis=0)
        for g in range(ATT_GROUPS):
            h = h2 * ATT_GROUPS + g
            qh = q[:, h * n:(h + 1) * n].astype(BF16)
            s = _dot_nt(qh, k_all) * (HEAD_DIM ** -0.5) + tb_ref[h]
            s = jnp.where(dead, NEG_INF, s)
            o_ref[0, :, h * n:(h + 1) * n] = _softmax_sink_pv([s], [(v_all, _dot)], sink_ref[0, h])


def _attn_prompt(q, kv, rel_bias, sinks):
    b, t, _ = q.shape
    wq = WINDOW
    dist = np.arange(wq)[:, None] + wq - np.arange(2 * wq)[None, :]
    idx = jnp.asarray(_rel_bucket_np(dist))
    smem = pl.BlockSpec(memory_space=pltpu.SMEM)
    return pl.pallas_call(
        _attn_prompt_kernel,
        grid=(b, t // wq),
        in_specs=[pl.BlockSpec((1, wq, ATT_WIDTH), lambda i, j: (i, j, 0)),
                  pl.BlockSpec((1, wq, 2 * ATT_KV_WIDTH), lambda i, j: (i, j, 0)),
                  pl.BlockSpec((1, wq, 2 * ATT_KV_WIDTH), lambda i, j: (i, jnp.maximum(j - 1, 0), 0)),
                  pl.BlockSpec(idx.shape, lambda i, j: (0, 0)),
                  smem, smem],
        out_specs=pl.BlockSpec((1, wq, ATT_WIDTH), lambda i, j: (i, j, 0)),
        out_shape=jax.ShapeDtypeStruct((b, t, ATT_WIDTH), F32),
        scratch_shapes=[pltpu.VMEM((ATT_HEADS, wq, 2 * wq), F32)],
        compiler_params=pltpu.CompilerParams(dimension_semantics=("arbitrary", "arbitrary")),
        name="attn_prompt",
    )(q, kv, kv, idx, rel_bias, sinks.reshape(1, ATT_HEADS))


def _attn_sample_kernel(q_ref, kvn_ref, kb_ref, vb_ref, idxb_ref, idxn_ref, rb_ref, sink_ref,
                        o_ref, kbo_ref, vbo_ref, tbb_ref, tbn_ref):
    n = HEAD_DIM
    bb, t, _ = q_ref.shape
    wb = kb_ref.shape[1]

    @pl.when(pl.program_id(0) == 0)
    def _():
        qi = lax.broadcasted_iota(jnp.int32, (t, wb), 0)
        kj = lax.broadcasted_iota(jnp.int32, (t, wb), 1)
        dist_b = qi + wb - kj
        mask_b = (dist_b >= 0) & (dist_b < WINDOW)
        qn = lax.broadcasted_iota(jnp.int32, (t, t), 0)
        kn = lax.broadcasted_iota(jnp.int32, (t, t), 1)
        dist_n = qn - kn
        mask_n = (dist_n >= 0) & (dist_n < WINDOW)
        for h in range(ATT_HEADS):
            tbb_ref[h] = _bias_table(idxb_ref[...], mask_b, rb_ref, h)
            tbn_ref[h] = _bias_table(idxn_ref[...], mask_n, rb_ref, h)

    q = q_ref[...]
    kvn = kvn_ref[...]
    kb = kb_ref[...]
    vb = vb_ref[...]
    kbo_ref[:, :wb - t, :] = kb[:, t:, :]
    kbo_ref[:, wb - t:, :] = kvn[:, :, :ATT_KV_WIDTH]
    vbo_ref[:, :wb - t, :] = vb[:, t:, :]
    vbo_ref[:, wb - t:, :] = kvn[:, :, ATT_KV_WIDTH:]
    bnt = lambda a, b_: jnp.einsum("bqd,bkd->bqk", a, b_, preferred_element_type=F32)
    bnn = lambda a, b_: jnp.einsum("bqk,bkd->bqd", a, b_, preferred_element_type=F32)
    for h2 in range(ATT_KV_HEADS):
        ks = slice(h2 * n, (h2 + 1) * n)
        vs = slice(ATT_KV_WIDTH + h2 * n, ATT_KV_WIDTH + (h2 + 1) * n)
        k_buf = kb[:, :, ks].astype(BF16)
        v_buf = vb[:, :, ks].astype(BF16)
        k_new = kvn[:, :, ks].astype(BF16)
        v_new = kvn[:, :, vs].astype(BF16)
        for g in range(ATT_GROUPS):
            h = h2 * ATT_GROUPS + g
            qh = q[:, :, h * n:(h + 1) * n].astype(BF16)
            s_b = bnt(qh, k_buf) * (HEAD_DIM ** -0.5) + tbb_ref[h][None]
            s_n = bnt(qh, k_new) * (HEAD_DIM ** -0.5) + tbn_ref[h][None]
            o_ref[:, :, h * n:(h + 1) * n] = _softmax_sink_pv(
                [s_b, s_n], [(v_buf, bnn), (v_new, bnn)], sink_ref[0, h])


def _attn_sample(q, kv, kbuf, vbuf, rel_bias, sinks, bb):
    b, t, _ = q.shape
    wb = kbuf.shape[1]
    dist_b = np.arange(t)[:, None] + wb - np.arange(wb)[None, :]
    dist_n = np.arange(t)[:, None] - np.arange(t)[None, :]
    idx_b = jnp.asarray(_rel_bucket_np(dist_b))
    idx_n = jnp.asarray(_rel_bucket_np(dist_n))
    smem = pl.BlockSpec(memory_space=pltpu.SMEM)
    tok = lambda w: pl.BlockSpec((bb, t, w), lambda i: (i, 0, 0))
    buf = pl.BlockSpec((bb, wb, ATT_KV_WIDTH), lambda i: (i, 0, 0))
    return pl.pallas_call(
        _attn_sample_kernel,
        grid=(b // bb,),
        in_specs=[tok(ATT_WIDTH), tok(2 * ATT_KV_WIDTH), buf, buf,
                  pl.BlockSpec(idx_b.shape, lambda i: (0, 0)),
                  pl.BlockSpec(idx_n.shape, lambda i: (0, 0)),
                  smem, smem],
        out_specs=[tok(ATT_WIDTH), buf, buf],
        out_shape=[jax.ShapeDtypeStruct((b, t, ATT_WIDTH), F32),
                   jax.ShapeDtypeStruct(kbuf.shape, F32),
                   jax.ShapeDtypeStruct(vbuf.shape, F32)],
        scratch_shapes=[pltpu.VMEM((ATT_HEADS, t, wb), F32), pltpu.VMEM((ATT_HEADS, t, t), F32)],
        compiler_params=pltpu.CompilerParams(dimension_semantics=("arbitrary",)),
        name="attn_sample",
    )(q, kv, kbuf, vbuf, idx_b, idx_n, rel_bias, sinks.reshape(1, ATT_HEADS))


def _out_kernel(x1_ref, mod_ref, y_ref, g_ref, bonus_ref, att_ref, gnw_ref, gnb_ref, bd_ref,
                wo_ref, w1_ref, w2_ref, lng_ref, lnb_ref, o_ref, *, d_ff):
    bb, tt, d = x1_ref.shape
    w = RWKV_WIDTH
    m = bb * tt
    x1 = x1_ref[...]
    gt2 = mod_ref[:, 5:6, :]
    sh3, sc3, gt3 = mod_ref[:, 6:7, :], mod_ref[:, 7:8, :], mod_ref[:, 8:9, :]
    bd = bd_ref[...]
    y = y_ref[...].reshape(m, w)
    mu = _segsum(y, bd) * (1.0 / HEAD_DIM)
    dy = y - mu
    var = _segsum(dy * dy, bd) * (1.0 / HEAD_DIM)
    yn = dy * lax.rsqrt(var + GN_EPS) * gnw_ref[...] + gnb_ref[...]
    y_rwkv = (yn + bonus_ref[...].reshape(m, w)) * g_ref[...].reshape(m, w)
    mix = (_dot(y_rwkv.astype(BF16), wo_ref[:w, :])
           + _dot(att_ref[...].reshape(m, ATT_WIDTH).astype(BF16), wo_ref[w:, :])).reshape(bb, tt, d)
    x2 = _post_ln(ALPHA * x1 + (1.0 + gt2) * mix, lng_ref[1:2, :], lnb_ref[1:2, :])
    u = (x2 * (1.0 + sc3) + sh3).reshape(m, d).astype(BF16)
    h = _dot(u, w1_ref[...])
    act = (_silu(h[:, :d_ff]) * h[:, d_ff:]).astype(BF16)
    f2 = _dot(act, w2_ref[...]).reshape(bb, tt, d)
    o_ref[...] = _post_ln(ALPHA * x2 + 0.5 * (1.0 + gt3) * f2, lng_ref[2:3, :], lnb_ref[2:3, :])


def _out(x1, mod, y, g, bonus, att, gn_w, gn_b, bd, wo, w1, w2, ln_g, ln_b, bb, tt):
    b, t, d = x1.shape
    d_ff = w2.shape[0]
    tok = lambda w: pl.BlockSpec((bb, tt, w), lambda i, j: (i, j, 0))
    return pl.pallas_call(
        functools.partial(_out_kernel, d_ff=d_ff),
        grid=(b // bb, t // tt),
        in_specs=[tok(d),
                  pl.BlockSpec((bb, N_MOD, d), lambda i, j: (i, 0, 0)),
                  tok(RWKV_WIDTH), tok(RWKV_WIDTH), tok(RWKV_WIDTH), tok(ATT_WIDTH),
                  _const_spec(gn_w.shape), _const_spec(gn_b.shape), _const_spec(bd.shape),
                  _const_spec(wo.shape), _const_spec(w1.shape), _const_spec(w2.shape),
                  _const_spec(ln_g.shape), _const_spec(ln_b.shape)],
        out_specs=tok(d),
        out_shape=jax.ShapeDtypeStruct((b, t, d), F32),
        compiler_params=pltpu.CompilerParams(
            dimension_semantics=("arbitrary", "arbitrary"), vmem_limit_bytes=VMEM_LIMIT_BYTES),
        name="mix_ln_ffn2",
    )(x1, mod, y, g, bonus, att, gn_w, gn_b, bd, wo, w1, w2, ln_g, ln_b)


def _layer(x, mod, wkv0, shift0, kbuf, vbuf, wts, bb, tt, wkv_bb, wkv_c, wkv_group):
    b, t, d = x.shape
    x1, p_rwkv, q, kv = _ffn1(x, mod, wts["w1a"], wts["w2a"], wts["win"], wts["ln_g"], wts["ln_b"], bb, tt)
    nt = t // tt
    firsts = jnp.concatenate([shift0[:, None, :], p_rwkv[:, tt - 1:t - 1:tt, :]], axis=1)
    firsts = firsts.reshape(b, nt, 1, RWKV_PROJ)
    r, lw, kt, v, kap, bvec, g, bonus = _prep(
        p_rwkv, firsts, wts["mu"], wts["w0"], wts["a0"], wts["k_k"], wts["k_a"], wts["r_k"],
        wts["w_lora"], wts["bd"], bb, tt)
    y, s_fin = _wkv(r, lw, kt, v, kap, bvec, wkv0, wkv_bb, wkv_c, wkv_group)
    shift_new = p_rwkv[:, t - 1, :]
    if kbuf is None:
        att = _attn_prompt(q, kv, wts["rel_bias"], wts["sinks"])
        wb = WINDOW
        kb_new = kv[:, t - wb:, :ATT_KV_WIDTH]
        vb_new = kv[:, t - wb:, ATT_KV_WIDTH:]
    else:
        att, kb_new, vb_new = _attn_sample(q, kv, kbuf, vbuf, wts["rel_bias"], wts["sinks"], bb)
    out = _out(x1, mod, y, g, bonus, att, wts["gn_w"], wts["gn_b"], wts["bd"], wts["wo"],
               wts["w1b"], wts["w2b"], wts["ln_g"], wts["ln_b"], bb, tt)
    return out, s_fin, shift_new, kb_new, vb_new


def kernel(x_prompt, x_sample, state_wkv, state_shift, cache_win_k, cache_win_v, c_prompt, c_sample, rel_bias, w_ada, b_ada, ln_g, ln_b, w_ffn1_in, w_ffn1_out, w_in, mu_shift, w0, w_decay, a0, w_iclr, w_gate, k_k, k_a, r_k, gn_w, gn_b, sinks, w_out, w_ffn2_in, w_ffn2_out):
    bp, tp, d = x_prompt.shape
    bs, ts, _ = x_sample.shape
    depth = w_ada.shape[0]
    assert depth == 1
    l = 0
    w = RWKV_WIDTH
    wb = cache_win_k.shape[2]

    mod = _ada(jnp.concatenate([c_prompt, c_sample], axis=0), w_ada[l], b_ada[l])
    mod = mod.reshape(bp + bs, N_MOD, d)
    mod_p, mod_s = mod[:bp], mod[bp:]

    w_lora = jnp.zeros((LORA_WIDTH, 3 * w), F32)
    w_lora = w_lora.at[:DECAY_LORA, :w].set(w_decay[l])
    w_lora = w_lora.at[DECAY_LORA:DECAY_LORA + ICLR_LORA, w:2 * w].set(w_iclr[l])
    w_lora = w_lora.at[DECAY_LORA + ICLR_LORA:, 2 * w:].set(w_gate[l])
    seg = np.arange(w) // HEAD_DIM
    bd = jnp.asarray((seg[:, None] == seg[None, :]).astype(np.float32), dtype=BF16)
    row = lambda z: z.reshape(1, -1)
    wts = dict(
        w1a=w_ffn1_in[l].astype(BF16), w2a=w_ffn1_out[l].astype(BF16), win=w_in[l].astype(BF16),
        w1b=w_ffn2_in[l].astype(BF16), w2b=w_ffn2_out[l].astype(BF16), wo=w_out[l].astype(BF16),
        ln_g=ln_g[l], ln_b=ln_b[l], mu=row(mu_shift[l]), w0=row(w0[l]), a0=row(a0[l]),
        k_k=row(k_k[l]), k_a=row(k_a[l]), r_k=row(r_k[l]), w_lora=w_lora.astype(BF16), bd=bd,
        gn_w=row(gn_w[l]), gn_b=row(gn_b[l]), rel_bias=rel_bias, sinks=sinks[l])

    wkv0_p = jnp.zeros((bp, RWKV_HEADS, HEAD_DIM, HEAD_DIM), F32)
    shift0_p = jnp.zeros((bp, RWKV_PROJ), F32)
    y_p, wkv_p, sh_p, k_p, v_p = _layer(x_prompt, mod_p, wkv0_p, shift0_p, None, None, wts,
                                        1, ROW_TILE, bp, WKV_CHUNK, bp)
    kbuf = cache_win_k[l].reshape(bs, wb, ATT_KV_WIDTH)
    vbuf = cache_win_v[l].reshape(bs, wb, ATT_KV_WIDTH)
    y_s, wkv_s, sh_s, k_s, v_s = _layer(x_sample, mod_s, state_wkv[l], state_shift[l], kbuf, vbuf, wts,
                                        ROW_TILE // ts, ts, 8, ts, 2)
    kvshape = lambda z: z.reshape(1, z.shape[0], wb, ATT_KV_HEADS, HEAD_DIM)
    return (y_p, y_s, wkv_p[None], sh_p[None], kvshape(k_p), kvshape(v_p),
            wkv_s[None], sh_s[None], kvshape(k_s), kvshape(v_s))
```

```python
import functools
import math

import numpy as np
import jax
import jax.numpy as jnp
from jax import lax
from jax.experimental import pallas as pl
from jax.experimental.pallas import tpu as pltpu

F32 = jnp.float32
BF16 = jnp.bfloat16

HEAD_DIM = 64
RWKV_HEADS = 8
RWKV_WIDTH = RWKV_HEADS * HEAD_DIM
ATT_HEADS = 8
ATT_KV_HEADS = 2
ATT_GROUPS = ATT_HEADS // ATT_KV_HEADS
ATT_WIDTH = ATT_HEADS * HEAD_DIM
ATT_KV_WIDTH = ATT_KV_HEADS * HEAD_DIM
DECAY_LORA = 64
ICLR_LORA = 64
GATE_LORA = 128
LORA_WIDTH = DECAY_LORA + ICLR_LORA + GATE_LORA
RWKV_PROJ = 3 * RWKV_WIDTH + LORA_WIDTH
WINDOW = 128
N_BUCKETS = 32
MAX_DISTANCE = 128
N_MOD = 9
DEPTH = 1
ALPHA = (2 * DEPTH) ** 0.25
LN_EPS = 1e-5
GN_EPS = 64e-5
NEG_INF = -1e30

VMEM_LIMIT_BYTES = 56 * 1024 * 1024
ROW_TILE = 256
WKV_CHUNK = 64


def _dot(a, b):
    return jnp.dot(a, b, preferred_element_type=F32)


def _dot_nt(a, b):
    return lax.dot_general(a, b, (((1,), (1,)), ((), ())), preferred_element_type=F32)


def _dot_tn(a, b):
    return lax.dot_general(a, b, (((0,), (0,)), ((), ())), preferred_element_type=F32)


def _split2(a):
    hi = a.astype(BF16)
    lo = (a - hi.astype(F32)).astype(BF16)
    return hi, lo


def _split3(a):
    hi = a.astype(BF16)
    r1 = a - hi.astype(F32)
    mid = r1.astype(BF16)
    lo = (r1 - mid.astype(F32)).astype(BF16)
    return hi, mid, lo


def _dot3(asp, bsp, dot):
    (ah, al), (bh, bl) = asp, bsp
    return dot(ah, bh) + dot(ah, bl) + dot(al, bh)


def _segsum(x, bd):
    hi, mid, lo = _split3(x)
    return _dot(hi, bd) + _dot(mid, bd) + _dot(lo, bd)


def _silu(x):
    return x * jax.nn.sigmoid(x)


def _post_ln(h, g, b):
    mu = jnp.mean(h, axis=-1, keepdims=True)
    d = h - mu
    var = jnp.mean(d * d, axis=-1, keepdims=True)
    return d * lax.rsqrt(var + LN_EPS) * g + b


def _const_spec(shape):
    nd = len(shape)
    return pl.BlockSpec(shape, lambda *_: (0,) * nd, pipeline_mode=pl.Buffered(1))


def _ada_kernel(c_ref, w_ref, b_ref, o_ref):
    s = _silu(c_ref[...]).astype(BF16)
    o_ref[...] = _dot(s, w_ref[...].astype(BF16)) + b_ref[...]


def _ada(c_all, w_ada, b_ada):
    nb, d = c_all.shape
    n = w_ada.shape[1]
    tn = d
    return pl.pallas_call(
        _ada_kernel,
        grid=(n // tn,),
        in_specs=[pl.BlockSpec((nb, d), lambda j: (0, 0)),
                  pl.BlockSpec((d, tn), lambda j: (0, j)),
                  pl.BlockSpec((1, tn), lambda j: (0, j))],
        out_specs=pl.BlockSpec((nb, tn), lambda j: (0, j)),
        out_shape=jax.ShapeDtypeStruct((nb, n), F32),
        name="ada_mod",
    )(c_all, w_ada, b_ada.reshape(1, n))


def _ffn1_kernel(x_ref, mod_ref, w1_ref, w2_ref, win_ref, lng_ref, lnb_ref,
                 x1_ref, pr_ref, q_ref, kv_ref, plast_ref, *, d_ff):
    bb, tt, d = x_ref.shape
    x = x_ref[...]
    sh1, sc1, gt1 = mod_ref[:, 0:1, :], mod_ref[:, 1:2, :], mod_ref[:, 2:3, :]
    sh2, sc2 = mod_ref[:, 3:4, :], mod_ref[:, 4:5, :]
    u = (x * (1.0 + sc1) + sh1).reshape(bb * tt, d).astype(BF16)
    h = _dot(u, w1_ref[...])
    act = (_silu(h[:, :d_ff]) * h[:, d_ff:]).astype(BF16)
    f1 = _dot(act, w2_ref[...]).reshape(bb, tt, d)
    x1 = _post_ln(ALPHA * x + 0.5 * (1.0 + gt1) * f1, lng_ref[0:1, :], lnb_ref[0:1, :])
    x1_ref[...] = x1
    u2 = (x1 * (1.0 + sc2) + sh2).reshape(bb * tt, d).astype(BF16)
    p = _dot(u2, win_ref[...])
    p_rwkv = p[:, :RWKV_PROJ].reshape(bb, tt, RWKV_PROJ)
    pr_ref[...] = p_rwkv
    plast_ref[:, 0] = p_rwkv[:, tt - 1:tt, :]
    q_ref[...] = p[:, RWKV_PROJ:RWKV_PROJ + ATT_WIDTH].reshape(bb, tt, ATT_WIDTH)
    kv_ref[...] = p[:, RWKV_PROJ + ATT_WIDTH:].reshape(bb, tt, 2 * ATT_KV_WIDTH)


def _ffn1(x, mod, w1, w2, win, ln_g, ln_b, bb, tt):
    b, t, d = x.shape
    d_ff = w2.shape[0]
    tok = lambda w: pl.BlockSpec((bb, tt, w), lambda i, j: (i, j, 0))
    return pl.pallas_call(
        functools.partial(_ffn1_kernel, d_ff=d_ff),
        grid=(b // bb, t // tt),
        in_specs=[tok(d),
                  pl.BlockSpec((bb, N_MOD, d), lambda i, j: (i, 0, 0)),
                  _const_spec(w1.shape), _const_spec(w2.shape), _const_spec(win.shape),
                  _const_spec(ln_g.shape), _const_spec(ln_b.shape)],
        out_specs=[tok(d), tok(RWKV_PROJ), tok(ATT_WIDTH), tok(2 * ATT_KV_WIDTH),
                   pl.BlockSpec((bb, 1, 1, RWKV_PROJ), lambda i, j: (i, j, 0, 0))],
        out_shape=[jax.ShapeDtypeStruct((b, t, d), F32),
                   jax.ShapeDtypeStruct((b, t, RWKV_PROJ), F32),
                   jax.ShapeDtypeStruct((b, t, ATT_WIDTH), F32),
                   jax.ShapeDtypeStruct((b, t, 2 * ATT_KV_WIDTH), F32),
                   jax.ShapeDtypeStruct((b, t // tt, 1, RWKV_PROJ), F32)],
        compiler_params=pltpu.CompilerParams(
            dimension_semantics=("arbitrary", "arbitrary"), vmem_limit_bytes=VMEM_LIMIT_BYTES),
        name="ffn1_ln_inproj",
    )(x, mod, w1, w2, win, ln_g, ln_b)


def _prep_kernel(p_ref, first_ref, mu_ref, w0_ref, a0_ref, kk_ref, ka_ref, rk_ref, wl_ref, bd_ref,
                 r_ref, lw_ref, kt_ref, v_ref, kap_ref, b_ref, g_ref, bonus_ref):
    bb, tt, pw = p_ref.shape
    w = RWKV_WIDTH
    p = p_ref[...]
    rolled = pltpu.roll(p, 1, axis=1)
    row = lax.broadcasted_iota(jnp.int32, p.shape, 1)
    prev = jnp.where(row == 0, first_ref[:, 0], rolled)
    xs = (p + (prev - p) * mu_ref[...]).reshape(bb * tt, pw)
    r, k, v = xs[:, :w], xs[:, w:2 * w], xs[:, 2 * w:3 * w]
    lo = xs[:, 3 * w:]
    lane = lax.broadcasted_iota(jnp.int32, lo.shape, 1)
    nl = jnp.where(lane < DECAY_LORA, jnp.tanh(lo),
                   jnp.where(lane < DECAY_LORA + ICLR_LORA, lo, jax.nn.sigmoid(lo)))
    lora = _dot(nl.astype(BF16), wl_ref[...])
    z = -(w0_ref[...] + lora[:, :w])
    softplus = jnp.maximum(z, 0.0) + jnp.log(1.0 + jnp.exp(-jnp.abs(z)))
    lw = -jnp.exp(-softplus - 0.5)
    a = jax.nn.sigmoid(a0_ref[...] + lora[:, w:2 * w])
    g = lora[:, 2 * w:]
    bd = bd_ref[...]
    kk = k * kk_ref[...]
    kap = kk / jnp.maximum(jnp.sqrt(_segsum(kk * kk, bd)), 1e-12)
    kt = k * (1.0 + (a - 1.0) * ka_ref[...])
    bonus = _segsum(r * kt * rk_ref[...], bd) * v
    shp = (bb, tt, w)
    r_ref[...] = r.reshape(shp)
    lw_ref[...] = lw.reshape(shp)
    kt_ref[...] = kt.reshape(shp)
    v_ref[...] = v.reshape(shp)
    kap_ref[...] = kap.reshape(shp)
    b_ref[...] = (kap * a).reshape(shp)
    g_ref[...] = g.reshape(shp)
    bonus_ref[...] = bonus.reshape(shp)


def _prep(p_rwkv, firsts, mu, w0, a0, k_k, k_a, r_k, w_lora, bd, bb, tt):
    b, t, pw = p_rwkv.shape
    w = RWKV_WIDTH
    tok = lambda width: pl.BlockSpec((bb, tt, width), lambda i, j: (i, j, 0))
    vec = lambda width: pl.BlockSpec((1, width), lambda i, j: (0, 0))
    return pl.pallas_call(
        _prep_kernel,
        grid=(b // bb, t // tt),
        in_specs=[tok(pw),
                  pl.BlockSpec((bb, 1, 1, pw), lambda i, j: (i, j, 0, 0)),
                  vec(pw), vec(w), vec(w), vec(w), vec(w), vec(w),
                  pl.BlockSpec(w_lora.shape, lambda i, j: (0, 0)),
                  pl.BlockSpec(bd.shape, lambda i, j: (0, 0))],
        out_specs=[tok(w)] * 8,
        out_shape=[jax.ShapeDtypeStruct((b, t, w), F32)] * 8,
        compiler_params=pltpu.CompilerParams(dimension_semantics=("arbitrary", "arbitrary")),
        name="rwkv_prep",
    )(p_rwkv, firsts, mu, w0, a0, k_k, k_a, r_k, w_lora, bd)


def _wkv_units(full, s_list, decays, tri, c):
    strict, incl, levels, eye = tri
    rng = range(len(s_list))
    ss = [_split2(s) for s in s_list]
    gb = [_dot3(full[u]["kr"], full[u]["bi"], _dot_nt) for u in rng]
    gk = [_dot3(full[u]["kr"], full[u]["ki"], _dot_nt) for u in rng]
    st = [_dot3(full[u]["kr"], ss[u], _dot_nt) for u in rng]
    mab = [jnp.where(strict, g[:c], 0.0) for g in gb]
    arb = [_split2(jnp.where(incl, g[c:], 0.0)) for g in gb]
    mk = [_split2(jnp.concatenate([jnp.where(strict, g[:c], 0.0), jnp.where(incl, g[c:], 0.0)], axis=0))
          for g in gk]
    mv = [_dot3(mk[u], full[u]["v"], _dot) for u in rng]
    x = [eye - jnp.where(levels[0], m, 0.0) for m in mab]
    for lvl in levels[1:]:
        xs = [_split2(xi) for xi in x]
        bm = [_split2(jnp.where(lvl, m, 0.0)) for m in mab]
        t = [_split2(_dot3(xs[u], bm[u], _dot)) for u in rng]
        x = [x[u] - _dot3(t[u], xs[u], _dot) for u in rng]
    sa = [_dot3(_split2(x[u]), _split2(st[u][:c] + mv[u][:c]), _dot) for u in rng]
    sas = [_split2(s) for s in sa]
    y = [st[u][c:] + mv[u][c:] - _dot3(arb[u], sas[u], _dot) for u in rng]
    vsa = [_split2(jnp.concatenate([full[u]["vf"], -sa[u]], axis=0)) for u in rng]
    s_new = [s_list[u] * decays[u] + _dot3(vsa[u], full[u]["keb"], _dot_tn) for u in rng]
    return y, s_new


def _wkv_kernel(r_ref, lw_ref, kt_ref, v_ref, kap_ref, b_ref, s0_ref, y_ref, sfin_ref, s_scr, *, group):
    bb, c, w = r_ref.shape
    n = HEAD_DIM
    nh = w // n
    j = pl.program_id(1)

    @pl.when(j == 0)
    def _():
        s_scr[...] = s0_ref[...]

    ti = lax.broadcasted_iota(jnp.int32, (c, c), 0)
    ji = lax.broadcasted_iota(jnp.int32, (c, c), 1)
    levels = []
    s = 1
    while s < c:
        same = ((ti ^ ji) & ~(2 * s - 1)) == 0
        levels.append(same & ((ti & s) != 0) & ((ji & s) == 0))
        s *= 2
    tri = (ti > ji, ti >= ji, levels, (ti == ji).astype(F32))
    row = lax.broadcasted_iota(jnp.int32, (c, w), 0)

    def body(gi, carry):
        full, s_list, decays, where = [], [], [], []
        for k in range(group):
            bi_ = gi * group + k
            lw = lw_ref[bi_]
            cl = lw
            s_ = 1
            while s_ < c:
                cl = cl + jnp.where(row >= s_, pltpu.roll(cl, s_, axis=0), 0.0)
                s_ *= 2
            last = cl[c - 1:c, :]
            e_inv = jnp.exp(-cl)
            e_end = jnp.exp(last - cl)
            kt = kt_ref[bi_]
            bv = b_ref[bi_]
            v = v_ref[bi_]
            wide = dict(
                kr=_split2(jnp.concatenate([kap_ref[bi_] * jnp.exp(cl - lw), r_ref[bi_] * jnp.exp(cl)], axis=0)),
                bi=_split2(bv * e_inv), ki=_split2(kt * e_inv), v=_split2(v),
                keb=_split2(jnp.concatenate([kt * e_end, bv * e_end], axis=0)))
            decay = jnp.exp(last)
            for h in range(nh):
                sl = slice(h * n, (h + 1) * n)
                u = {key: (hi[:, sl], lo[:, sl]) for key, (hi, lo) in wide.items()}
                u["vf"] = v[:, sl]
                full.append(u)
                s_list.append(s_scr[bi_, h])
                decays.append(decay[:, sl])
                where.append((bi_, h, sl))
        ys, s_new = _wkv_units(full, s_list, decays, tri, c)
        for (bi_, h, sl), y, sn in zip(where, ys, s_new):
            s_scr[bi_, h] = sn
            y_ref[bi_, :, sl] = y
        return carry

    if bb == group:
        body(0, 0)
    else:
        lax.fori_loop(0, bb // group, body, 0)

    @pl.when(j == pl.num_programs(1) - 1)
    def _():
        sfin_ref[...] = s_scr[...]


def _wkv(r, lw, kt, v, kap, bvec, s0, bb, c, group):
    b, t, w = r.shape
    h = w // HEAD_DIM
    tok = pl.BlockSpec((bb, c, w), lambda i, j: (i, j, 0))
    st = pl.BlockSpec((bb, h, HEAD_DIM, HEAD_DIM), lambda i, j: (i, 0, 0, 0))
    return pl.pallas_call(
        functools.partial(_wkv_kernel, group=group),
        grid=(b // bb, t // c),
        in_specs=[tok] * 6 + [st],
        out_specs=[tok, st],
        out_shape=[jax.ShapeDtypeStruct((b, t, w), F32),
                   jax.ShapeDtypeStruct((b, h, HEAD_DIM, HEAD_DIM), F32)],
        scratch_shapes=[pltpu.VMEM((bb, h, HEAD_DIM, HEAD_DIM), F32)],
        compiler_params=pltpu.CompilerParams(dimension_semantics=("arbitrary", "arbitrary")),
        name="wkv_scan",
    )(r, lw, kt, v, kap, bvec, s0)


def _rel_bucket_np(dist):
    max_exact = N_BUCKETS // 2
    n = np.maximum(dist, 0)
    n_f = np.maximum(n, max_exact).astype(np.float32)
    large = max_exact + (np.log(n_f / np.float32(max_exact)) / np.float32(math.log(MAX_DISTANCE / max_exact))
                         * np.float32(N_BUCKETS - max_exact)).astype(np.int32)
    return np.where(n < max_exact, n, np.minimum(large, N_BUCKETS - 1)).astype(np.int32)


def _bias_table(idx, mask, rb_ref, h):
    acc = jnp.zeros(idx.shape, F32)
    for bk in range(N_BUCKETS):
        acc = jnp.where(idx == bk, rb_ref[bk, h], acc)
    return jnp.where(mask, acc, NEG_INF)


def _softmax_sink_pv(s_heads, v_heads, sinks):
    ms = []
    for parts, sink in zip(s_heads, sinks):
        m = sink
        for s in parts:
            m = jnp.maximum(m, jnp.max(s, axis=-1, keepdims=True))
        ms.append(m)
    es = [[jnp.exp(s - m) for s in parts] for parts, m in zip(s_heads, ms)]
    dens = []
    for parts, sink, m in zip(es, sinks, ms):
        den = jnp.exp(sink - m)
        for e in parts:
            den = den + jnp.sum(e, axis=-1, keepdims=True)
        dens.append(den)
    outs = []
    for parts, vparts in zip(es, v_heads):
        o = None
        for e, (v, dot) in zip(parts, vparts):
            pv = dot(e.astype(BF16), v)
            o = pv if o is None else o + pv
        outs.append(o)
    return [o / den for o, den in zip(outs, dens)]


def _attn_prompt_kernel(q_ref, kvc_ref, kvp_ref, idx_ref, rb_ref, sink_ref, o_ref, tb_ref):
    n = HEAD_DIM
    wq = q_ref.shape[1]
    wk = 2 * wq
    first = (pl.program_id(0) == 0) & (pl.program_id(1) == 0)

    @pl.when(first)
    def _():
        qi = lax.broadcasted_iota(jnp.int32, (wq, wk), 0)
        kj = lax.broadcasted_iota(jnp.int32, (wq, wk), 1)
        dist = qi + wq - kj
        mask = (dist >= 0) & (dist < WINDOW)
        idx = idx_ref[...]
        for h in range(ATT_HEADS):
            tb_ref[h] = _bias_table(idx, mask, rb_ref, h)

    kj = lax.broadcasted_iota(jnp.int32, (wq, wk), 1)
    dead = (pl.program_id(1) == 0) & (kj < wq)
    q = q_ref[0]
    kvc = kvc_ref[0].astype(BF16)
    kvp = kvp_ref[0].astype(BF16)
    s_heads, v_heads = [], []
    for h2 in range(ATT_KV_HEADS):
        k_all = jnp.concatenate([kvp[:, h2 * n:(h2 + 1) * n], kvc[:, h2 * n:(h2 + 1) * n]], axis=0)
        v_all = jnp.concatenate([kvp[:, ATT_KV_WIDTH + h2 * n:ATT_KV_WIDTH + (h2 + 1) * n],
                                 kvc[:, ATT_KV_WIDTH + h2 * n:ATT_KV_WIDTH + (h2 + 1) * n]], axis=0)
        for g in range(ATT_GROUPS):
            h = h2 * ATT_GROUPS + g
            qh = q[:, h * n:(h + 1) * n].astype(BF16)
            s = _dot_nt(qh, k_all) * (HEAD_DIM ** -0.5) + tb_ref[h]
            s_heads.append([jnp.where(dead, NEG_INF, s)])
            v_heads.append([(v_all, _dot)])
    outs = _softmax_sink_pv(s_heads, v_heads, [sink_ref[0, h] for h in range(ATT_HEADS)])
    for h, o in enumerate(outs):
        o_ref[0, :, h * n:(h + 1) * n] = o


def _attn_prompt(q, kv, rel_bias, sinks):
    b, t, _ = q.shape
    wq = WINDOW
    dist = np.arange(wq)[:, None] + wq - np.arange(2 * wq)[None, :]
    idx = jnp.asarray(_rel_bucket_np(dist))
    smem = pl.BlockSpec(memory_space=pltpu.SMEM)
    return pl.pallas_call(
        _attn_prompt_kernel,
        grid=(b, t // wq),
        in_specs=[pl.BlockSpec((1, wq, ATT_WIDTH), lambda i, j: (i, j, 0)),
                  pl.BlockSpec((1, wq, 2 * ATT_KV_WIDTH), lambda i, j: (i, j, 0)),
                  pl.BlockSpec((1, wq, 2 * ATT_KV_WIDTH), lambda i, j: (i, jnp.maximum(j - 1, 0), 0)),
                  pl.BlockSpec(idx.shape, lambda i, j: (0, 0)),
                  smem, smem],
        out_specs=pl.BlockSpec((1, wq, ATT_WIDTH), lambda i, j: (i, j, 0)),
        out_shape=jax.ShapeDtypeStruct((b, t, ATT_WIDTH), F32),
        scratch_shapes=[pltpu.VMEM((ATT_HEADS, wq, 2 * wq), F32)],
        compiler_params=pltpu.CompilerParams(dimension_semantics=("arbitrary", "arbitrary")),
        name="attn_prompt",
    )(q, kv, kv, idx, rel_bias, sinks.reshape(1, ATT_HEADS))


def _attn_sample_kernel(q_ref, kvn_ref, kb_ref, vb_ref, idxb_ref, idxn_ref, rb_ref, sink_ref,
                        o_ref, kbo_ref, vbo_ref, tbb_ref, tbn_ref):
    n = HEAD_DIM
    bb, t, _ = q_ref.shape
    wb = kb_ref.shape[1]

    @pl.when(pl.program_id(0) == 0)
    def _():
        qi = lax.broadcasted_iota(jnp.int32, (t, wb), 0)
        kj = lax.broadcasted_iota(jnp.int32, (t, wb), 1)
        dist_b = qi + wb - kj
        mask_b = (dist_b >= 0) & (dist_b < WINDOW)
        qn = lax.broadcasted_iota(jnp.int32, (t, t), 0)
        kn = lax.broadcasted_iota(jnp.int32, (t, t), 1)
        dist_n = qn - kn
        mask_n = (dist_n >= 0) & (dist_n < WINDOW)
        for h in range(ATT_HEADS):
            tbb_ref[h] = _bias_table(idxb_ref[...], mask_b, rb_ref, h)
            tbn_ref[h] = _bias_table(idxn_ref[...], mask_n, rb_ref, h)

    q = q_ref[...]
    kvn = kvn_ref[...]
    kb = kb_ref[...]
    vb = vb_ref[...]
    kbo_ref[:, :wb - t, :] = kb[:, t:, :]
    kbo_ref[:, wb - t:, :] = kvn[:, :, :ATT_KV_WIDTH]
    vbo_ref[:, :wb - t, :] = vb[:, t:, :]
    vbo_ref[:, wb - t:, :] = kvn[:, :, ATT_KV_WIDTH:]
    bnt = lambda a, b_: jnp.einsum("bqd,bkd->bqk", a, b_, preferred_element_type=F32)
    bnn = lambda a, b_: jnp.einsum("bqk,bkd->bqd", a, b_, preferred_element_type=F32)
    s_heads, v_heads = [], []
    for h2 in range(ATT_KV_HEADS):
        ks = slice(h2 * n, (h2 + 1) * n)
        vs = slice(ATT_KV_WIDTH + h2 * n, ATT_KV_WIDTH + (h2 + 1) * n)
        k_buf = kb[:, :, ks].astype(BF16)
        v_buf = vb[:, :, ks].astype(BF16)
        k_new = kvn[:, :, ks].astype(BF16)
        v_new = kvn[:, :, vs].astype(BF16)
        for g in range(ATT_GROUPS):
            h = h2 * ATT_GROUPS + g
            qh = q[:, :, h * n:(h + 1) * n].astype(BF16)
            s_heads.append([bnt(qh, k_buf) * (HEAD_DIM ** -0.5) + tbb_ref[h][None],
                            bnt(qh, k_new) * (HEAD_DIM ** -0.5) + tbn_ref[h][None]])
            v_heads.append([(v_buf, bnn), (v_new, bnn)])
    outs = _softmax_sink_pv(s_heads, v_heads, [sink_ref[0, h] for h in range(ATT_HEADS)])
    for h, o in enumerate(outs):
        o_ref[:, :, h * n:(h + 1) * n] = o


def _attn_sample(q, kv, kbuf, vbuf, rel_bias, sinks, bb):
    b, t, _ = q.shape
    wb = kbuf.shape[1]
    dist_b = np.arange(t)[:, None] + wb - np.arange(wb)[None, :]
    dist_n = np.arange(t)[:, None] - np.arange(t)[None, :]
    idx_b = jnp.asarray(_rel_bucket_np(dist_b))
    idx_n = jnp.asarray(_rel_bucket_np(dist_n))
    smem = pl.BlockSpec(memory_space=pltpu.SMEM)
    tok = lambda w: pl.BlockSpec((bb, t, w), lambda i: (i, 0, 0))
    buf = pl.BlockSpec((bb, wb, ATT_KV_WIDTH), lambda i: (i, 0, 0))
    return pl.pallas_call(
        _attn_sample_kernel,
        grid=(b // bb,),
        in_specs=[tok(ATT_WIDTH), tok(2 * ATT_KV_WIDTH), buf, buf,
                  pl.BlockSpec(idx_b.shape, lambda i: (0, 0)),
                  pl.BlockSpec(idx_n.shape, lambda i: (0, 0)),
                  smem, smem],
        out_specs=[tok(ATT_WIDTH), buf, buf],
        out_shape=[jax.ShapeDtypeStruct((b, t, ATT_WIDTH), F32),
                   jax.ShapeDtypeStruct(kbuf.shape, F32),
                   jax.ShapeDtypeStruct(vbuf.shape, F32)],
        scratch_shapes=[pltpu.VMEM((ATT_HEADS, t, wb), F32), pltpu.VMEM((ATT_HEADS, t, t), F32)],
        compiler_params=pltpu.CompilerParams(dimension_semantics=("arbitrary",)),
        name="attn_sample",
    )(q, kv, kbuf, vbuf, idx_b, idx_n, rel_bias, sinks.reshape(1, ATT_HEADS))


def _out_kernel(x1_ref, mod_ref, y_ref, g_ref, bonus_ref, att_ref, gnw_ref, gnb_ref, bd_ref,
                wo_ref, w1_ref, w2_ref, lng_ref, lnb_ref, o_ref, *, d_ff):
    bb, tt, d = x1_ref.shape
    w = RWKV_WIDTH
    m = bb * tt
    x1 = x1_ref[...]
    gt2 = mod_ref[:, 5:6, :]
    sh3, sc3, gt3 = mod_ref[:, 6:7, :], mod_ref[:, 7:8, :], mod_ref[:, 8:9, :]
    bd = bd_ref[...]
    y = y_ref[...].reshape(m, w)
    mu = _segsum(y, bd) * (1.0 / HEAD_DIM)
    dy = y - mu
    var = _segsum(dy * dy, bd) * (1.0 / HEAD_DIM)
    yn = dy * lax.rsqrt(var + GN_EPS) * gnw_ref[...] + gnb_ref[...]
    y_rwkv = (yn + bonus_ref[...].reshape(m, w)) * g_ref[...].reshape(m, w)
    mix = (_dot(y_rwkv.astype(BF16), wo_ref[:w, :])
           + _dot(att_ref[...].reshape(m, ATT_WIDTH).astype(BF16), wo_ref[w:, :])).reshape(bb, tt, d)
    x2 = _post_ln(ALPHA * x1 + (1.0 + gt2) * mix, lng_ref[1:2, :], lnb_ref[1:2, :])
    u = (x2 * (1.0 + sc3) + sh3).reshape(m, d).astype(BF16)
    h = _dot(u, w1_ref[...])
    act = (_silu(h[:, :d_ff]) * h[:, d_ff:]).astype(BF16)
    f2 = _dot(act, w2_ref[...]).reshape(bb, tt, d)
    o_ref[...] = _post_ln(ALPHA * x2 + 0.5 * (1.0 + gt3) * f2, lng_ref[2:3, :], lnb_ref[2:3, :])


def _out(x1, mod, y, g, bonus, att, gn_w, gn_b, bd, wo, w1, w2, ln_g, ln_b, bb, tt):
    b, t, d = x1.shape
    d_ff = w2.shape[0]
    tok = lambda w: pl.BlockSpec((bb, tt, w), lambda i, j: (i, j, 0))
    return pl.pallas_call(
        functools.partial(_out_kernel, d_ff=d_ff),
        grid=(b // bb, t // tt),
        in_specs=[tok(d),
                  pl.BlockSpec((bb, N_MOD, d), lambda i, j: (i, 0, 0)),
                  tok(RWKV_WIDTH), tok(RWKV_WIDTH), tok(RWKV_WIDTH), tok(ATT_WIDTH),
                  _const_spec(gn_w.shape), _const_spec(gn_b.shape), _const_spec(bd.shape),
                  _const_spec(wo.shape), _const_spec(w1.shape), _const_spec(w2.shape),
                  _const_spec(ln_g.shape), _const_spec(ln_b.shape)],
        out_specs=tok(d),
        out_shape=jax.ShapeDtypeStruct((b, t, d), F32),
        compiler_params=pltpu.CompilerParams(
            dimension_semantics=("arbitrary", "arbitrary"), vmem_limit_bytes=VMEM_LIMIT_BYTES),
        name="mix_ln_ffn2",
    )(x1, mod, y, g, bonus, att, gn_w, gn_b, bd, wo, w1, w2, ln_g, ln_b)


def _layer(x, mod, wkv0, shift0, kbuf, vbuf, wts, bb, tt, wkv_bb, wkv_c, wkv_group):
    b, t, d = x.shape
    x1, p_rwkv, q, kv, plast = _ffn1(x, mod, wts["w1a"], wts["w2a"], wts["win"], wts["ln_g"], wts["ln_b"],
                                     bb, tt)
    firsts = jnp.concatenate([shift0[:, None, None, :], plast[:, :-1]], axis=1)
    r, lw, kt, v, kap, bvec, g, bonus = _prep(
        p_rwkv, firsts, wts["mu"], wts["w0"], wts["a0"], wts["k_k"], wts["k_a"], wts["r_k"],
        wts["w_lora"], wts["bd"], bb, tt)
    y, s_fin = _wkv(r, lw, kt, v, kap, bvec, wkv0, wkv_bb, wkv_c, wkv_group)
    shift_new = plast[:, -1, 0, :]
    if kbuf is None:
        att = _attn_prompt(q, kv, wts["rel_bias"], wts["sinks"])
        wb = WINDOW
        kb_new = kv[:, t - wb:, :ATT_KV_WIDTH]
        vb_new = kv[:, t - wb:, ATT_KV_WIDTH:]
    else:
        att, kb_new, vb_new = _attn_sample(q, kv, kbuf, vbuf, wts["rel_bias"], wts["sinks"], bb)
    out = _out(x1, mod, y, g, bonus, att, wts["gn_w"], wts["gn_b"], wts["bd"], wts["wo"],
               wts["w1b"], wts["w2b"], wts["ln_g"], wts["ln_b"], bb, tt)
    return out, s_fin, shift_new, kb_new, vb_new


def kernel(x_prompt, x_sample, state_wkv, state_shift, cache_win_k, cache_win_v, c_prompt, c_sample, rel_bias, w_ada, b_ada, ln_g, ln_b, w_ffn1_in, w_ffn1_out, w_in, mu_shift, w0, w_decay, a0, w_iclr, w_gate, k_k, k_a, r_k, gn_w, gn_b, sinks, w_out, w_ffn2_in, w_ffn2_out):
    bp, tp, d = x_prompt.shape
    bs, ts, _ = x_sample.shape
    depth = w_ada.shape[0]
    assert depth == 1
    l = 0
    w = RWKV_WIDTH
    wb = cache_win_k.shape[2]

    mod = _ada(jnp.concatenate([c_prompt, c_sample], axis=0), w_ada[l], b_ada[l])
    mod = mod.reshape(bp + bs, N_MOD, d)
    mod_p, mod_s = mod[:bp], mod[bp:]

    w_lora = jnp.zeros((LORA_WIDTH, 3 * w), F32)
    w_lora = w_lora.at[:DECAY_LORA, :w].set(w_decay[l])
    w_lora = w_lora.at[DECAY_LORA:DECAY_LORA + ICLR_LORA, w:2 * w].set(w_iclr[l])
    w_lora = w_lora.at[DECAY_LORA + ICLR_LORA:, 2 * w:].set(w_gate[l])
    seg = np.arange(w) // HEAD_DIM
    bd = jnp.asarray((seg[:, None] == seg[None, :]).astype(np.float32), dtype=BF16)
    row = lambda z: z.reshape(1, -1)
    wts = dict(
        w1a=w_ffn1_in[l].astype(BF16), w2a=w_ffn1_out[l].astype(BF16), win=w_in[l].astype(BF16),
        w1b=w_ffn2_in[l].astype(BF16), w2b=w_ffn2_out[l].astype(BF16), wo=w_out[l].astype(BF16),
        ln_g=ln_g[l], ln_b=ln_b[l], mu=row(mu_shift[l]), w0=row(w0[l]), a0=row(a0[l]),
        k_k=row(k_k[l]), k_a=row(k_a[l]), r_k=row(r_k[l]), w_lora=w_lora.astype(BF16), bd=bd,
        gn_w=row(gn_w[l]), gn_b=row(gn_b[l]), rel_bias=rel_bias, sinks=sinks[l])

    wkv0_p = jnp.zeros((bp, RWKV_HEADS, HEAD_DIM, HEAD_DIM), F32)
    shift0_p = jnp.zeros((bp, RWKV_PROJ), F32)
    y_p, wkv_p, sh_p, k_p, v_p = _layer(x_prompt, mod_p, wkv0_p, shift0_p, None, None, wts,
                                        1, ROW_TILE, bp, WKV_CHUNK, bp)
    kbuf = cache_win_k[l].reshape(bs, wb, ATT_KV_WIDTH)
    vbuf = cache_win_v[l].reshape(bs, wb, ATT_KV_WIDTH)
    y_s, wkv_s, sh_s, k_s, v_s = _layer(x_sample, mod_s, state_wkv[l], state_shift[l], kbuf, vbuf, wts,
                                        ROW_TILE // ts, ts, 8, ts, 2)
    kvshape = lambda z: z.reshape(1, z.shape[0], wb, ATT_KV_HEADS, HEAD_DIM)
    return (y_p, y_s, wkv_p[None], sh_p[None], kvshape(k_p), kvshape(v_p),
            wkv_s[None], sh_s[None], kvshape(k_s), kvshape(v_s))
```

```python
import functools
import math

import numpy as np
import jax
import jax.numpy as jnp
from jax import lax
from jax.experimental import pallas as pl
from jax.experimental.pallas import tpu as pltpu

F32 = jnp.float32
BF16 = jnp.bfloat16

HEAD_DIM = 64
RWKV_HEADS = 8
RWKV_WIDTH = RWKV_HEADS * HEAD_DIM
ATT_HEADS = 8
ATT_KV_HEADS = 2
ATT_GROUPS = ATT_HEADS // ATT_KV_HEADS
ATT_WIDTH = ATT_HEADS * HEAD_DIM
ATT_KV_WIDTH = ATT_KV_HEADS * HEAD_DIM
DECAY_LORA = 64
ICLR_LORA = 64
GATE_LORA = 128
LORA_WIDTH = DECAY_LORA + ICLR_LORA + GATE_LORA
RWKV_PROJ = 3 * RWKV_WIDTH + LORA_WIDTH
WINDOW = 128
N_BUCKETS = 32
MAX_DISTANCE = 128
N_MOD = 9
DEPTH = 1
ALPHA = (2 * DEPTH) ** 0.25
LN_EPS = 1e-5
GN_EPS = 64e-5
NEG_INF = -1e30

VMEM_LIMIT_BYTES = 56 * 1024 * 1024
ROW_TILE = 256
WKV_CHUNK = 64


def _dot(a, b):
    return jnp.dot(a, b, preferred_element_type=F32)


def _dot_nt(a, b):
    return lax.dot_general(a, b, (((1,), (1,)), ((), ())), preferred_element_type=F32)


def _dot_tn(a, b):
    return lax.dot_general(a, b, (((0,), (0,)), ((), ())), preferred_element_type=F32)


def _split2(a):
    hi = a.astype(BF16)
    lo = (a - hi.astype(F32)).astype(BF16)
    return hi, lo


def _dot3(asp, bsp, dot):
    (ah, al), (bh, bl) = asp, bsp
    return dot(ah, bh) + dot(ah, bl) + dot(al, bh)


def _head_sums(x):
    pair = 2 * HEAD_DIM
    low = lax.broadcasted_iota(jnp.int32, (x.shape[0], pair), 1) < HEAD_DIM
    outs = []
    for p in range(x.shape[1] // pair):
        xs = x[:, p * pair:(p + 1) * pair]
        s_lo = jnp.sum(jnp.where(low, xs, 0.0), axis=-1, keepdims=True)
        s_hi = jnp.sum(jnp.where(low, 0.0, xs), axis=-1, keepdims=True)
        outs.append(jnp.where(low, s_lo, s_hi))
    return jnp.concatenate(outs, axis=-1)


def _silu(x):
    return x * jax.nn.sigmoid(x)


def _post_ln(h, g, b):
    mu = jnp.mean(h, axis=-1, keepdims=True)
    d = h - mu
    var = jnp.mean(d * d, axis=-1, keepdims=True)
    return d * lax.rsqrt(var + LN_EPS) * g + b


def _const_spec(shape):
    nd = len(shape)
    return pl.BlockSpec(shape, lambda *_: (0,) * nd, pipeline_mode=pl.Buffered(1))


def _ada_kernel(c_ref, w_ref, b_ref, o_ref):
    s = _silu(c_ref[...]).astype(BF16)
    o_ref[...] = _dot(s, w_ref[...].astype(BF16)) + b_ref[...]


def _ada(c_all, w_ada, b_ada):
    nb, d = c_all.shape
    n = w_ada.shape[1]
    tn = d
    return pl.pallas_call(
        _ada_kernel,
        grid=(n // tn,),
        in_specs=[pl.BlockSpec((nb, d), lambda j: (0, 0)),
                  pl.BlockSpec((d, tn), lambda j: (0, j)),
                  pl.BlockSpec((1, tn), lambda j: (0, j))],
        out_specs=pl.BlockSpec((nb, tn), lambda j: (0, j)),
        out_shape=jax.ShapeDtypeStruct((nb, n), F32),
        name="ada_mod",
    )(c_all, w_ada, b_ada.reshape(1, n))


def _ffn1_kernel(x_ref, mod_ref, shift0_ref, w1_ref, w2_ref, win_ref, lng_ref, lnb_ref,
                 mu_ref, w0_ref, a0_ref, kk_ref, ka_ref, rk_ref, wl_ref,
                 x1_ref, q_ref, kv_ref, shift_ref, *rest, d_ff):
    prep_refs, carry_ref = rest[:-1], rest[-1]
    bb, tt, d = x_ref.shape
    x = x_ref[...]
    sh1, sc1, gt1 = mod_ref[:, 0:1, :], mod_ref[:, 1:2, :], mod_ref[:, 2:3, :]
    sh2, sc2 = mod_ref[:, 3:4, :], mod_ref[:, 4:5, :]
    u = (x * (1.0 + sc1) + sh1).reshape(bb * tt, d).astype(BF16)
    h = _dot(u, w1_ref[...])
    act = (_silu(h[:, :d_ff]) * h[:, d_ff:]).astype(BF16)
    f1 = _dot(act, w2_ref[...]).reshape(bb, tt, d)
    x1 = _post_ln(ALPHA * x + 0.5 * (1.0 + gt1) * f1, lng_ref[0:1, :], lnb_ref[0:1, :])
    x1_ref[...] = x1
    u2 = (x1 * (1.0 + sc2) + sh2).reshape(bb * tt, d).astype(BF16)
    p = _dot(u2, win_ref[...])
    q_ref[...] = p[:, RWKV_PROJ:RWKV_PROJ + ATT_WIDTH].reshape(bb, tt, ATT_WIDTH)
    kv_ref[...] = p[:, RWKV_PROJ + ATT_WIDTH:].reshape(bb, tt, 2 * ATT_KV_WIDTH)
    p_rwkv = p[:, :RWKV_PROJ].reshape(bb, tt, RWKV_PROJ)
    @pl.when(pl.program_id(1) == 0)
    def _():
        carry_ref[...] = shift0_ref[...]

    first = carry_ref[...]
    last = p_rwkv[:, tt - 1:tt, :]
    carry_ref[...] = last
    shift_ref[...] = last
    outs = _rwkv_prep(p_rwkv, first, mu_ref[...], w0_ref[...], a0_ref[...], kk_ref[...], ka_ref[...],
                      rk_ref[...], wl_ref[...])
    for ref, val in zip(prep_refs, outs):
        ref[...] = val.reshape(bb, tt, RWKV_WIDTH)


def _ffn1(x, mod, shift0, w1, w2, win, ln_g, ln_b, mu, w0, a0, k_k, k_a, r_k, w_lora, bb, tt):
    b, t, d = x.shape
    d_ff = w2.shape[0]
    w = RWKV_WIDTH
    tok = lambda width: pl.BlockSpec((bb, tt, width), lambda i, j: (i, j, 0))
    per_seq = pl.BlockSpec((bb, 1, RWKV_PROJ), lambda i, j: (i, 0, 0))
    consts = [w1, w2, win, ln_g, ln_b, mu, w0, a0, k_k, k_a, r_k, w_lora]
    return pl.pallas_call(
        functools.partial(_ffn1_kernel, d_ff=d_ff),
        grid=(b // bb, t // tt),
        in_specs=[tok(d), pl.BlockSpec((bb, N_MOD, d), lambda i, j: (i, 0, 0)), per_seq]
                 + [_const_spec(c.shape) for c in consts],
        out_specs=[tok(d), tok(ATT_WIDTH), tok(2 * ATT_KV_WIDTH), per_seq] + [tok(w)] * 8,
        out_shape=[jax.ShapeDtypeStruct((b, t, d), F32),
                   jax.ShapeDtypeStruct((b, t, ATT_WIDTH), F32),
                   jax.ShapeDtypeStruct((b, t, 2 * ATT_KV_WIDTH), F32),
                   jax.ShapeDtypeStruct((b, 1, RWKV_PROJ), F32)]
                  + [jax.ShapeDtypeStruct((b, t, w), F32)] * 8,
        scratch_shapes=[pltpu.VMEM((bb, 1, RWKV_PROJ), F32)],
        compiler_params=pltpu.CompilerParams(
            dimension_semantics=("arbitrary", "arbitrary"), vmem_limit_bytes=VMEM_LIMIT_BYTES),
        name="ffn1_ln_inproj_prep",
    )(x, mod, shift0, *consts)


def _rwkv_prep(p, first, mu, w0, a0, k_k, k_a, r_k, w_lora):
    bb, tt, pw = p.shape
    w = RWKV_WIDTH
    rolled = pltpu.roll(p, 1, axis=1)
    row = lax.broadcasted_iota(jnp.int32, p.shape, 1)
    prev = jnp.where(row == 0, first, rolled)
    xs = (p + (prev - p) * mu).reshape(bb * tt, pw)
    r, k, v = xs[:, :w], xs[:, w:2 * w], xs[:, 2 * w:3 * w]
    lo = xs[:, 3 * w:]
    lane = lax.broadcasted_iota(jnp.int32, lo.shape, 1)
    nl = jnp.where(lane < DECAY_LORA, jnp.tanh(lo),
                   jnp.where(lane < DECAY_LORA + ICLR_LORA, lo, jax.nn.sigmoid(lo)))
    lora = _dot(nl.astype(BF16), w_lora)
    z = -(w0 + lora[:, :w])
    softplus = jnp.maximum(z, 0.0) + jnp.log(1.0 + jnp.exp(-jnp.abs(z)))
    lw = -jnp.exp(-softplus - 0.5)
    a = jax.nn.sigmoid(a0 + lora[:, w:2 * w])
    g = lora[:, 2 * w:]
    kk = k * k_k
    kap = kk / jnp.maximum(jnp.sqrt(_head_sums(kk * kk)), 1e-12)
    kt = k * (1.0 + (a - 1.0) * k_a)
    bonus = _head_sums(r * kt * r_k) * v
    return r, lw, kt, v, kap, kap * a, g, bonus


def _wkv_units(full, s_list, decays, tri, c):
    strict, incl, levels, eye = tri
    rng = range(len(s_list))
    ss = [_split2(s) for s in s_list]
    gb = [_dot3(full[u]["kr"], full[u]["bi"], _dot_nt) for u in rng]
    gk = [_dot3(full[u]["kr"], full[u]["ki"], _dot_nt) for u in rng]
    st = [_dot3(full[u]["kr"], ss[u], _dot_nt) for u in rng]
    mab = [jnp.where(strict, g[:c], 0.0) for g in gb]
    arb = [_split2(jnp.where(incl, g[c:], 0.0)) for g in gb]
    mk = [_split2(jnp.concatenate([jnp.where(strict, g[:c], 0.0), jnp.where(incl, g[c:], 0.0)], axis=0))
          for g in gk]
    mv = [_dot3(mk[u], full[u]["v"], _dot) for u in rng]
    x = [eye - jnp.where(levels[0], m, 0.0) for m in mab]
    for lvl in levels[1:]:
        xs = [_split2(xi) for xi in x]
        bm = [_split2(jnp.where(lvl, m, 0.0)) for m in mab]
        t = [_split2(_dot3(xs[u], bm[u], _dot)) for u in rng]
        x = [x[u] - _dot3(t[u], xs[u], _dot) for u in rng]
    sa = [_dot3(_split2(x[u]), _split2(st[u][:c] + mv[u][:c]), _dot) for u in rng]
    sas = [_split2(s) for s in sa]
    y = [st[u][c:] + mv[u][c:] - _dot3(arb[u], sas[u], _dot) for u in rng]
    vsa = [_split2(jnp.concatenate([full[u]["vf"], -sa[u]], axis=0)) for u in rng]
    s_new = [s_list[u] * decays[u] + _dot3(vsa[u], full[u]["keb"], _dot_tn) for u in rng]
    return y, s_new


def _wkv_kernel(r_ref, lw_ref, kt_ref, v_ref, kap_ref, b_ref, s0_ref, y_ref, sfin_ref, s_scr, *, group):
    bb, c, w = r_ref.shape
    n = HEAD_DIM
    nh = w // n
    j = pl.program_id(1)

    @pl.when(j == 0)
    def _():
        s_scr[...] = s0_ref[...]

    ti = lax.broadcasted_iota(jnp.int32, (c, c), 0)
    ji = lax.broadcasted_iota(jnp.int32, (c, c), 1)
    levels = []
    s = 1
    while s < c:
        same = ((ti ^ ji) & ~(2 * s - 1)) == 0
        levels.append(same & ((ti & s) != 0) & ((ji & s) == 0))
        s *= 2
    tri = (ti > ji, ti >= ji, levels, (ti == ji).astype(F32))
    row = lax.broadcasted_iota(jnp.int32, (c, w), 0)

    def body(gi, carry):
        full, s_list, decays, where = [], [], [], []
        for k in range(group):
            bi_ = gi * group + k
            lw = lw_ref[bi_]
            cl = lw
            s_ = 1
            while s_ < c:
                cl = cl + jnp.where(row >= s_, pltpu.roll(cl, s_, axis=0), 0.0)
                s_ *= 2
            last = cl[c - 1:c, :]
            e_inv = jnp.exp(-cl)
            e_end = jnp.exp(last - cl)
            kt = kt_ref[bi_]
            bv = b_ref[bi_]
            v = v_ref[bi_]
            wide = dict(
                kr=_split2(jnp.concatenate([kap_ref[bi_] * jnp.exp(cl - lw), r_ref[bi_] * jnp.exp(cl)], axis=0)),
                bi=_split2(bv * e_inv), ki=_split2(kt * e_inv), v=_split2(v),
                keb=_split2(jnp.concatenate([kt * e_end, bv * e_end], axis=0)))
            decay = jnp.exp(last)
            for h in range(nh):
                sl = slice(h * n, (h + 1) * n)
                u = {key: (hi[:, sl], lo[:, sl]) for key, (hi, lo) in wide.items()}
                u["vf"] = v[:, sl]
                full.append(u)
                s_list.append(s_scr[bi_, h])
                decays.append(decay[:, sl])
                where.append((bi_, h, sl))
        ys, s_new = _wkv_units(full, s_list, decays, tri, c)
        for (bi_, h, sl), y, sn in zip(where, ys, s_new):
            s_scr[bi_, h] = sn
            y_ref[bi_, :, sl] = y
        return carry

    if bb == group:
        body(0, 0)
    else:
        lax.fori_loop(0, bb // group, body, 0)

    @pl.when(j == pl.num_programs(1) - 1)
    def _():
        sfin_ref[...] = s_scr[...]


def _wkv(r, lw, kt, v, kap, bvec, s0, bb, c, group):
    b, t, w = r.shape
    h = w // HEAD_DIM
    tok = pl.BlockSpec((bb, c, w), lambda i, j: (i, j, 0))
    st = pl.BlockSpec((bb, h, HEAD_DIM, HEAD_DIM), lambda i, j: (i, 0, 0, 0))
    return pl.pallas_call(
        functools.partial(_wkv_kernel, group=group),
        grid=(b // bb, t // c),
        in_specs=[tok] * 6 + [st],
        out_specs=[tok, st],
        out_shape=[jax.ShapeDtypeStruct((b, t, w), F32),
                   jax.ShapeDtypeStruct((b, h, HEAD_DIM, HEAD_DIM), F32)],
        scratch_shapes=[pltpu.VMEM((bb, h, HEAD_DIM, HEAD_DIM), F32)],
        compiler_params=pltpu.CompilerParams(dimension_semantics=("arbitrary", "arbitrary")),
        name="wkv_scan",
    )(r, lw, kt, v, kap, bvec, s0)


def _rel_bucket_np(dist):
    max_exact = N_BUCKETS // 2
    n = np.maximum(dist, 0)
    n_f = np.maximum(n, max_exact).astype(np.float32)
    large = max_exact + (np.log(n_f / np.float32(max_exact)) / np.float32(math.log(MAX_DISTANCE / max_exact))
                         * np.float32(N_BUCKETS - max_exact)).astype(np.int32)
    return np.where(n < max_exact, n, np.minimum(large, N_BUCKETS - 1)).astype(np.int32)


def _bias_table(idx, mask, rb_ref, h):
    acc = jnp.zeros(idx.shape, F32)
    for bk in range(N_BUCKETS):
        acc = jnp.where(idx == bk, rb_ref[bk, h], acc)
    return jnp.where(mask, acc, NEG_INF)


def _softmax_sink_pv(s_heads, v_heads, sinks):
    ms = []
    for parts, sink in zip(s_heads, sinks):
        m = sink
        for s in parts:
            m = jnp.maximum(m, jnp.max(s, axis=-1, keepdims=True))
        ms.append(m)
    es = [[jnp.exp(s - m) for s in parts] for parts, m in zip(s_heads, ms)]
    dens = []
    for parts, sink, m in zip(es, sinks, ms):
        den = jnp.exp(sink - m)
        for e in parts:
            den = den + jnp.sum(e, axis=-1, keepdims=True)
        dens.append(den)
    outs = []
    for parts, vparts in zip(es, v_heads):
        o = None
        for e, (v, dot) in zip(parts, vparts):
            pv = dot(e.astype(BF16), v)
            o = pv if o is None else o + pv
        outs.append(o)
    return [o / den for o, den in zip(outs, dens)]


def _attn_prompt_kernel(q_ref, kvc_ref, kvp_ref, idx_ref, rb_ref, sink_ref, o_ref, tb_ref):
    n = HEAD_DIM
    wq = q_ref.shape[1]
    wk = 2 * wq
    first = (pl.program_id(0) == 0) & (pl.program_id(1) == 0)

    @pl.when(first)
    def _():
        qi = lax.broadcasted_iota(jnp.int32, (wq, wk), 0)
        kj = lax.broadcasted_iota(jnp.int32, (wq, wk), 1)
        dist = qi + wq - kj
        mask = (dist >= 0) & (dist < WINDOW)
        idx = idx_ref[...]
        for h in range(ATT_HEADS):
            tb_ref[h] = _bias_table(idx, mask, rb_ref, h)

    kj = lax.broadcasted_iota(jnp.int32, (wq, wk), 1)
    dead = (pl.program_id(1) == 0) & (kj < wq)
    q = q_ref[0]
    kvc = kvc_ref[0].astype(BF16)
    kvp = kvp_ref[0].astype(BF16)
    s_heads, v_heads = [], []
    for h2 in range(ATT_KV_HEADS):
        k_all = jnp.concatenate([kvp[:, h2 * n:(h2 + 1) * n], kvc[:, h2 * n:(h2 + 1) * n]], axis=0)
        v_all = jnp.concatenate([kvp[:, ATT_KV_WIDTH + h2 * n:ATT_KV_WIDTH + (h2 + 1) * n],
                                 kvc[:, ATT_KV_WIDTH + h2 * n:ATT_KV_WIDTH + (h2 + 1) * n]], axis=0)
        for g in range(ATT_GROUPS):
            h = h2 * ATT_GROUPS + g
            qh = q[:, h * n:(h + 1) * n].astype(BF16)
            s = _dot_nt(qh, k_all) * (HEAD_DIM ** -0.5) + tb_ref[h]
            s_heads.append([jnp.where(dead, NEG_INF, s)])
            v_heads.append([(v_all, _dot)])
    outs = _softmax_sink_pv(s_heads, v_heads, [sink_ref[0, h] for h in range(ATT_HEADS)])
    for h, o in enumerate(outs):
        o_ref[0, :, h * n:(h + 1) * n] = o


def _attn_prompt(q, kv, rel_bias, sinks):
    b, t, _ = q.shape
    wq = WINDOW
    dist = np.arange(wq)[:, None] + wq - np.arange(2 * wq)[None, :]
    idx = jnp.asarray(_rel_bucket_np(dist))
    smem = pl.BlockSpec(memory_space=pltpu.SMEM)
    return pl.pallas_call(
        _attn_prompt_kernel,
        grid=(b, t // wq),
        in_specs=[pl.BlockSpec((1, wq, ATT_WIDTH), lambda i, j: (i, j, 0)),
                  pl.BlockSpec((1, wq, 2 * ATT_KV_WIDTH), lambda i, j: (i, j, 0)),
                  pl.BlockSpec((1, wq, 2 * ATT_KV_WIDTH), lambda i, j: (i, jnp.maximum(j - 1, 0), 0)),
                  pl.BlockSpec(idx.shape, lambda i, j: (0, 0)),
                  smem, smem],
        out_specs=pl.BlockSpec((1, wq, ATT_WIDTH), lambda i, j: (i, j, 0)),
        out_shape=jax.ShapeDtypeStruct((b, t, ATT_WIDTH), F32),
        scratch_shapes=[pltpu.VMEM((ATT_HEADS, wq, 2 * wq), F32)],
        compiler_params=pltpu.CompilerParams(dimension_semantics=("arbitrary", "arbitrary")),
        name="attn_prompt",
    )(q, kv, kv, idx, rel_bias, sinks.reshape(1, ATT_HEADS))


def _attn_sample_kernel(q_ref, kvn_ref, kb_ref, vb_ref, idxb_ref, idxn_ref, rb_ref, sink_ref,
                        o_ref, kbo_ref, vbo_ref, tbb_ref, tbn_ref):
    n = HEAD_DIM
    bb, t, _ = q_ref.shape
    wb = kb_ref.shape[1]

    @pl.when(pl.program_id(0) == 0)
    def _():
        qi = lax.broadcasted_iota(jnp.int32, (t, wb), 0)
        kj = lax.broadcasted_iota(jnp.int32, (t, wb), 1)
        dist_b = qi + wb - kj
        mask_b = (dist_b >= 0) & (dist_b < WINDOW)
        qn = lax.broadcasted_iota(jnp.int32, (t, t), 0)
        kn = lax.broadcasted_iota(jnp.int32, (t, t), 1)
        dist_n = qn - kn
        mask_n = (dist_n >= 0) & (dist_n < WINDOW)
        for h in range(ATT_HEADS):
            tbb_ref[h] = _bias_table(idxb_ref[...], mask_b, rb_ref, h)
            tbn_ref[h] = _bias_table(idxn_ref[...], mask_n, rb_ref, h)

    q = q_ref[...]
    kvn = kvn_ref[...]
    kb = kb_ref[...]
    vb = vb_ref[...]
    kbo_ref[:, :wb - t, :] = kb[:, t:, :]
    kbo_ref[:, wb - t:, :] = kvn[:, :, :ATT_KV_WIDTH]
    vbo_ref[:, :wb - t, :] = vb[:, t:, :]
    vbo_ref[:, wb - t:, :] = kvn[:, :, ATT_KV_WIDTH:]
    bnt = lambda a, b_: jnp.einsum("bqd,bkd->bqk", a, b_, preferred_element_type=F32)
    bnn = lambda a, b_: jnp.einsum("bqk,bkd->bqd", a, b_, preferred_element_type=F32)
    s_heads, v_heads = [], []
    for h2 in range(ATT_KV_HEADS):
        ks = slice(h2 * n, (h2 + 1) * n)
        vs = slice(ATT_KV_WIDTH + h2 * n, ATT_KV_WIDTH + (h2 + 1) * n)
        k_buf = kb[:, :, ks].astype(BF16)
        v_buf = vb[:, :, ks].astype(BF16)
        k_new = kvn[:, :, ks].astype(BF16)
        v_new = kvn[:, :, vs].astype(BF16)
        for g in range(ATT_GROUPS):
            h = h2 * ATT_GROUPS + g
            qh = q[:, :, h * n:(h + 1) * n].astype(BF16)
            s_heads.append([bnt(qh, k_buf) * (HEAD_DIM ** -0.5) + tbb_ref[h][None],
                            bnt(qh, k_new) * (HEAD_DIM ** -0.5) + tbn_ref[h][None]])
            v_heads.append([(v_buf, bnn), (v_new, bnn)])
    outs = _softmax_sink_pv(s_heads, v_heads, [sink_ref[0, h] for h in range(ATT_HEADS)])
    for h, o in enumerate(outs):
        o_ref[:, :, h * n:(h + 1) * n] = o


def _attn_sample(q, kv, kbuf, vbuf, rel_bias, sinks, bb):
    b, t, _ = q.shape
    wb = kbuf.shape[1]
    dist_b = np.arange(t)[:, None] + wb - np.arange(wb)[None, :]
    dist_n = np.arange(t)[:, None] - np.arange(t)[None, :]
    idx_b = jnp.asarray(_rel_bucket_np(dist_b))
    idx_n = jnp.asarray(_rel_bucket_np(dist_n))
    smem = pl.BlockSpec(memory_space=pltpu.SMEM)
    tok = lambda w: pl.BlockSpec((bb, t, w), lambda i: (i, 0, 0))
    buf = pl.BlockSpec((bb, wb, ATT_KV_WIDTH), lambda i: (i, 0, 0))
    return pl.pallas_call(
        _attn_sample_kernel,
        grid=(b // bb,),
        in_specs=[tok(ATT_WIDTH), tok(2 * ATT_KV_WIDTH), buf, buf,
                  pl.BlockSpec(idx_b.shape, lambda i: (0, 0)),
                  pl.BlockSpec(idx_n.shape, lambda i: (0, 0)),
                  smem, smem],
        out_specs=[tok(ATT_WIDTH), buf, buf],
        out_shape=[jax.ShapeDtypeStruct((b, t, ATT_WIDTH), F32),
                   jax.ShapeDtypeStruct(kbuf.shape, F32),
                   jax.ShapeDtypeStruct(vbuf.shape, F32)],
        scratch_shapes=[pltpu.VMEM((ATT_HEADS, t, wb), F32), pltpu.VMEM((ATT_HEADS, t, t), F32)],
        compiler_params=pltpu.CompilerParams(dimension_semantics=("arbitrary",)),
        name="attn_sample",
    )(q, kv, kbuf, vbuf, idx_b, idx_n, rel_bias, sinks.reshape(1, ATT_HEADS))


def _out_kernel(x1_ref, mod_ref, y_ref, g_ref, bonus_ref, att_ref, gnw_ref, gnb_ref,
                wo_ref, w1_ref, w2_ref, lng_ref, lnb_ref, o_ref, *, d_ff):
    bb, tt, d = x1_ref.shape
    w = RWKV_WIDTH
    m = bb * tt
    x1 = x1_ref[...]
    gt2 = mod_ref[:, 5:6, :]
    sh3, sc3, gt3 = mod_ref[:, 6:7, :], mod_ref[:, 7:8, :], mod_ref[:, 8:9, :]
    y = y_ref[...].reshape(m, w)
    mu = _head_sums(y) * (1.0 / HEAD_DIM)
    dy = y - mu
    var = _head_sums(dy * dy) * (1.0 / HEAD_DIM)
    yn = dy * lax.rsqrt(var + GN_EPS) * gnw_ref[...] + gnb_ref[...]
    y_rwkv = (yn + bonus_ref[...].reshape(m, w)) * g_ref[...].reshape(m, w)
    mix = (_dot(y_rwkv.astype(BF16), wo_ref[:w, :])
           + _dot(att_ref[...].reshape(m, ATT_WIDTH).astype(BF16), wo_ref[w:, :])).reshape(bb, tt, d)
    x2 = _post_ln(ALPHA * x1 + (1.0 + gt2) * mix, lng_ref[1:2, :], lnb_ref[1:2, :])
    u = (x2 * (1.0 + sc3) + sh3).reshape(m, d).astype(BF16)
    h = _dot(u, w1_ref[...])
    act = (_silu(h[:, :d_ff]) * h[:, d_ff:]).astype(BF16)
    f2 = _dot(act, w2_ref[...]).reshape(bb, tt, d)
    o_ref[...] = _post_ln(ALPHA * x2 + 0.5 * (1.0 + gt3) * f2, lng_ref[2:3, :], lnb_ref[2:3, :])


def _out(x1, mod, y, g, bonus, att, gn_w, gn_b, wo, w1, w2, ln_g, ln_b, bb, tt):
    b, t, d = x1.shape
    d_ff = w2.shape[0]
    tok = lambda w: pl.BlockSpec((bb, tt, w), lambda i, j: (i, j, 0))
    return pl.pallas_call(
        functools.partial(_out_kernel, d_ff=d_ff),
        grid=(b // bb, t // tt),
        in_specs=[tok(d),
                  pl.BlockSpec((bb, N_MOD, d), lambda i, j: (i, 0, 0)),
                  tok(RWKV_WIDTH), tok(RWKV_WIDTH), tok(RWKV_WIDTH), tok(ATT_WIDTH),
                  _const_spec(gn_w.shape), _const_spec(gn_b.shape),
                  _const_spec(wo.shape), _const_spec(w1.shape), _const_spec(w2.shape),
                  _const_spec(ln_g.shape), _const_spec(ln_b.shape)],
        out_specs=tok(d),
        out_shape=jax.ShapeDtypeStruct((b, t, d), F32),
        compiler_params=pltpu.CompilerParams(
            dimension_semantics=("arbitrary", "arbitrary"), vmem_limit_bytes=VMEM_LIMIT_BYTES),
        name="mix_ln_ffn2",
    )(x1, mod, y, g, bonus, att, gn_w, gn_b, wo, w1, w2, ln_g, ln_b)


def _layer(x, mod, wkv0, shift0, kbuf, vbuf, wts, bb, tt, wkv_bb, wkv_c, wkv_group):
    b, t, d = x.shape
    x1, q, kv, shift_new, r, lw, kt, v, kap, bvec, g, bonus = _ffn1(
        x, mod, shift0[:, None, :], wts["w1a"], wts["w2a"], wts["win"], wts["ln_g"], wts["ln_b"],
        wts["mu"], wts["w0"], wts["a0"], wts["k_k"], wts["k_a"], wts["r_k"], wts["w_lora"], bb, tt)
    y, s_fin = _wkv(r, lw, kt, v, kap, bvec, wkv0, wkv_bb, wkv_c, wkv_group)
    shift_new = shift_new[:, 0, :]
    if kbuf is None:
        att = _attn_prompt(q, kv, wts["rel_bias"], wts["sinks"])
        wb = WINDOW
        kb_new = kv[:, t - wb:, :ATT_KV_WIDTH]
        vb_new = kv[:, t - wb:, ATT_KV_WIDTH:]
    else:
        att, kb_new, vb_new = _attn_sample(q, kv, kbuf, vbuf, wts["rel_bias"], wts["sinks"], bb)
    out = _out(x1, mod, y, g, bonus, att, wts["gn_w"], wts["gn_b"], wts["wo"],
               wts["w1b"], wts["w2b"], wts["ln_g"], wts["ln_b"], bb, tt)
    return out, s_fin, shift_new, kb_new, vb_new


def kernel(x_prompt, x_sample, state_wkv, state_shift, cache_win_k, cache_win_v, c_prompt, c_sample, rel_bias, w_ada, b_ada, ln_g, ln_b, w_ffn1_in, w_ffn1_out, w_in, mu_shift, w0, w_decay, a0, w_iclr, w_gate, k_k, k_a, r_k, gn_w, gn_b, sinks, w_out, w_ffn2_in, w_ffn2_out):
    bp, tp, d = x_prompt.shape
    bs, ts, _ = x_sample.shape
    depth = w_ada.shape[0]
    assert depth == 1
    l = 0
    w = RWKV_WIDTH
    wb = cache_win_k.shape[2]

    mod = _ada(jnp.concatenate([c_prompt, c_sample], axis=0), w_ada[l], b_ada[l])
    mod = mod.reshape(bp + bs, N_MOD, d)
    mod_p, mod_s = mod[:bp], mod[bp:]

    w_lora = jnp.zeros((LORA_WIDTH, 3 * w), F32)
    w_lora = w_lora.at[:DECAY_LORA, :w].set(w_decay[l])
    w_lora = w_lora.at[DECAY_LORA:DECAY_LORA + ICLR_LORA, w:2 * w].set(w_iclr[l])
    w_lora = w_lora.at[DECAY_LORA + ICLR_LORA:, 2 * w:].set(w_gate[l])
    row = lambda z: z.reshape(1, -1)
    wts = dict(
        w1a=w_ffn1_in[l].astype(BF16), w2a=w_ffn1_out[l].astype(BF16), win=w_in[l].astype(BF16),
        w1b=w_ffn2_in[l].astype(BF16), w2b=w_ffn2_out[l].astype(BF16), wo=w_out[l].astype(BF16),
        ln_g=ln_g[l], ln_b=ln_b[l], mu=row(mu_shift[l]), w0=row(w0[l]), a0=row(a0[l]),
        k_k=row(k_k[l]), k_a=row(k_a[l]), r_k=row(r_k[l]), w_lora=w_lora.astype(BF16),
        gn_w=row(gn_w[l]), gn_b=row(gn_b[l]), rel_bias=rel_bias, sinks=sinks[l])

    wkv0_p = jnp.zeros((bp, RWKV_HEADS, HEAD_DIM, HEAD_DIM), F32)
    shift0_p = jnp.zeros((bp, RWKV_PROJ), F32)
    y_p, wkv_p, sh_p, k_p, v_p = _layer(x_prompt, mod_p, wkv0_p, shift0_p, None, None, wts,
                                        1, ROW_TILE, bp, WKV_CHUNK, bp)
    kbuf = cache_win_k[l].reshape(bs, wb, ATT_KV_WIDTH)
    vbuf = cache_win_v[l].reshape(bs, wb, ATT_KV_WIDTH)
    y_s, wkv_s, sh_s, k_s, v_s = _layer(x_sample, mod_s, state_wkv[l], state_shift[l], kbuf, vbuf, wts,
                                        ROW_TILE // ts, ts, 8, ts, 2)
    kvshape = lambda z: z.reshape(1, z.shape[0], wb, ATT_KV_HEADS, HEAD_DIM)
    return (y_p, y_s, wkv_p[None], sh_p[None], kvshape(k_p), kvshape(v_p),
            wkv_s[None], sh_s[None], kvshape(k_s), kvshape(v_s))
```

```python
import functools
import math

import numpy as np
import jax
import jax.numpy as jnp
from jax import lax
from jax.experimental import pallas as pl
from jax.experimental.pallas import tpu as pltpu

F32 = jnp.float32
BF16 = jnp.bfloat16

HEAD_DIM = 64
RWKV_HEADS = 8
RWKV_WIDTH = RWKV_HEADS * HEAD_DIM
ATT_HEADS = 8
ATT_KV_HEADS = 2
ATT_GROUPS = ATT_HEADS // ATT_KV_HEADS
ATT_WIDTH = ATT_HEADS * HEAD_DIM
ATT_KV_WIDTH = ATT_KV_HEADS * HEAD_DIM
DECAY_LORA = 64
ICLR_LORA = 64
GATE_LORA = 128
LORA_WIDTH = DECAY_LORA + ICLR_LORA + GATE_LORA
RWKV_PROJ = 3 * RWKV_WIDTH + LORA_WIDTH
WINDOW = 128
N_BUCKETS = 32
MAX_DISTANCE = 128
N_MOD = 9
DEPTH = 1
ALPHA = (2 * DEPTH) ** 0.25
LN_EPS = 1e-5
GN_EPS = 64e-5
NEG_INF = -1e30

VMEM_LIMIT_BYTES = 56 * 1024 * 1024
ROW_TILE = 256
SUB_TILES = 2
WKV_CHUNK = 64


def _dot(a, b):
    return jnp.dot(a, b, preferred_element_type=F32)


def _dot_nt(a, b):
    return lax.dot_general(a, b, (((1,), (1,)), ((), ())), preferred_element_type=F32)


def _dot_tn(a, b):
    return lax.dot_general(a, b, (((0,), (0,)), ((), ())), preferred_element_type=F32)


def _split2(a):
    hi = a.astype(BF16)
    lo = (a - hi.astype(F32)).astype(BF16)
    return hi, lo


def _dot3(asp, bsp, dot):
    (ah, al), (bh, bl) = asp, bsp
    return dot(ah, bh) + dot(ah, bl) + dot(al, bh)


def _head_sums(x):
    pair = 2 * HEAD_DIM
    low = lax.broadcasted_iota(jnp.int32, (x.shape[0], pair), 1) < HEAD_DIM
    outs = []
    for p in range(x.shape[1] // pair):
        xs = x[:, p * pair:(p + 1) * pair]
        s_lo = jnp.sum(jnp.where(low, xs, 0.0), axis=-1, keepdims=True)
        s_hi = jnp.sum(jnp.where(low, 0.0, xs), axis=-1, keepdims=True)
        outs.append(jnp.where(low, s_lo, s_hi))
    return jnp.concatenate(outs, axis=-1)


def _silu(x):
    return x * jax.nn.sigmoid(x)


def _post_ln(h, g, b):
    mu = jnp.mean(h, axis=-1, keepdims=True)
    d = h - mu
    var = jnp.mean(d * d, axis=-1, keepdims=True)
    return d * lax.rsqrt(var + LN_EPS) * g + b


def _const_spec(shape):
    nd = len(shape)
    return pl.BlockSpec(shape, lambda *_: (0,) * nd, pipeline_mode=pl.Buffered(1))


def _ada_kernel(c_ref, w_ref, b_ref, o_ref):
    s = _silu(c_ref[...]).astype(BF16)
    o_ref[...] = _dot(s, w_ref[...].astype(BF16)) + b_ref[...]


def _ada(c_all, w_ada, b_ada):
    nb, d = c_all.shape
    n = w_ada.shape[1]
    tn = d
    return pl.pallas_call(
        _ada_kernel,
        grid=(n // tn,),
        in_specs=[pl.BlockSpec((nb, d), lambda j: (0, 0)),
                  pl.BlockSpec((d, tn), lambda j: (0, j)),
                  pl.BlockSpec((1, tn), lambda j: (0, j))],
        out_specs=pl.BlockSpec((nb, tn), lambda j: (0, j)),
        out_shape=jax.ShapeDtypeStruct((nb, n), F32),
        name="ada_mod",
    )(c_all, w_ada, b_ada.reshape(1, n))


def _sub_tiles(bb, tt, n):
    if bb >= n:
        step = bb // n
        return [(slice(s * step, (s + 1) * step), slice(0, tt), False) for s in range(n)]
    step = tt // n
    return [(slice(0, bb), slice(s * step, (s + 1) * step), True) for s in range(n)]


def _ffn1_kernel(x_ref, mod_ref, shift0_ref, w1_ref, w2_ref, win_ref, lng_ref, lnb_ref,
                 mu_ref, w0_ref, a0_ref, kk_ref, ka_ref, rk_ref, wl_ref,
                 x1_ref, q_ref, kv_ref, shift_ref, *rest, d_ff):
    prep_refs, carry_ref = rest[:-1], rest[-1]
    bb, tt, d = x_ref.shape

    @pl.when(pl.program_id(1) == 0)
    def _():
        carry_ref[...] = shift0_ref[...]

    carry = carry_ref[...]
    for bs, rs, by_rows in _sub_tiles(bb, tt, SUB_TILES):
        x = x_ref[bs, rs, :]
        sb, st, _ = x.shape
        mod = lambda k: mod_ref[bs, k:k + 1, :]
        u = (x * (1.0 + mod(1)) + mod(0)).reshape(sb * st, d).astype(BF16)
        h = _dot(u, w1_ref[...])
        act = (_silu(h[:, :d_ff]) * h[:, d_ff:]).astype(BF16)
        f1 = _dot(act, w2_ref[...]).reshape(sb, st, d)
        x1 = _post_ln(ALPHA * x + 0.5 * (1.0 + mod(2)) * f1, lng_ref[0:1, :], lnb_ref[0:1, :])
        x1_ref[bs, rs, :] = x1
        u2 = (x1 * (1.0 + mod(4)) + mod(3)).reshape(sb * st, d).astype(BF16)
        p = _dot(u2, win_ref[...])
        q_ref[bs, rs, :] = p[:, RWKV_PROJ:RWKV_PROJ + ATT_WIDTH].reshape(sb, st, ATT_WIDTH)
        kv_ref[bs, rs, :] = p[:, RWKV_PROJ + ATT_WIDTH:].reshape(sb, st, 2 * ATT_KV_WIDTH)
        p_rwkv = p[:, :RWKV_PROJ].reshape(sb, st, RWKV_PROJ)
        first = carry if by_rows else carry[bs]
        last = p_rwkv[:, st - 1:st, :]
        if by_rows:
            carry = last
        else:
            carry_ref[bs] = last
            shift_ref[bs] = last
        outs = _rwkv_prep(p_rwkv, first, mu_ref[...], w0_ref[...], a0_ref[...], kk_ref[...], ka_ref[...],
                          rk_ref[...], wl_ref[...])
        for ref, val in zip(prep_refs, outs):
            ref[bs, rs, :] = val.reshape(sb, st, RWKV_WIDTH)
    if by_rows:
        carry_ref[...] = carry
        shift_ref[...] = carry


def _ffn1(x, mod, shift0, w1, w2, win, ln_g, ln_b, mu, w0, a0, k_k, k_a, r_k, w_lora, bb, tt):
    b, t, d = x.shape
    d_ff = w2.shape[0]
    w = RWKV_WIDTH
    tok = lambda width: pl.BlockSpec((bb, tt, width), lambda i, j: (i, j, 0))
    per_seq = pl.BlockSpec((bb, 1, RWKV_PROJ), lambda i, j: (i, 0, 0))
    consts = [w1, w2, win, ln_g, ln_b, mu, w0, a0, k_k, k_a, r_k, w_lora]
    return pl.pallas_call(
        functools.partial(_ffn1_kernel, d_ff=d_ff),
        grid=(b // bb, t // tt),
        in_specs=[tok(d), pl.BlockSpec((bb, N_MOD, d), lambda i, j: (i, 0, 0)), per_seq]
                 + [_const_spec(c.shape) for c in consts],
        out_specs=[tok(d), tok(ATT_WIDTH), tok(2 * ATT_KV_WIDTH), per_seq] + [tok(w)] * 8,
        out_shape=[jax.ShapeDtypeStruct((b, t, d), F32),
                   jax.ShapeDtypeStruct((b, t, ATT_WIDTH), F32),
                   jax.ShapeDtypeStruct((b, t, 2 * ATT_KV_WIDTH), F32),
                   jax.ShapeDtypeStruct((b, 1, RWKV_PROJ), F32)]
                  + [jax.ShapeDtypeStruct((b, t, w), F32)] * 8,
        scratch_shapes=[pltpu.VMEM((bb, 1, RWKV_PROJ), F32)],
        compiler_params=pltpu.CompilerParams(
            dimension_semantics=("arbitrary", "arbitrary"), vmem_limit_bytes=VMEM_LIMIT_BYTES),
        name="ffn1_ln_inproj_prep",
    )(x, mod, shift0, *consts)


def _rwkv_prep(p, first, mu, w0, a0, k_k, k_a, r_k, w_lora):
    bb, tt, pw = p.shape
    w = RWKV_WIDTH
    rolled = pltpu.roll(p, 1, axis=1)
    row = lax.broadcasted_iota(jnp.int32, p.shape, 1)
    prev = jnp.where(row == 0, first, rolled)
    xs = (p + (prev - p) * mu).reshape(bb * tt, pw)
    r, k, v = xs[:, :w], xs[:, w:2 * w], xs[:, 2 * w:3 * w]
    lo = xs[:, 3 * w:]
    lane = lax.broadcasted_iota(jnp.int32, lo.shape, 1)
    nl = jnp.where(lane < DECAY_LORA, jnp.tanh(lo),
                   jnp.where(lane < DECAY_LORA + ICLR_LORA, lo, jax.nn.sigmoid(lo)))
    lora = _dot(nl.astype(BF16), w_lora)
    z = -(w0 + lora[:, :w])
    softplus = jnp.maximum(z, 0.0) + jnp.log(1.0 + jnp.exp(-jnp.abs(z)))
    lw = -jnp.exp(-softplus - 0.5)
    a = jax.nn.sigmoid(a0 + lora[:, w:2 * w])
    g = lora[:, 2 * w:]
    kk = k * k_k
    kap = kk / jnp.maximum(jnp.sqrt(_head_sums(kk * kk)), 1e-12)
    kt = k * (1.0 + (a - 1.0) * k_a)
    bonus = _head_sums(r * kt * r_k) * v
    return r, lw, kt, v, kap, kap * a, g, bonus


def _wkv_units(full, s_list, decays, tri, c):
    strict, incl, levels, eye = tri
    rng = range(len(s_list))
    ss = [_split2(s) for s in s_list]
    gb = [_dot3(full[u]["kr"], full[u]["bi"], _dot_nt) for u in rng]
    gk = [_dot3(full[u]["kr"], full[u]["ki"], _dot_nt) for u in rng]
    st = [_dot3(full[u]["kr"], ss[u], _dot_nt) for u in rng]
    mab = [jnp.where(strict, g[:c], 0.0) for g in gb]
    arb = [_split2(jnp.where(incl, g[c:], 0.0)) for g in gb]
    mk = [_split2(jnp.concatenate([jnp.where(strict, g[:c], 0.0), jnp.where(incl, g[c:], 0.0)], axis=0))
          for g in gk]
    mv = [_dot3(mk[u], full[u]["v"], _dot) for u in rng]
    x = [eye - jnp.where(levels[0], m, 0.0) for m in mab]
    for lvl in levels[1:]:
        xs = [_split2(xi) for xi in x]
        bm = [_split2(jnp.where(lvl, m, 0.0)) for m in mab]
        t = [_split2(_dot3(xs[u], bm[u], _dot)) for u in rng]
        x = [x[u] - _dot3(t[u], xs[u], _dot) for u in rng]
    sa = [_dot3(_split2(x[u]), _split2(st[u][:c] + mv[u][:c]), _dot) for u in rng]
    sas = [_split2(s) for s in sa]
    y = [st[u][c:] + mv[u][c:] - _dot3(arb[u], sas[u], _dot) for u in rng]
    vsa = [_split2(jnp.concatenate([full[u]["vf"], -sa[u]], axis=0)) for u in rng]
    s_new = [s_list[u] * decays[u] + _dot3(vsa[u], full[u]["keb"], _dot_tn) for u in rng]
    return y, s_new


def _wkv_kernel(r_ref, lw_ref, kt_ref, v_ref, kap_ref, b_ref, s0_ref, y_ref, sfin_ref, s_scr, *, group):
    bb, c, w = r_ref.shape
    n = HEAD_DIM
    nh = w // n
    j = pl.program_id(1)

    @pl.when(j == 0)
    def _():
        s_scr[...] = s0_ref[...]

    ti = lax.broadcasted_iota(jnp.int32, (c, c), 0)
    ji = lax.broadcasted_iota(jnp.int32, (c, c), 1)
    levels = []
    s = 1
    while s < c:
        same = ((ti ^ ji) & ~(2 * s - 1)) == 0
        levels.append(same & ((ti & s) != 0) & ((ji & s) == 0))
        s *= 2
    tri = (ti > ji, ti >= ji, levels, (ti == ji).astype(F32))
    row = lax.broadcasted_iota(jnp.int32, (c, w), 0)

    def body(gi, carry):
        full, s_list, decays, where = [], [], [], []
        for k in range(group):
            bi_ = gi * group + k
            lw = lw_ref[bi_]
            cl = lw
            s_ = 1
            while s_ < c:
                cl = cl + jnp.where(row >= s_, pltpu.roll(cl, s_, axis=0), 0.0)
                s_ *= 2
            last = cl[c - 1:c, :]
            e_inv = jnp.exp(-cl)
            e_end = jnp.exp(last - cl)
            kt = kt_ref[bi_]
            bv = b_ref[bi_]
            v = v_ref[bi_]
            wide = dict(
                kr=_split2(jnp.concatenate([kap_ref[bi_] * jnp.exp(cl - lw), r_ref[bi_] * jnp.exp(cl)], axis=0)),
                bi=_split2(bv * e_inv), ki=_split2(kt * e_inv), v=_split2(v),
                keb=_split2(jnp.concatenate([kt * e_end, bv * e_end], axis=0)))
            decay = jnp.exp(last)
            for h in range(nh):
                sl = slice(h * n, (h + 1) * n)
                u = {key: (hi[:, sl], lo[:, sl]) for key, (hi, lo) in wide.items()}
                u["vf"] = v[:, sl]
                full.append(u)
                s_list.append(s_scr[bi_, h])
                decays.append(decay[:, sl])
                where.append((bi_, h, sl))
        ys, s_new = _wkv_units(full, s_list, decays, tri, c)
        for (bi_, h, sl), y, sn in zip(where, ys, s_new):
            s_scr[bi_, h] = sn
            y_ref[bi_, :, sl] = y
        return carry

    if bb == group:
        body(0, 0)
    else:
        lax.fori_loop(0, bb // group, body, 0)

    @pl.when(j == pl.num_programs(1) - 1)
    def _():
        sfin_ref[...] = s_scr[...]


def _wkv(r, lw, kt, v, kap, bvec, s0, bb, c, group):
    b, t, w = r.shape
    h = w // HEAD_DIM
    tok = pl.BlockSpec((bb, c, w), lambda i, j: (i, j, 0))
    st = pl.BlockSpec((bb, h, HEAD_DIM, HEAD_DIM), lambda i, j: (i, 0, 0, 0))
    return pl.pallas_call(
        functools.partial(_wkv_kernel, group=group),
        grid=(b // bb, t // c),
        in_specs=[tok] * 6 + [st],
        out_specs=[tok, st],
        out_shape=[jax.ShapeDtypeStruct((b, t, w), F32),
                   jax.ShapeDtypeStruct((b, h, HEAD_DIM, HEAD_DIM), F32)],
        scratch_shapes=[pltpu.VMEM((bb, h, HEAD_DIM, HEAD_DIM), F32)],
        compiler_params=pltpu.CompilerParams(dimension_semantics=("arbitrary", "arbitrary")),
        name="wkv_scan",
    )(r, lw, kt, v, kap, bvec, s0)


def _rel_bucket_np(dist):
    max_exact = N_BUCKETS // 2
    n = np.maximum(dist, 0)
    n_f = np.maximum(n, max_exact).astype(np.float32)
    large = max_exact + (np.log(n_f / np.float32(max_exact)) / np.float32(math.log(MAX_DISTANCE / max_exact))
                         * np.float32(N_BUCKETS - max_exact)).astype(np.int32)
    return np.where(n < max_exact, n, np.minimum(large, N_BUCKETS - 1)).astype(np.int32)


def _bias_table(idx, mask, rb_ref, h):
    acc = jnp.zeros(idx.shape, F32)
    for bk in range(N_BUCKETS):
        acc = jnp.where(idx == bk, rb_ref[bk, h], acc)
    return jnp.where(mask, acc, NEG_INF)


def _softmax_sink_pv(s_heads, v_heads, sinks):
    ms = []
    for parts, sink in zip(s_heads, sinks):
        m = sink
        for s in parts:
            m = jnp.maximum(m, jnp.max(s, axis=-1, keepdims=True))
        ms.append(m)
    es = [[jnp.exp(s - m) for s in parts] for parts, m in zip(s_heads, ms)]
    dens = []
    for parts, sink, m in zip(es, sinks, ms):
        den = jnp.exp(sink - m)
        for e in parts:
            den = den + jnp.sum(e, axis=-1, keepdims=True)
        dens.append(den)
    outs = []
    for parts, vparts in zip(es, v_heads):
        o = None
        for e, (v, dot) in zip(parts, vparts):
            pv = dot(e.astype(BF16), v)
            o = pv if o is None else o + pv
        outs.append(o)
    return [o / den for o, den in zip(outs, dens)]


def _attn_prompt_kernel(q_ref, kvc_ref, kvp_ref, idx_ref, rb_ref, sink_ref, o_ref, tb_ref):
    n = HEAD_DIM
    wq = q_ref.shape[1]
    wk = 2 * wq
    first = (pl.program_id(0) == 0) & (pl.program_id(1) == 0)

    @pl.when(first)
    def _():
        qi = lax.broadcasted_iota(jnp.int32, (wq, wk), 0)
        kj = lax.broadcasted_iota(jnp.int32, (wq, wk), 1)
        dist = qi + wq - kj
        mask = (dist >= 0) & (dist < WINDOW)
        idx = idx_ref[...]
        for h in range(ATT_HEADS):
            tb_ref[h] = _bias_table(idx, mask, rb_ref, h)

    kj = lax.broadcasted_iota(jnp.int32, (wq, wk), 1)
    dead = (pl.program_id(1) == 0) & (kj < wq)
    q = q_ref[0]
    kvc = kvc_ref[0].astype(BF16)
    kvp = kvp_ref[0].astype(BF16)
    s_heads, v_heads = [], []
    for h2 in range(ATT_KV_HEADS):
        k_all = jnp.concatenate([kvp[:, h2 * n:(h2 + 1) * n], kvc[:, h2 * n:(h2 + 1) * n]], axis=0)
        v_all = jnp.concatenate([kvp[:, ATT_KV_WIDTH + h2 * n:ATT_KV_WIDTH + (h2 + 1) * n],
                                 kvc[:, ATT_KV_WIDTH + h2 * n:ATT_KV_WIDTH + (h2 + 1) * n]], axis=0)
        for g in range(ATT_GROUPS):
            h = h2 * ATT_GROUPS + g
            qh = q[:, h * n:(h + 1) * n].astype(BF16)
            s = _dot_nt(qh, k_all) * (HEAD_DIM ** -0.5) + tb_ref[h]
            s_heads.append([jnp.where(dead, NEG_INF, s)])
            v_heads.append([(v_all, _dot)])
    outs = _softmax_sink_pv(s_heads, v_heads, [sink_ref[0, h] for h in range(ATT_HEADS)])
    for h, o in enumerate(outs):
        o_ref[0, :, h * n:(h + 1) * n] = o


def _attn_prompt(q, kv, rel_bias, sinks):
    b, t, _ = q.shape
    wq = WINDOW
    dist = np.arange(wq)[:, None] + wq - np.arange(2 * wq)[None, :]
    idx = jnp.asarray(_rel_bucket_np(dist))
    smem = pl.BlockSpec(memory_space=pltpu.SMEM)
    return pl.pallas_call(
        _attn_prompt_kernel,
        grid=(b, t // wq),
        in_specs=[pl.BlockSpec((1, wq, ATT_WIDTH), lambda i, j: (i, j, 0)),
                  pl.BlockSpec((1, wq, 2 * ATT_KV_WIDTH), lambda i, j: (i, j, 0)),
                  pl.BlockSpec((1, wq, 2 * ATT_KV_WIDTH), lambda i, j: (i, jnp.maximum(j - 1, 0), 0)),
                  pl.BlockSpec(idx.shape, lambda i, j: (0, 0)),
                  smem, smem],
        out_specs=pl.BlockSpec((1, wq, ATT_WIDTH), lambda i, j: (i, j, 0)),
        out_shape=jax.ShapeDtypeStruct((b, t, ATT_WIDTH), F32),
        scratch_shapes=[pltpu.VMEM((ATT_HEADS, wq, 2 * wq), F32)],
        compiler_params=pltpu.CompilerParams(dimension_semantics=("arbitrary", "arbitrary")),
        name="attn_prompt",
    )(q, kv, kv, idx, rel_bias, sinks.reshape(1, ATT_HEADS))


def _attn_sample_kernel(q_ref, kvn_ref, kb_ref, vb_ref, idxb_ref, idxn_ref, rb_ref, sink_ref,
                        o_ref, kbo_ref, vbo_ref, tbb_ref, tbn_ref):
    n = HEAD_DIM
    bb, t, _ = q_ref.shape
    wb = kb_ref.shape[1]

    @pl.when(pl.program_id(0) == 0)
    def _():
        qi = lax.broadcasted_iota(jnp.int32, (t, wb), 0)
        kj = lax.broadcasted_iota(jnp.int32, (t, wb), 1)
        dist_b = qi + wb - kj
        mask_b = (dist_b >= 0) & (dist_b < WINDOW)
        qn = lax.broadcasted_iota(jnp.int32, (t, t), 0)
        kn = lax.broadcasted_iota(jnp.int32, (t, t), 1)
        dist_n = qn - kn
        mask_n = (dist_n >= 0) & (dist_n < WINDOW)
        for h in range(ATT_HEADS):
            tbb_ref[h] = _bias_table(idxb_ref[...], mask_b, rb_ref, h)
            tbn_ref[h] = _bias_table(idxn_ref[...], mask_n, rb_ref, h)

    q = q_ref[...]
    kvn = kvn_ref[...]
    kb = kb_ref[...]
    vb = vb_ref[...]
    kbo_ref[:, :wb - t, :] = kb[:, t:, :]
    kbo_ref[:, wb - t:, :] = kvn[:, :, :ATT_KV_WIDTH]
    vbo_ref[:, :wb - t, :] = vb[:, t:, :]
    vbo_ref[:, wb - t:, :] = kvn[:, :, ATT_KV_WIDTH:]
    bnt = lambda a, b_: jnp.einsum("bqd,bkd->bqk", a, b_, preferred_element_type=F32)
    bnn = lambda a, b_: jnp.einsum("bqk,bkd->bqd", a, b_, preferred_element_type=F32)
    s_heads, v_heads = [], []
    for h2 in range(ATT_KV_HEADS):
        ks = slice(h2 * n, (h2 + 1) * n)
        vs = slice(ATT_KV_WIDTH + h2 * n, ATT_KV_WIDTH + (h2 + 1) * n)
        k_buf = kb[:, :, ks].astype(BF16)
        v_buf = vb[:, :, ks].astype(BF16)
        k_new = kvn[:, :, ks].astype(BF16)
        v_new = kvn[:, :, vs].astype(BF16)
        for g in range(ATT_GROUPS):
            h = h2 * ATT_GROUPS + g
            qh = q[:, :, h * n:(h + 1) * n].astype(BF16)
            s_heads.append([bnt(qh, k_buf) * (HEAD_DIM ** -0.5) + tbb_ref[h][None],
                            bnt(qh, k_new) * (HEAD_DIM ** -0.5) + tbn_ref[h][None]])
            v_heads.append([(v_buf, bnn), (v_new, bnn)])
    outs = _softmax_sink_pv(s_heads, v_heads, [sink_ref[0, h] for h in range(ATT_HEADS)])
    for h, o in enumerate(outs):
        o_ref[:, :, h * n:(h + 1) * n] = o


def _attn_sample(q, kv, kbuf, vbuf, rel_bias, sinks, bb):
    b, t, _ = q.shape
    wb = kbuf.shape[1]
    dist_b = np.arange(t)[:, None] + wb - np.arange(wb)[None, :]
    dist_n = np.arange(t)[:, None] - np.arange(t)[None, :]
    idx_b = jnp.asarray(_rel_bucket_np(dist_b))
    idx_n = jnp.asarray(_rel_bucket_np(dist_n))
    smem = pl.BlockSpec(memory_space=pltpu.SMEM)
    tok = lambda w: pl.BlockSpec((bb, t, w), lambda i: (i, 0, 0))
    buf = pl.BlockSpec((bb, wb, ATT_KV_WIDTH), lambda i: (i, 0, 0))
    return pl.pallas_call(
        _attn_sample_kernel,
        grid=(b // bb,),
        in_specs=[tok(ATT_WIDTH), tok(2 * ATT_KV_WIDTH), buf, buf,
                  pl.BlockSpec(idx_b.shape, lambda i: (0, 0)),
                  pl.BlockSpec(idx_n.shape, lambda i: (0, 0)),
                  smem, smem],
        out_specs=[tok(ATT_WIDTH), buf, buf],
        out_shape=[jax.ShapeDtypeStruct((b, t, ATT_WIDTH), F32),
                   jax.ShapeDtypeStruct(kbuf.shape, F32),
                   jax.ShapeDtypeStruct(vbuf.shape, F32)],
        scratch_shapes=[pltpu.VMEM((ATT_HEADS, t, wb), F32), pltpu.VMEM((ATT_HEADS, t, t), F32)],
        compiler_params=pltpu.CompilerParams(dimension_semantics=("arbitrary",)),
        name="attn_sample",
    )(q, kv, kbuf, vbuf, idx_b, idx_n, rel_bias, sinks.reshape(1, ATT_HEADS))


def _out_kernel(x1_ref, mod_ref, y_ref, g_ref, bonus_ref, att_ref, gnw_ref, gnb_ref,
                wo_ref, w1_ref, w2_ref, lng_ref, lnb_ref, o_ref, *, d_ff):
    bb, tt, d = x1_ref.shape
    w = RWKV_WIDTH
    for bs, rs, _ in _sub_tiles(bb, tt, SUB_TILES):
        x1 = x1_ref[bs, rs, :]
        sb, st, _ = x1.shape
        m = sb * st
        mod = lambda k: mod_ref[bs, k:k + 1, :]
        y = y_ref[bs, rs, :].reshape(m, w)
        mu = _head_sums(y) * (1.0 / HEAD_DIM)
        dy = y - mu
        var = _head_sums(dy * dy) * (1.0 / HEAD_DIM)
        yn = dy * lax.rsqrt(var + GN_EPS) * gnw_ref[...] + gnb_ref[...]
        y_rwkv = (yn + bonus_ref[bs, rs, :].reshape(m, w)) * g_ref[bs, rs, :].reshape(m, w)
        mix = (_dot(y_rwkv.astype(BF16), wo_ref[:w, :])
               + _dot(att_ref[bs, rs, :].reshape(m, ATT_WIDTH).astype(BF16), wo_ref[w:, :])).reshape(sb, st, d)
        x2 = _post_ln(ALPHA * x1 + (1.0 + mod(5)) * mix, lng_ref[1:2, :], lnb_ref[1:2, :])
        u = (x2 * (1.0 + mod(7)) + mod(6)).reshape(m, d).astype(BF16)
        h = _dot(u, w1_ref[...])
        act = (_silu(h[:, :d_ff]) * h[:, d_ff:]).astype(BF16)
        f2 = _dot(act, w2_ref[...]).reshape(sb, st, d)
        o_ref[bs, rs, :] = _post_ln(ALPHA * x2 + 0.5 * (1.0 + mod(8)) * f2, lng_ref[2:3, :], lnb_ref[2:3, :])


def _out(x1, mod, y, g, bonus, att, gn_w, gn_b, wo, w1, w2, ln_g, ln_b, bb, tt):
    b, t, d = x1.shape
    d_ff = w2.shape[0]
    tok = lambda w: pl.BlockSpec((bb, tt, w), lambda i, j: (i, j, 0))
    return pl.pallas_call(
        functools.partial(_out_kernel, d_ff=d_ff),
        grid=(b // bb, t // tt),
        in_specs=[tok(d),
                  pl.BlockSpec((bb, N_MOD, d), lambda i, j: (i, 0, 0)),
                  tok(RWKV_WIDTH), tok(RWKV_WIDTH), tok(RWKV_WIDTH), tok(ATT_WIDTH),
                  _const_spec(gn_w.shape), _const_spec(gn_b.shape),
                  _const_spec(wo.shape), _const_spec(w1.shape), _const_spec(w2.shape),
                  _const_spec(ln_g.shape), _const_spec(ln_b.shape)],
        out_specs=tok(d),
        out_shape=jax.ShapeDtypeStruct((b, t, d), F32),
        compiler_params=pltpu.CompilerParams(
            dimension_semantics=("arbitrary", "arbitrary"), vmem_limit_bytes=VMEM_LIMIT_BYTES),
        name="mix_ln_ffn2",
    )(x1, mod, y, g, bonus, att, gn_w, gn_b, wo, w1, w2, ln_g, ln_b)


def _layer(x, mod, wkv0, shift0, kbuf, vbuf, wts, bb, tt, wkv_bb, wkv_c, wkv_group):
    b, t, d = x.shape
    x1, q, kv, shift_new, r, lw, kt, v, kap, bvec, g, bonus = _ffn1(
        x, mod, shift0[:, None, :], wts["w1a"], wts["w2a"], wts["win"], wts["ln_g"], wts["ln_b"],
        wts["mu"], wts["w0"], wts["a0"], wts["k_k"], wts["k_a"], wts["r_k"], wts["w_lora"], bb, tt)
    y, s_fin = _wkv(r, lw, kt, v, kap, bvec, wkv0, wkv_bb, wkv_c, wkv_group)
    shift_new = shift_new[:, 0, :]
    if kbuf is None:
        att = _attn_prompt(q, kv, wts["rel_bias"], wts["sinks"])
        wb = WINDOW
        kb_new = kv[:, t - wb:, :ATT_KV_WIDTH]
        vb_new = kv[:, t - wb:, ATT_KV_WIDTH:]
    else:
        att, kb_new, vb_new = _attn_sample(q, kv, kbuf, vbuf, wts["rel_bias"], wts["sinks"], bb)
    out = _out(x1, mod, y, g, bonus, att, wts["gn_w"], wts["gn_b"], wts["wo"],
               wts["w1b"], wts["w2b"], wts["ln_g"], wts["ln_b"], bb, tt)
    return out, s_fin, shift_new, kb_new, vb_new


def kernel(x_prompt, x_sample, state_wkv, state_shift, cache_win_k, cache_win_v, c_prompt, c_sample, rel_bias, w_ada, b_ada, ln_g, ln_b, w_ffn1_in, w_ffn1_out, w_in, mu_shift, w0, w_decay, a0, w_iclr, w_gate, k_k, k_a, r_k, gn_w, gn_b, sinks, w_out, w_ffn2_in, w_ffn2_out):
    bp, tp, d = x_prompt.shape
    bs, ts, _ = x_sample.shape
    depth = w_ada.shape[0]
    assert depth == 1
    l = 0
    w = RWKV_WIDTH
    wb = cache_win_k.shape[2]

    mod = _ada(jnp.concatenate([c_prompt, c_sample], axis=0), w_ada[l], b_ada[l])
    mod = mod.reshape(bp + bs, N_MOD, d)
    mod_p, mod_s = mod[:bp], mod[bp:]

    w_lora = jnp.zeros((LORA_WIDTH, 3 * w), F32)
    w_lora = w_lora.at[:DECAY_LORA, :w].set(w_decay[l])
    w_lora = w_lora.at[DECAY_LORA:DECAY_LORA + ICLR_LORA, w:2 * w].set(w_iclr[l])
    w_lora = w_lora.at[DECAY_LORA + ICLR_LORA:, 2 * w:].set(w_gate[l])
    row = lambda z: z.reshape(1, -1)
    wts = dict(
        w1a=w_ffn1_in[l].astype(BF16), w2a=w_ffn1_out[l].astype(BF16), win=w_in[l].astype(BF16),
        w1b=w_ffn2_in[l].astype(BF16), w2b=w_ffn2_out[l].astype(BF16), wo=w_out[l].astype(BF16),
        ln_g=ln_g[l], ln_b=ln_b[l], mu=row(mu_shift[l]), w0=row(w0[l]), a0=row(a0[l]),
        k_k=row(k_k[l]), k_a=row(k_a[l]), r_k=row(r_k[l]), w_lora=w_lora.astype(BF16),
        gn_w=row(gn_w[l]), gn_b=row(gn_b[l]), rel_bias=rel_bias, sinks=sinks[l])

    wkv0_p = jnp.zeros((bp, RWKV_HEADS, HEAD_DIM, HEAD_DIM), F32)
    shift0_p = jnp.zeros((bp, RWKV_PROJ), F32)
    y_p, wkv_p, sh_p, k_p, v_p = _layer(x_prompt, mod_p, wkv0_p, shift0_p, None, None, wts,
                                        1, ROW_TILE, bp, WKV_CHUNK, bp)
    kbuf = cache_win_k[l].reshape(bs, wb, ATT_KV_WIDTH)
    vbuf = cache_win_v[l].reshape(bs, wb, ATT_KV_WIDTH)
    y_s, wkv_s, sh_s, k_s, v_s = _layer(x_sample, mod_s, state_wkv[l], state_shift[l], kbuf, vbuf, wts,
                                        ROW_TILE // ts, ts, 8, ts, 2)
    kvshape = lambda z: z.reshape(1, z.shape[0], wb, ATT_KV_HEADS, HEAD_DIM)
    return (y_p, y_s, wkv_p[None], sh_p[None], kvshape(k_p), kvshape(v_p),
            wkv_s[None], sh_s[None], kvshape(k_s), kvshape(v_s))
```

```python
import functools
import math

import numpy as np
import jax
import jax.numpy as jnp
from jax import lax
from jax.experimental import pallas as pl
from jax.experimental.pallas import tpu as pltpu

F32 = jnp.float32
BF16 = jnp.bfloat16

HEAD_DIM = 64
RWKV_HEADS = 8
RWKV_WIDTH = RWKV_HEADS * HEAD_DIM
ATT_HEADS = 8
ATT_KV_HEADS = 2
ATT_GROUPS = ATT_HEADS // ATT_KV_HEADS
ATT_WIDTH = ATT_HEADS * HEAD_DIM
ATT_KV_WIDTH = ATT_KV_HEADS * HEAD_DIM
DECAY_LORA = 64
ICLR_LORA = 64
GATE_LORA = 128
LORA_WIDTH = DECAY_LORA + ICLR_LORA + GATE_LORA
RWKV_PROJ = 3 * RWKV_WIDTH + LORA_WIDTH
WINDOW = 128
N_BUCKETS = 32
MAX_DISTANCE = 128
N_MOD = 9
DEPTH = 1
ALPHA = (2 * DEPTH) ** 0.25
LN_EPS = 1e-5
GN_EPS = 64e-5
NEG_INF = -1e30

VMEM_LIMIT_BYTES = 56 * 1024 * 1024
ROW_TILE = 256
SUB_TILES = 1
FF_CHUNK = 2816
WKV_CHUNK = 64


def _dot(a, b):
    return jnp.dot(a, b, preferred_element_type=F32)


def _dot_nt(a, b):
    return lax.dot_general(a, b, (((1,), (1,)), ((), ())), preferred_element_type=F32)


def _dot_tn(a, b):
    return lax.dot_general(a, b, (((0,), (0,)), ((), ())), preferred_element_type=F32)


def _split2(a):
    hi = a.astype(BF16)
    lo = (a - hi.astype(F32)).astype(BF16)
    return hi, lo


def _head_sums(x):
    pair = 2 * HEAD_DIM
    low = lax.broadcasted_iota(jnp.int32, (x.shape[0], pair), 1) < HEAD_DIM
    outs = []
    for p in range(x.shape[1] // pair):
        xs = x[:, p * pair:(p + 1) * pair]
        s_lo = jnp.sum(jnp.where(low, xs, 0.0), axis=-1, keepdims=True)
        s_hi = jnp.sum(jnp.where(low, 0.0, xs), axis=-1, keepdims=True)
        outs.append(jnp.where(low, s_lo, s_hi))
    return jnp.concatenate(outs, axis=-1)


def _silu(x):
    return x * jax.nn.sigmoid(x)


def _swiglu(u, w1_ref, w2_ref, d_ff):
    acc = None
    for c0 in range(0, d_ff, FF_CHUNK):
        gate = _dot(u, w1_ref[:, c0:c0 + FF_CHUNK])
        up = _dot(u, w1_ref[:, d_ff + c0:d_ff + c0 + FF_CHUNK])
        part = _dot((_silu(gate) * up).astype(BF16), w2_ref[c0:c0 + FF_CHUNK, :])
        acc = part if acc is None else acc + part
    return acc


def _post_ln(h, g, b):
    mu = jnp.mean(h, axis=-1, keepdims=True)
    d = h - mu
    var = jnp.mean(d * d, axis=-1, keepdims=True)
    return d * lax.rsqrt(var + LN_EPS) * g + b


def _const_spec(shape):
    nd = len(shape)
    return pl.BlockSpec(shape, lambda *_: (0,) * nd, pipeline_mode=pl.Buffered(1))


def _ada_kernel(c_ref, w_ref, b_ref, o_ref):
    s = _silu(c_ref[...]).astype(BF16)
    o_ref[...] = _dot(s, w_ref[...].astype(BF16)) + b_ref[...]


def _ada(c_all, w_ada, b_ada):
    nb, d = c_all.shape
    n = w_ada.shape[1]
    tn = d
    return pl.pallas_call(
        _ada_kernel,
        grid=(n // tn,),
        in_specs=[pl.BlockSpec((nb, d), lambda j: (0, 0)),
                  pl.BlockSpec((d, tn), lambda j: (0, j)),
                  pl.BlockSpec((1, tn), lambda j: (0, j))],
        out_specs=pl.BlockSpec((nb, tn), lambda j: (0, j)),
        out_shape=jax.ShapeDtypeStruct((nb, n), F32),
        name="ada_mod",
    )(c_all, w_ada, b_ada.reshape(1, n))


def _sub_tiles(bb, tt, n):
    if bb >= n:
        step = bb // n
        return [(slice(s * step, (s + 1) * step), slice(0, tt), False) for s in range(n)]
    step = tt // n
    return [(slice(0, bb), slice(s * step, (s + 1) * step), True) for s in range(n)]


def _ffn1_kernel(x_ref, mod_ref, shift0_ref, w1_ref, w2_ref, win_ref, lng_ref, lnb_ref,
                 mu_ref, w0_ref, a0_ref, kk_ref, ka_ref, rk_ref, wl_ref,
                 x1_ref, q_ref, kv_ref, shift_ref, *rest, d_ff):
    prep_refs, carry_ref = rest[:-1], rest[-1]
    bb, tt, d = x_ref.shape

    @pl.when(pl.program_id(1) == 0)
    def _():
        carry_ref[...] = shift0_ref[...]

    carry = carry_ref[...]
    for bs, rs, by_rows in _sub_tiles(bb, tt, SUB_TILES):
        x = x_ref[bs, rs, :]
        sb, st, _ = x.shape
        mod = lambda k: mod_ref[bs, k:k + 1, :]
        u = (x * (1.0 + mod(1)) + mod(0)).reshape(sb * st, d).astype(BF16)
        f1 = _swiglu(u, w1_ref, w2_ref, d_ff).reshape(sb, st, d)
        x1 = _post_ln(ALPHA * x + 0.5 * (1.0 + mod(2)) * f1, lng_ref[0:1, :], lnb_ref[0:1, :])
        x1_ref[bs, rs, :] = x1
        u2 = (x1 * (1.0 + mod(4)) + mod(3)).reshape(sb * st, d).astype(BF16)
        p = _dot(u2, win_ref[...])
        q_ref[bs, rs, :] = p[:, RWKV_PROJ:RWKV_PROJ + ATT_WIDTH].reshape(sb, st, ATT_WIDTH)
        kv_ref[bs, rs, :] = p[:, RWKV_PROJ + ATT_WIDTH:].reshape(sb, st, 2 * ATT_KV_WIDTH)
        p_rwkv = p[:, :RWKV_PROJ].reshape(sb, st, RWKV_PROJ)
        first = carry if by_rows else carry[bs]
        last = p_rwkv[:, st - 1:st, :]
        if by_rows:
            carry = last
        else:
            carry_ref[bs] = last
            shift_ref[bs] = last
        outs = _rwkv_prep(p_rwkv, first, mu_ref[...], w0_ref[...], a0_ref[...], kk_ref[...], ka_ref[...],
                          rk_ref[...], wl_ref[...])
        for ref, val in zip(prep_refs, outs):
            ref[bs, rs, :] = val.reshape(sb, st, RWKV_WIDTH)
    if by_rows:
        carry_ref[...] = carry
        shift_ref[...] = carry


def _ffn1(x, mod, shift0, w1, w2, win, ln_g, ln_b, mu, w0, a0, k_k, k_a, r_k, w_lora, bb, tt):
    b, t, d = x.shape
    d_ff = w2.shape[0]
    w = RWKV_WIDTH
    tok = lambda width: pl.BlockSpec((bb, tt, width), lambda i, j: (i, j, 0))
    per_seq = pl.BlockSpec((bb, 1, RWKV_PROJ), lambda i, j: (i, 0, 0))
    consts = [w1, w2, win, ln_g, ln_b, mu, w0, a0, k_k, k_a, r_k, w_lora]
    return pl.pallas_call(
        functools.partial(_ffn1_kernel, d_ff=d_ff),
        grid=(b // bb, t // tt),
        in_specs=[tok(d), pl.BlockSpec((bb, N_MOD, d), lambda i, j: (i, 0, 0)), per_seq]
                 + [_const_spec(c.shape) for c in consts],
        out_specs=[tok(d), tok(ATT_WIDTH), tok(2 * ATT_KV_WIDTH), per_seq] + [tok(w)] * 8,
        out_shape=[jax.ShapeDtypeStruct((b, t, d), F32),
                   jax.ShapeDtypeStruct((b, t, ATT_WIDTH), F32),
                   jax.ShapeDtypeStruct((b, t, 2 * ATT_KV_WIDTH), F32),
                   jax.ShapeDtypeStruct((b, 1, RWKV_PROJ), F32)]
                  + [jax.ShapeDtypeStruct((b, t, w), F32)] * 8,
        scratch_shapes=[pltpu.VMEM((bb, 1, RWKV_PROJ), F32)],
        compiler_params=pltpu.CompilerParams(
            dimension_semantics=("arbitrary", "arbitrary"), vmem_limit_bytes=VMEM_LIMIT_BYTES),
        name="ffn1_ln_inproj_prep",
    )(x, mod, shift0, *consts)


def _rwkv_prep(p, first, mu, w0, a0, k_k, k_a, r_k, w_lora):
    bb, tt, pw = p.shape
    w = RWKV_WIDTH
    rolled = pltpu.roll(p, 1, axis=1)
    row = lax.broadcasted_iota(jnp.int32, p.shape, 1)
    prev = jnp.where(row == 0, first, rolled)
    xs = (p + (prev - p) * mu).reshape(bb * tt, pw)
    r, k, v = xs[:, :w], xs[:, w:2 * w], xs[:, 2 * w:3 * w]
    lo = xs[:, 3 * w:]
    lane = lax.broadcasted_iota(jnp.int32, lo.shape, 1)
    nl = jnp.where(lane < DECAY_LORA, jnp.tanh(lo),
                   jnp.where(lane < DECAY_LORA + ICLR_LORA, lo, jax.nn.sigmoid(lo)))
    lora = _dot(nl.astype(BF16), w_lora)
    z = -(w0 + lora[:, :w])
    softplus = jnp.maximum(z, 0.0) + jnp.log(1.0 + jnp.exp(-jnp.abs(z)))
    lw = -jnp.exp(-softplus - 0.5)
    a = jax.nn.sigmoid(a0 + lora[:, w:2 * w])
    g = lora[:, 2 * w:]
    kk = k * k_k
    kap = kk / jnp.maximum(jnp.sqrt(_head_sums(kk * kk)), 1e-12)
    kt = k * (1.0 + (a - 1.0) * k_a)
    bonus = _head_sums(r * kt * r_k) * v
    return r, lw, kt, v, kap, kap * a, g, bonus


def _pdot(asp, bsp, dot, fuse):
    (ah, al), (bh, bl) = asp, bsp
    if not fuse:
        return dot(ah, bh) + dot(ah, bl) + dot(al, bh)
    zero = jnp.zeros_like(bl)
    if dot is _dot_nt:
        lhs = jnp.concatenate([ah, al], axis=1)
        rhs = jnp.concatenate([jnp.concatenate([bh, bh], axis=1), jnp.concatenate([bl, zero], axis=1)], axis=0)
        n = bh.shape[0]
    else:
        lhs = jnp.concatenate([ah, al], axis=1 if dot is _dot else 0)
        rhs = jnp.concatenate([jnp.concatenate([bh, bl], axis=1), jnp.concatenate([bh, zero], axis=1)], axis=0)
        n = bh.shape[1]
    out = dot(lhs, rhs)
    return out[:, :n] + out[:, n:]


def _pair_masks(keep0, keep1, use_bf16):
    if use_bf16:
        return keep0, keep1, jnp.where(keep0, 1.0, 0.0).astype(BF16), jnp.where(keep1, 1.0, 0.0).astype(BF16)
    return keep0, keep1, None, None


def _expand(x, pre, masks):
    keep0, keep1, m0, m1 = masks
    if m0 is not None:
        hi, lo = pre
        return (jnp.concatenate([hi * m0, hi * m1], axis=0), jnp.concatenate([lo * m0, lo * m1], axis=0))
    return _split2(jnp.concatenate([jnp.where(keep0, x, 0.0), jnp.where(keep1, x, 0.0)], axis=0))


def _wkv_pairs(units, tri, half, c, fuse):
    strict, incl, first_level, levels, eye = tri
    rng = range(len(units))
    ex = lambda xf, pre: _expand(xf, pre, half(xf.shape))
    sp = lambda xf: (xf, _split2(xf))
    gb = [_pdot(u["kr"], ex(*u["bi"]), _dot_nt, fuse) for u in units]
    gk = [_pdot(u["kr"], ex(*u["ki"]), _dot_nt, fuse) for u in units]
    st = [_pdot(u["kr"], ex(*sp(u["s"])), _dot_nt, fuse) for u in units]
    mab = [sp(jnp.where(strict, g[:c], 0.0)) for g in gb]
    arb = [_split2(jnp.where(incl, g[c:], 0.0)) for g in gb]
    mk = [_split2(jnp.concatenate([jnp.where(strict, g[:c], 0.0), jnp.where(incl, g[c:], 0.0)], axis=0))
          for g in gk]
    mv = [_pdot(mk[i], ex(*units[i]["v"]), _dot, fuse) for i in rng]
    x = [eye - jnp.where(first_level, m[0], 0.0) for m in mab]
    for lvl in levels:
        xs = [sp(xi) for xi in x]
        t = [_split2(_pdot(xs[i][1], _expand(*mab[i], lvl), _dot, fuse)) for i in rng]
        x = [x[i] - _pdot(t[i], ex(*xs[i]), _dot, fuse) for i in rng]
    sa = [_pdot(_split2(x[i]), ex(*sp(st[i][:c] + mv[i][:c])), _dot, fuse) for i in rng]
    y = [st[i][c:] + mv[i][c:] - _pdot(arb[i], ex(*sp(sa[i])), _dot, fuse) for i in rng]
    vsa = [_split2(jnp.concatenate([units[i]["v"][0], -sa[i]], axis=0)) for i in rng]
    cross = [_pdot(vsa[i], units[i]["keb"], _dot_tn, fuse) for i in rng]
    n = HEAD_DIM
    low = half((n, 2 * n))[0]
    s_new = [units[i]["s"] * units[i]["decay"] + jnp.where(low, cross[i][:n], cross[i][n:]) for i in rng]
    return y, s_new


def _wkv_kernel(r_ref, lw_ref, kt_ref, v_ref, kap_ref, b_ref, s0_ref, y_ref, sfin_ref, s_scr, *, group):
    bb, c, w = r_ref.shape
    n = HEAD_DIM
    pair = 2 * n
    npairs = w // pair
    j = pl.program_id(1)
    use_bf16 = c % 16 == 0
    fuse = (2 * c) % 128 == 0

    @pl.when(j == 0)
    def _():
        for b_ in range(bb):
            for p in range(npairs):
                s_scr[b_, p] = jnp.concatenate([s0_ref[b_, 2 * p], s0_ref[b_, 2 * p + 1]], axis=1)

    ti = lax.broadcasted_iota(jnp.int32, (c, 2 * c), 0)
    jl = lax.broadcasted_iota(jnp.int32, (c, 2 * c), 1)
    jm = jl & (c - 1)
    low = jl < c
    lvl_masks = []
    s = 1
    while s < c:
        same = ((ti ^ jm) & ~(2 * s - 1)) == 0
        lvl_masks.append(same & ((ti & s) != 0) & ((jm & s) == 0))
        s *= 2
    levels = [_pair_masks(m & low, m & ~low, use_bf16) for m in lvl_masks[1:]]
    tri = (ti > jm, ti >= jm, lvl_masks[0], levels, (ti == jm).astype(F32))
    half_cache = {}

    def half(shape):
        if shape not in half_cache:
            lo_ = lax.broadcasted_iota(jnp.int32, shape, 1) < shape[1] // 2
            half_cache[shape] = _pair_masks(lo_, ~lo_, use_bf16)
        return half_cache[shape]

    for shape in ((c, pair), (c, 2 * c), (n, pair)):
        half(shape)
    row = lax.broadcasted_iota(jnp.int32, (c, w), 0)

    def body(gi, carry):
        units, where = [], []
        for k in range(group):
            bi_ = gi * group + k
            lw = lw_ref[bi_]
            cl = lw
            s_ = 1
            while s_ < c:
                cl = cl + jnp.where(row >= s_, pltpu.roll(cl, s_, axis=0), 0.0)
                s_ *= 2
            last = cl[c - 1:c, :]
            e_inv = jnp.exp(-cl)
            e_end = jnp.exp(last - cl)
            kt = kt_ref[bi_]
            bv = b_ref[bi_]
            wide = dict(
                kr=jnp.concatenate([kap_ref[bi_] * jnp.exp(cl - lw), r_ref[bi_] * jnp.exp(cl)], axis=0),
                bi=bv * e_inv, ki=kt * e_inv, v=v_ref[bi_],
                keb=jnp.concatenate([kt * e_end, bv * e_end], axis=0))
            split = {key: _split2(val) for key, val in wide.items()}
            decay = jnp.exp(last)
            for p in range(npairs):
                sl = slice(p * pair, (p + 1) * pair)
                cut = lambda key: (split[key][0][:, sl], split[key][1][:, sl])
                units.append(dict(kr=cut("kr"), keb=cut("keb"),
                                  bi=(wide["bi"][:, sl], cut("bi")), ki=(wide["ki"][:, sl], cut("ki")),
                                  v=(wide["v"][:, sl], cut("v")), s=s_scr[bi_, p], decay=decay[:, sl]))
                where.append((bi_, p, sl))
        ys, s_new = _wkv_pairs(units, tri, half, c, fuse)
        for (bi_, p, sl), y, sn in zip(where, ys, s_new):
            s_scr[bi_, p] = sn
            y_ref[bi_, :, sl] = y
        return carry

    if bb == group:
        body(0, 0)
    else:
        lax.fori_loop(0, bb // group, body, 0)

    @pl.when(j == pl.num_programs(1) - 1)
    def _():
        for b_ in range(bb):
            for p in range(npairs):
                sp_ = s_scr[b_, p]
                sfin_ref[b_, 2 * p] = sp_[:, :n]
                sfin_ref[b_, 2 * p + 1] = sp_[:, n:]


def _wkv(r, lw, kt, v, kap, bvec, s0, bb, c, group):
    b, t, w = r.shape
    h = w // HEAD_DIM
    tok = pl.BlockSpec((bb, c, w), lambda i, j: (i, j, 0))
    st = pl.BlockSpec((bb, h, HEAD_DIM, HEAD_DIM), lambda i, j: (i, 0, 0, 0))
    return pl.pallas_call(
        functools.partial(_wkv_kernel, group=group),
        grid=(b // bb, t // c),
        in_specs=[tok] * 6 + [st],
        out_specs=[tok, st],
        out_shape=[jax.ShapeDtypeStruct((b, t, w), F32),
                   jax.ShapeDtypeStruct((b, h, HEAD_DIM, HEAD_DIM), F32)],
        scratch_shapes=[pltpu.VMEM((bb, h // 2, HEAD_DIM, 2 * HEAD_DIM), F32)],
        compiler_params=pltpu.CompilerParams(dimension_semantics=("arbitrary", "arbitrary")),
        name="wkv_scan",
    )(r, lw, kt, v, kap, bvec, s0)


def _rel_bucket_np(dist):
    max_exact = N_BUCKETS // 2
    n = np.maximum(dist, 0)
    n_f = np.maximum(n, max_exact).astype(np.float32)
    large = max_exact + (np.log(n_f / np.float32(max_exact)) / np.float32(math.log(MAX_DISTANCE / max_exact))
                         * np.float32(N_BUCKETS - max_exact)).astype(np.int32)
    return np.where(n < max_exact, n, np.minimum(large, N_BUCKETS - 1)).astype(np.int32)


def _bias_table(idx, mask, rb_ref, h):
    acc = jnp.zeros(idx.shape, F32)
    for bk in range(N_BUCKETS):
        acc = jnp.where(idx == bk, rb_ref[bk, h], acc)
    return jnp.where(mask, acc, NEG_INF)


def _softmax_sink_pv(s_heads, v_heads, sinks):
    ms = []
    for parts, sink in zip(s_heads, sinks):
        m = sink
        for s in parts:
            m = jnp.maximum(m, jnp.max(s, axis=-1, keepdims=True))
        ms.append(m)
    es = [[jnp.exp(s - m) for s in parts] for parts, m in zip(s_heads, ms)]
    dens = []
    for parts, sink, m in zip(es, sinks, ms):
        den = jnp.exp(sink - m)
        for e in parts:
            den = den + jnp.sum(e, axis=-1, keepdims=True)
        dens.append(den)
    outs = []
    for parts, vparts in zip(es, v_heads):
        o = None
        for e, (v, dot) in zip(parts, vparts):
            pv = dot(e.astype(BF16), v)
            o = pv if o is None else o + pv
        outs.append(o)
    return [o / den for o, den in zip(outs, dens)]


def _attn_prompt_kernel(q_ref, kvc_ref, kvp_ref, idx_ref, rb_ref, sink_ref, o_ref, tb_ref):
    n = HEAD_DIM
    wq = q_ref.shape[1]
    wk = 2 * wq
    first = (pl.program_id(0) == 0) & (pl.program_id(1) == 0)

    @pl.when(first)
    def _():
        qi = lax.broadcasted_iota(jnp.int32, (wq, wk), 0)
        kj = lax.broadcasted_iota(jnp.int32, (wq, wk), 1)
        dist = qi + wq - kj
        mask = (dist >= 0) & (dist < WINDOW)
        idx = idx_ref[...]
        for h in range(ATT_HEADS):
            tb_ref[h] = _bias_table(idx, mask, rb_ref, h)

    kj = lax.broadcasted_iota(jnp.int32, (wq, wk), 1)
    dead = (pl.program_id(1) == 0) & (kj < wq)
    q = q_ref[0]
    kvc = kvc_ref[0].astype(BF16)
    kvp = kvp_ref[0].astype(BF16)
    s_heads, v_heads = [], []
    for h2 in range(ATT_KV_HEADS):
        k_all = jnp.concatenate([kvp[:, h2 * n:(h2 + 1) * n], kvc[:, h2 * n:(h2 + 1) * n]], axis=0)
        v_all = jnp.concatenate([kvp[:, ATT_KV_WIDTH + h2 * n:ATT_KV_WIDTH + (h2 + 1) * n],
                                 kvc[:, ATT_KV_WIDTH + h2 * n:ATT_KV_WIDTH + (h2 + 1) * n]], axis=0)
        for g in range(ATT_GROUPS):
            h = h2 * ATT_GROUPS + g
            qh = q[:, h * n:(h + 1) * n].astype(BF16)
            s = _dot_nt(qh, k_all) * (HEAD_DIM ** -0.5) + tb_ref[h]
            s_heads.append([jnp.where(dead, NEG_INF, s)])
            v_heads.append([(v_all, _dot)])
    outs = _softmax_sink_pv(s_heads, v_heads, [sink_ref[0, h] for h in range(ATT_HEADS)])
    for h, o in enumerate(outs):
        o_ref[0, :, h * n:(h + 1) * n] = o


def _attn_prompt(q, kv, rel_bias, sinks):
    b, t, _ = q.shape
    wq = WINDOW
    dist = np.arange(wq)[:, None] + wq - np.arange(2 * wq)[None, :]
    idx = jnp.asarray(_rel_bucket_np(dist))
    smem = pl.BlockSpec(memory_space=pltpu.SMEM)
    return pl.pallas_call(
        _attn_prompt_kernel,
        grid=(b, t // wq),
        in_specs=[pl.BlockSpec((1, wq, ATT_WIDTH), lambda i, j: (i, j, 0)),
                  pl.BlockSpec((1, wq, 2 * ATT_KV_WIDTH), lambda i, j: (i, j, 0)),
                  pl.BlockSpec((1, wq, 2 * ATT_KV_WIDTH), lambda i, j: (i, jnp.maximum(j - 1, 0), 0)),
                  pl.BlockSpec(idx.shape, lambda i, j: (0, 0)),
                  smem, smem],
        out_specs=pl.BlockSpec((1, wq, ATT_WIDTH), lambda i, j: (i, j, 0)),
        out_shape=jax.ShapeDtypeStruct((b, t, ATT_WIDTH), F32),
        scratch_shapes=[pltpu.VMEM((ATT_HEADS, wq, 2 * wq), F32)],
        compiler_params=pltpu.CompilerParams(dimension_semantics=("arbitrary", "arbitrary")),
        name="attn_prompt",
    )(q, kv, kv, idx, rel_bias, sinks.reshape(1, ATT_HEADS))


def _attn_sample_kernel(q_ref, kvn_ref, kb_ref, vb_ref, idxb_ref, idxn_ref, rb_ref, sink_ref,
                        o_ref, kbo_ref, vbo_ref, tbb_ref, tbn_ref):
    n = HEAD_DIM
    bb, t, _ = q_ref.shape
    wb = kb_ref.shape[1]

    @pl.when(pl.program_id(0) == 0)
    def _():
        qi = lax.broadcasted_iota(jnp.int32, (t, wb), 0)
        kj = lax.broadcasted_iota(jnp.int32, (t, wb), 1)
        dist_b = qi + wb - kj
        mask_b = (dist_b >= 0) & (dist_b < WINDOW)
        qn = lax.broadcasted_iota(jnp.int32, (t, t), 0)
        kn = lax.broadcasted_iota(jnp.int32, (t, t), 1)
        dist_n = qn - kn
        mask_n = (dist_n >= 0) & (dist_n < WINDOW)
        for h in range(ATT_HEADS):
            tbb_ref[h] = _bias_table(idxb_ref[...], mask_b, rb_ref, h)
            tbn_ref[h] = _bias_table(idxn_ref[...], mask_n, rb_ref, h)

    q = q_ref[...]
    kvn = kvn_ref[...]
    kb = kb_ref[...]
    vb = vb_ref[...]
    kbo_ref[:, :wb - t, :] = kb[:, t:, :]
    kbo_ref[:, wb - t:, :] = kvn[:, :, :ATT_KV_WIDTH]
    vbo_ref[:, :wb - t, :] = vb[:, t:, :]
    vbo_ref[:, wb - t:, :] = kvn[:, :, ATT_KV_WIDTH:]
    bnt = lambda a, b_: jnp.einsum("bqd,bkd->bqk", a, b_, preferred_element_type=F32)
    bnn = lambda a, b_: jnp.einsum("bqk,bkd->bqd", a, b_, preferred_element_type=F32)
    s_heads, v_heads = [], []
    for h2 in range(ATT_KV_HEADS):
        ks = slice(h2 * n, (h2 + 1) * n)
        vs = slice(ATT_KV_WIDTH + h2 * n, ATT_KV_WIDTH + (h2 + 1) * n)
        k_buf = kb[:, :, ks].astype(BF16)
        v_buf = vb[:, :, ks].astype(BF16)
        k_new = kvn[:, :, ks].astype(BF16)
        v_new = kvn[:, :, vs].astype(BF16)
        for g in range(ATT_GROUPS):
            h = h2 * ATT_GROUPS + g
            qh = q[:, :, h * n:(h + 1) * n].astype(BF16)
            s_heads.append([bnt(qh, k_buf) * (HEAD_DIM ** -0.5) + tbb_ref[h][None],
                            bnt(qh, k_new) * (HEAD_DIM ** -0.5) + tbn_ref[h][None]])
            v_heads.append([(v_buf, bnn), (v_new, bnn)])
    outs = _softmax_sink_pv(s_heads, v_heads, [sink_ref[0, h] for h in range(ATT_HEADS)])
    for h, o in enumerate(outs):
        o_ref[:, :, h * n:(h + 1) * n] = o


def _attn_sample(q, kv, kbuf, vbuf, rel_bias, sinks, bb):
    b, t, _ = q.shape
    wb = kbuf.shape[1]
    dist_b = np.arange(t)[:, None] + wb - np.arange(wb)[None, :]
    dist_n = np.arange(t)[:, None] - np.arange(t)[None, :]
    idx_b = jnp.asarray(_rel_bucket_np(dist_b))
    idx_n = jnp.asarray(_rel_bucket_np(dist_n))
    smem = pl.BlockSpec(memory_space=pltpu.SMEM)
    tok = lambda w: pl.BlockSpec((bb, t, w), lambda i: (i, 0, 0))
    buf = pl.BlockSpec((bb, wb, ATT_KV_WIDTH), lambda i: (i, 0, 0))
    return pl.pallas_call(
        _attn_sample_kernel,
        grid=(b // bb,),
        in_specs=[tok(ATT_WIDTH), tok(2 * ATT_KV_WIDTH), buf, buf,
                  pl.BlockSpec(idx_b.shape, lambda i: (0, 0)),
                  pl.BlockSpec(idx_n.shape, lambda i: (0, 0)),
                  smem, smem],
        out_specs=[tok(ATT_WIDTH), buf, buf],
        out_shape=[jax.ShapeDtypeStruct((b, t, ATT_WIDTH), F32),
                   jax.ShapeDtypeStruct(kbuf.shape, F32),
                   jax.ShapeDtypeStruct(vbuf.shape, F32)],
        scratch_shapes=[pltpu.VMEM((ATT_HEADS, t, wb), F32), pltpu.VMEM((ATT_HEADS, t, t), F32)],
        compiler_params=pltpu.CompilerParams(dimension_semantics=("arbitrary",)),
        name="attn_sample",
    )(q, kv, kbuf, vbuf, idx_b, idx_n, rel_bias, sinks.reshape(1, ATT_HEADS))


def _out_kernel(x1_ref, mod_ref, y_ref, g_ref, bonus_ref, att_ref, gnw_ref, gnb_ref,
                wo_ref, w1_ref, w2_ref, lng_ref, lnb_ref, o_ref, *, d_ff):
    bb, tt, d = x1_ref.shape
    w = RWKV_WIDTH
    for bs, rs, _ in _sub_tiles(bb, tt, SUB_TILES):
        x1 = x1_ref[bs, rs, :]
        sb, st, _ = x1.shape
        m = sb * st
        mod = lambda k: mod_ref[bs, k:k + 1, :]
        y = y_ref[bs, rs, :].reshape(m, w)
        mu = _head_sums(y) * (1.0 / HEAD_DIM)
        dy = y - mu
        var = _head_sums(dy * dy) * (1.0 / HEAD_DIM)
        yn = dy * lax.rsqrt(var + GN_EPS) * gnw_ref[...] + gnb_ref[...]
        y_rwkv = (yn + bonus_ref[bs, rs, :].reshape(m, w)) * g_ref[bs, rs, :].reshape(m, w)
        mix = (_dot(y_rwkv.astype(BF16), wo_ref[:w, :])
               + _dot(att_ref[bs, rs, :].reshape(m, ATT_WIDTH).astype(BF16), wo_ref[w:, :])).reshape(sb, st, d)
        x2 = _post_ln(ALPHA * x1 + (1.0 + mod(5)) * mix, lng_ref[1:2, :], lnb_ref[1:2, :])
        u = (x2 * (1.0 + mod(7)) + mod(6)).reshape(m, d).astype(BF16)
        f2 = _swiglu(u, w1_ref, w2_ref, d_ff).reshape(sb, st, d)
        o_ref[bs, rs, :] = _post_ln(ALPHA * x2 + 0.5 * (1.0 + mod(8)) * f2, lng_ref[2:3, :], lnb_ref[2:3, :])


def _out(x1, mod, y, g, bonus, att, gn_w, gn_b, wo, w1, w2, ln_g, ln_b, bb, tt):
    b, t, d = x1.shape
    d_ff = w2.shape[0]
    tok = lambda w: pl.BlockSpec((bb, tt, w), lambda i, j: (i, j, 0))
    return pl.pallas_call(
        functools.partial(_out_kernel, d_ff=d_ff),
        grid=(b // bb, t // tt),
        in_specs=[tok(d),
                  pl.BlockSpec((bb, N_MOD, d), lambda i, j: (i, 0, 0)),
                  tok(RWKV_WIDTH), tok(RWKV_WIDTH), tok(RWKV_WIDTH), tok(ATT_WIDTH),
                  _const_spec(gn_w.shape), _const_spec(gn_b.shape),
                  _const_spec(wo.shape), _const_spec(w1.shape), _const_spec(w2.shape),
                  _const_spec(ln_g.shape), _const_spec(ln_b.shape)],
        out_specs=tok(d),
        out_shape=jax.ShapeDtypeStruct((b, t, d), F32),
        compiler_params=pltpu.CompilerParams(
            dimension_semantics=("arbitrary", "arbitrary"), vmem_limit_bytes=VMEM_LIMIT_BYTES),
        name="mix_ln_ffn2",
    )(x1, mod, y, g, bonus, att, gn_w, gn_b, wo, w1, w2, ln_g, ln_b)


def _layer(x, mod, wkv0, shift0, kbuf, vbuf, wts, bb, tt, wkv_bb, wkv_c, wkv_group):
    b, t, d = x.shape
    x1, q, kv, shift_new, r, lw, kt, v, kap, bvec, g, bonus = _ffn1(
        x, mod, shift0[:, None, :], wts["w1a"], wts["w2a"], wts["win"], wts["ln_g"], wts["ln_b"],
        wts["mu"], wts["w0"], wts["a0"], wts["k_k"], wts["k_a"], wts["r_k"], wts["w_lora"], bb, tt)
    y, s_fin = _wkv(r, lw, kt, v, kap, bvec, wkv0, wkv_bb, wkv_c, wkv_group)
    shift_new = shift_new[:, 0, :]
    if kbuf is None:
        att = _attn_prompt(q, kv, wts["rel_bias"], wts["sinks"])
        wb = WINDOW
        kb_new = kv[:, t - wb:, :ATT_KV_WIDTH]
        vb_new = kv[:, t - wb:, ATT_KV_WIDTH:]
    else:
        att, kb_new, vb_new = _attn_sample(q, kv, kbuf, vbuf, wts["rel_bias"], wts["sinks"], bb)
    out = _out(x1, mod, y, g, bonus, att, wts["gn_w"], wts["gn_b"], wts["wo"],
               wts["w1b"], wts["w2b"], wts["ln_g"], wts["ln_b"], bb, tt)
    return out, s_fin, shift_new, kb_new, vb_new


def kernel(x_prompt, x_sample, state_wkv, state_shift, cache_win_k, cache_win_v, c_prompt, c_sample, rel_bias, w_ada, b_ada, ln_g, ln_b, w_ffn1_in, w_ffn1_out, w_in, mu_shift, w0, w_decay, a0, w_iclr, w_gate, k_k, k_a, r_k, gn_w, gn_b, sinks, w_out, w_ffn2_in, w_ffn2_out):
    bp, tp, d = x_prompt.shape
    bs, ts, _ = x_sample.shape
    depth = w_ada.shape[0]
    assert depth == 1
    l = 0
    w = RWKV_WIDTH
    wb = cache_win_k.shape[2]

    mod = _ada(jnp.concatenate([c_prompt, c_sample], axis=0), w_ada[l], b_ada[l])
    mod = mod.reshape(bp + bs, N_MOD, d)
    mod_p, mod_s = mod[:bp], mod[bp:]

    w_lora = jnp.zeros((LORA_WIDTH, 3 * w), F32)
    w_lora = w_lora.at[:DECAY_LORA, :w].set(w_decay[l])
    w_lora = w_lora.at[DECAY_LORA:DECAY_LORA + ICLR_LORA, w:2 * w].set(w_iclr[l])
    w_lora = w_lora.at[DECAY_LORA + ICLR_LORA:, 2 * w:].set(w_gate[l])
    row = lambda z: z.reshape(1, -1)
    wts = dict(
        w1a=w_ffn1_in[l].astype(BF16), w2a=w_ffn1_out[l].astype(BF16), win=w_in[l].astype(BF16),
        w1b=w_ffn2_in[l].astype(BF16), w2b=w_ffn2_out[l].astype(BF16), wo=w_out[l].astype(BF16),
        ln_g=ln_g[l], ln_b=ln_b[l], mu=row(mu_shift[l]), w0=row(w0[l]), a0=row(a0[l]),
        k_k=row(k_k[l]), k_a=row(k_a[l]), r_k=row(r_k[l]), w_lora=w_lora.astype(BF16),
        gn_w=row(gn_w[l]), gn_b=row(gn_b[l]), rel_bias=rel_bias, sinks=sinks[l])

    wkv0_p = jnp.zeros((bp, RWKV_HEADS, HEAD_DIM, HEAD_DIM), F32)
    shift0_p = jnp.zeros((bp, RWKV_PROJ), F32)
    y_p, wkv_p, sh_p, k_p, v_p = _layer(x_prompt, mod_p, wkv0_p, shift0_p, None, None, wts,
                                        1, ROW_TILE, bp, WKV_CHUNK, bp)
    kbuf = cache_win_k[l].reshape(bs, wb, ATT_KV_WIDTH)
    vbuf = cache_win_v[l].reshape(bs, wb, ATT_KV_WIDTH)
    y_s, wkv_s, sh_s, k_s, v_s = _layer(x_sample, mod_s, state_wkv[l], state_shift[l], kbuf, vbuf, wts,
                                        ROW_TILE // ts, ts, 8, ts, 2)
    kvshape = lambda z: z.reshape(1, z.shape[0], wb, ATT_KV_HEADS, HEAD_DIM)
    return (y_p, y_s, wkv_p[None], sh_p[None], kvshape(k_p), kvshape(v_p),
            wkv_s[None], sh_s[None], kvshape(k_s), kvshape(v_s))
```

```python
import functools
import math

import numpy as np
import jax
import jax.numpy as jnp
from jax import lax
from jax.experimental import pallas as pl
from jax.experimental.pallas import tpu as pltpu

F32 = jnp.float32
BF16 = jnp.bfloat16

HEAD_DIM = 64
RWKV_HEADS = 8
RWKV_WIDTH = RWKV_HEADS * HEAD_DIM
ATT_HEADS = 8
ATT_KV_HEADS = 2
ATT_GROUPS = ATT_HEADS // ATT_KV_HEADS
ATT_WIDTH = ATT_HEADS * HEAD_DIM
ATT_KV_WIDTH = ATT_KV_HEADS * HEAD_DIM
DECAY_LORA = 64
ICLR_LORA = 64
GATE_LORA = 128
LORA_WIDTH = DECAY_LORA + ICLR_LORA + GATE_LORA
RWKV_PROJ = 3 * RWKV_WIDTH + LORA_WIDTH
WINDOW = 128
N_BUCKETS = 32
MAX_DISTANCE = 128
N_MOD = 9
DEPTH = 1
ALPHA = (2 * DEPTH) ** 0.25
LN_EPS = 1e-5
GN_EPS = 64e-5
NEG_INF = -1e30

VMEM_LIMIT_BYTES = 56 * 1024 * 1024
ROW_TILE = 256
SUB_TILES = 1
FF_CHUNK = 2816
WKV_CHUNK = 64
ATT_QBLOCKS = 4


def _dot(a, b):
    return jnp.dot(a, b, preferred_element_type=F32)


def _dot_nt(a, b):
    return lax.dot_general(a, b, (((1,), (1,)), ((), ())), preferred_element_type=F32)


def _dot_tn(a, b):
    return lax.dot_general(a, b, (((0,), (0,)), ((), ())), preferred_element_type=F32)


def _split2(a):
    hi = a.astype(BF16)
    lo = (a - hi.astype(F32)).astype(BF16)
    return hi, lo


def _head_sums(x):
    pair = 2 * HEAD_DIM
    low = lax.broadcasted_iota(jnp.int32, (x.shape[0], pair), 1) < HEAD_DIM
    outs = []
    for p in range(x.shape[1] // pair):
        xs = x[:, p * pair:(p + 1) * pair]
        s_lo = jnp.sum(jnp.where(low, xs, 0.0), axis=-1, keepdims=True)
        s_hi = jnp.sum(jnp.where(low, 0.0, xs), axis=-1, keepdims=True)
        outs.append(jnp.where(low, s_lo, s_hi))
    return jnp.concatenate(outs, axis=-1)


def _silu(x):
    return x * jax.nn.sigmoid(x)


def _swiglu(u, w1_ref, w2_ref, d_ff):
    acc = None
    for c0 in range(0, d_ff, FF_CHUNK):
        gate = _dot(u, w1_ref[:, c0:c0 + FF_CHUNK])
        up = _dot(u, w1_ref[:, d_ff + c0:d_ff + c0 + FF_CHUNK])
        part = _dot((_silu(gate) * up).astype(BF16), w2_ref[c0:c0 + FF_CHUNK, :])
        acc = part if acc is None else acc + part
    return acc


def _post_ln(h, g, b):
    mu = jnp.mean(h, axis=-1, keepdims=True)
    d = h - mu
    var = jnp.mean(d * d, axis=-1, keepdims=True)
    return d * lax.rsqrt(var + LN_EPS) * g + b


def _const_spec(shape):
    nd = len(shape)
    return pl.BlockSpec(shape, lambda *_: (0,) * nd, pipeline_mode=pl.Buffered(1))


def _ada_kernel(c_ref, w_ref, b_ref, o_ref):
    s = _silu(c_ref[...]).astype(BF16)
    o_ref[...] = _dot(s, w_ref[...].astype(BF16)) + b_ref[...]


def _ada(c_all, w_ada, b_ada):
    nb, d = c_all.shape
    n = w_ada.shape[1]
    tn = d
    return pl.pallas_call(
        _ada_kernel,
        grid=(n // tn,),
        in_specs=[pl.BlockSpec((nb, d), lambda j: (0, 0)),
                  pl.BlockSpec((d, tn), lambda j: (0, j)),
                  pl.BlockSpec((1, tn), lambda j: (0, j))],
        out_specs=pl.BlockSpec((nb, tn), lambda j: (0, j)),
        out_shape=jax.ShapeDtypeStruct((nb, n), F32),
        name="ada_mod",
    )(c_all, w_ada, b_ada.reshape(1, n))


def _sub_tiles(bb, tt, n):
    if bb >= n:
        step = bb // n
        return [(slice(s * step, (s + 1) * step), slice(0, tt), False) for s in range(n)]
    step = tt // n
    return [(slice(0, bb), slice(s * step, (s + 1) * step), True) for s in range(n)]


def _ffn1_kernel(x_ref, mod_ref, shift0_ref, w1_ref, w2_ref, win_ref, lng_ref, lnb_ref,
                 mu_ref, w0_ref, a0_ref, kk_ref, ka_ref, rk_ref, wl_ref,
                 x1_ref, q_ref, kv_ref, shift_ref, *rest, d_ff):
    prep_refs, carry_ref = rest[:-1], rest[-1]
    bb, tt, d = x_ref.shape

    @pl.when(pl.program_id(1) == 0)
    def _():
        carry_ref[...] = shift0_ref[...]

    carry = carry_ref[...]
    for bs, rs, by_rows in _sub_tiles(bb, tt, SUB_TILES):
        x = x_ref[bs, rs, :]
        sb, st, _ = x.shape
        mod = lambda k: mod_ref[bs, k:k + 1, :]
        u = (x * (1.0 + mod(1)) + mod(0)).reshape(sb * st, d).astype(BF16)
        f1 = _swiglu(u, w1_ref, w2_ref, d_ff).reshape(sb, st, d)
        x1 = _post_ln(ALPHA * x + 0.5 * (1.0 + mod(2)) * f1, lng_ref[0:1, :], lnb_ref[0:1, :])
        x1_ref[bs, rs, :] = x1
        u2 = (x1 * (1.0 + mod(4)) + mod(3)).reshape(sb * st, d).astype(BF16)
        p = _dot(u2, win_ref[...])
        q_ref[bs, rs, :] = p[:, RWKV_PROJ:RWKV_PROJ + ATT_WIDTH].reshape(sb, st, ATT_WIDTH)
        kv_ref[bs, rs, :] = p[:, RWKV_PROJ + ATT_WIDTH:].reshape(sb, st, 2 * ATT_KV_WIDTH)
        p_rwkv = p[:, :RWKV_PROJ].reshape(sb, st, RWKV_PROJ)
        first = carry if by_rows else carry[bs]
        last = p_rwkv[:, st - 1:st, :]
        if by_rows:
            carry = last
        else:
            carry_ref[bs] = last
            shift_ref[bs] = last
        outs = _rwkv_prep(p_rwkv, first, mu_ref[...], w0_ref[...], a0_ref[...], kk_ref[...], ka_ref[...],
                          rk_ref[...], wl_ref[...])
        for ref, val in zip(prep_refs, outs):
            ref[bs, rs, :] = val.reshape(sb, st, RWKV_WIDTH)
    if by_rows:
        carry_ref[...] = carry
        shift_ref[...] = carry


def _ffn1(x, mod, shift0, w1, w2, win, ln_g, ln_b, mu, w0, a0, k_k, k_a, r_k, w_lora, bb, tt):
    b, t, d = x.shape
    d_ff = w2.shape[0]
    w = RWKV_WIDTH
    tok = lambda width: pl.BlockSpec((bb, tt, width), lambda i, j: (i, j, 0))
    per_seq = pl.BlockSpec((bb, 1, RWKV_PROJ), lambda i, j: (i, 0, 0))
    consts = [w1, w2, win, ln_g, ln_b, mu, w0, a0, k_k, k_a, r_k, w_lora]
    return pl.pallas_call(
        functools.partial(_ffn1_kernel, d_ff=d_ff),
        grid=(b // bb, t // tt),
        in_specs=[tok(d), pl.BlockSpec((bb, N_MOD, d), lambda i, j: (i, 0, 0)), per_seq]
                 + [_const_spec(c.shape) for c in consts],
        out_specs=[tok(d), tok(ATT_WIDTH), tok(2 * ATT_KV_WIDTH), per_seq] + [tok(w)] * 8,
        out_shape=[jax.ShapeDtypeStruct((b, t, d), F32),
                   jax.ShapeDtypeStruct((b, t, ATT_WIDTH), F32),
                   jax.ShapeDtypeStruct((b, t, 2 * ATT_KV_WIDTH), F32),
                   jax.ShapeDtypeStruct((b, 1, RWKV_PROJ), F32)]
                  + [jax.ShapeDtypeStruct((b, t, w), F32)] * 8,
        scratch_shapes=[pltpu.VMEM((bb, 1, RWKV_PROJ), F32)],
        compiler_params=pltpu.CompilerParams(
            dimension_semantics=("arbitrary", "arbitrary"), vmem_limit_bytes=VMEM_LIMIT_BYTES),
        name="ffn1_ln_inproj_prep",
    )(x, mod, shift0, *consts)


def _rwkv_prep(p, first, mu, w0, a0, k_k, k_a, r_k, w_lora):
    bb, tt, pw = p.shape
    w = RWKV_WIDTH
    rolled = pltpu.roll(p, 1, axis=1)
    row = lax.broadcasted_iota(jnp.int32, p.shape, 1)
    prev = jnp.where(row == 0, first, rolled)
    xs = (p + (prev - p) * mu).reshape(bb * tt, pw)
    r, k, v = xs[:, :w], xs[:, w:2 * w], xs[:, 2 * w:3 * w]
    lo = xs[:, 3 * w:]
    lane = lax.broadcasted_iota(jnp.int32, lo.shape, 1)
    nl = jnp.where(lane < DECAY_LORA, jnp.tanh(lo),
                   jnp.where(lane < DECAY_LORA + ICLR_LORA, lo, jax.nn.sigmoid(lo)))
    lora = _dot(nl.astype(BF16), w_lora)
    z = -(w0 + lora[:, :w])
    softplus = jnp.maximum(z, 0.0) + jnp.log(1.0 + jnp.exp(-jnp.abs(z)))
    lw = -jnp.exp(-softplus - 0.5)
    a = jax.nn.sigmoid(a0 + lora[:, w:2 * w])
    g = lora[:, 2 * w:]
    kk = k * k_k
    kap = kk / jnp.maximum(jnp.sqrt(_head_sums(kk * kk)), 1e-12)
    kt = k * (1.0 + (a - 1.0) * k_a)
    bonus = _head_sums(r * kt * r_k) * v
    return r, lw, kt, v, kap, kap * a, g, bonus


def _pdot(asp, bsp, dot, fuse):
    (ah, al), (bh, bl) = asp, bsp
    if not fuse:
        return dot(ah, bh) + dot(ah, bl) + dot(al, bh)
    zero = jnp.zeros_like(bl)
    if dot is _dot_nt:
        lhs = jnp.concatenate([ah, al], axis=1)
        rhs = jnp.concatenate([jnp.concatenate([bh, bh], axis=1), jnp.concatenate([bl, zero], axis=1)], axis=0)
        n = bh.shape[0]
    else:
        lhs = jnp.concatenate([ah, al], axis=1 if dot is _dot else 0)
        rhs = jnp.concatenate([jnp.concatenate([bh, bl], axis=1), jnp.concatenate([bh, zero], axis=1)], axis=0)
        n = bh.shape[1]
    out = dot(lhs, rhs)
    return out[:, :n] + out[:, n:]


def _pair_masks(keep0, keep1, use_bf16):
    if use_bf16:
        return keep0, keep1, jnp.where(keep0, 1.0, 0.0).astype(BF16), jnp.where(keep1, 1.0, 0.0).astype(BF16)
    return keep0, keep1, None, None


def _expand(x, pre, masks):
    keep0, keep1, m0, m1 = masks
    if m0 is not None:
        hi, lo = pre
        return (jnp.concatenate([hi * m0, hi * m1], axis=0), jnp.concatenate([lo * m0, lo * m1], axis=0))
    return _split2(jnp.concatenate([jnp.where(keep0, x, 0.0), jnp.where(keep1, x, 0.0)], axis=0))


def _wkv_pairs(units, tri, half, c, fuse):
    strict, incl, first_level, levels, eye = tri
    rng = range(len(units))
    ex = lambda xf, pre: _expand(xf, pre, half(xf.shape))
    sp = lambda xf: (xf, _split2(xf))
    gb = [_pdot(u["kr"], ex(*u["bi"]), _dot_nt, fuse) for u in units]
    gk = [_pdot(u["kr"], ex(*u["ki"]), _dot_nt, fuse) for u in units]
    st = [_pdot(u["kr"], ex(*sp(u["s"])), _dot_nt, fuse) for u in units]
    mab = [sp(jnp.where(strict, g[:c], 0.0)) for g in gb]
    arb = [_split2(jnp.where(incl, g[c:], 0.0)) for g in gb]
    mk = [_split2(jnp.concatenate([jnp.where(strict, g[:c], 0.0), jnp.where(incl, g[c:], 0.0)], axis=0))
          for g in gk]
    mv = [_pdot(mk[i], ex(*units[i]["v"]), _dot, fuse) for i in rng]
    x = [eye - jnp.where(first_level, m[0], 0.0) for m in mab]
    for lvl in levels:
        xs = [sp(xi) for xi in x]
        t = [_split2(_pdot(xs[i][1], _expand(*mab[i], lvl), _dot, fuse)) for i in rng]
        x = [x[i] - _pdot(t[i], ex(*xs[i]), _dot, fuse) for i in rng]
    sa = [_pdot(_split2(x[i]), ex(*sp(st[i][:c] + mv[i][:c])), _dot, fuse) for i in rng]
    y = [st[i][c:] + mv[i][c:] - _pdot(arb[i], ex(*sp(sa[i])), _dot, fuse) for i in rng]
    vsa = [_split2(jnp.concatenate([units[i]["v"][0], -sa[i]], axis=0)) for i in rng]
    cross = [_pdot(vsa[i], units[i]["keb"], _dot_tn, fuse) for i in rng]
    n = HEAD_DIM
    low = half((n, 2 * n))[0]
    s_new = [units[i]["s"] * units[i]["decay"] + jnp.where(low, cross[i][:n], cross[i][n:]) for i in rng]
    return y, s_new


def _wkv_kernel(r_ref, lw_ref, kt_ref, v_ref, kap_ref, b_ref, s0_ref, y_ref, sfin_ref, s_scr, *, group):
    bb, c, w = r_ref.shape
    n = HEAD_DIM
    pair = 2 * n
    npairs = w // pair
    j = pl.program_id(1)
    use_bf16 = c % 16 == 0
    fuse = True

    @pl.when(j == 0)
    def _():
        for b_ in range(bb):
            for p in range(npairs):
                s_scr[b_, p] = jnp.concatenate([s0_ref[b_, 2 * p], s0_ref[b_, 2 * p + 1]], axis=1)

    ti = lax.broadcasted_iota(jnp.int32, (c, 2 * c), 0)
    jl = lax.broadcasted_iota(jnp.int32, (c, 2 * c), 1)
    jm = jl & (c - 1)
    low = jl < c
    lvl_masks = []
    s = 1
    while s < c:
        same = ((ti ^ jm) & ~(2 * s - 1)) == 0
        lvl_masks.append(same & ((ti & s) != 0) & ((jm & s) == 0))
        s *= 2
    levels = [_pair_masks(m & low, m & ~low, use_bf16) for m in lvl_masks[1:]]
    tri = (ti > jm, ti >= jm, lvl_masks[0], levels, (ti == jm).astype(F32))
    half_cache = {}

    def half(shape):
        if shape not in half_cache:
            lo_ = lax.broadcasted_iota(jnp.int32, shape, 1) < shape[1] // 2
            half_cache[shape] = _pair_masks(lo_, ~lo_, use_bf16)
        return half_cache[shape]

    for shape in ((c, pair), (c, 2 * c), (n, pair)):
        half(shape)
    row = lax.broadcasted_iota(jnp.int32, (c, w), 0)

    def body(gi, carry):
        units, where = [], []
        for k in range(group):
            bi_ = gi * group + k
            lw = lw_ref[bi_]
            cl = lw
            s_ = 1
            while s_ < c:
                cl = cl + jnp.where(row >= s_, pltpu.roll(cl, s_, axis=0), 0.0)
                s_ *= 2
            last = cl[c - 1:c, :]
            e_inv = jnp.exp(-cl)
            e_end = jnp.exp(last - cl)
            kt = kt_ref[bi_]
            bv = b_ref[bi_]
            wide = dict(
                kr=jnp.concatenate([kap_ref[bi_] * jnp.exp(cl - lw), r_ref[bi_] * jnp.exp(cl)], axis=0),
                bi=bv * e_inv, ki=kt * e_inv, v=v_ref[bi_],
                keb=jnp.concatenate([kt * e_end, bv * e_end], axis=0))
            split = {key: _split2(val) for key, val in wide.items()}
            decay = jnp.exp(last)
            for p in range(npairs):
                sl = slice(p * pair, (p + 1) * pair)
                cut = lambda key: (split[key][0][:, sl], split[key][1][:, sl])
                units.append(dict(kr=cut("kr"), keb=cut("keb"),
                                  bi=(wide["bi"][:, sl], cut("bi")), ki=(wide["ki"][:, sl], cut("ki")),
                                  v=(wide["v"][:, sl], cut("v")), s=s_scr[bi_, p], decay=decay[:, sl]))
                where.append((bi_, p, sl))
        ys, s_new = _wkv_pairs(units, tri, half, c, fuse)
        for (bi_, p, sl), y, sn in zip(where, ys, s_new):
            s_scr[bi_, p] = sn
            y_ref[bi_, :, sl] = y
        return carry

    if bb == group:
        body(0, 0)
    else:
        lax.fori_loop(0, bb // group, body, 0)

    @pl.when(j == pl.num_programs(1) - 1)
    def _():
        for b_ in range(bb):
            for p in range(npairs):
                sp_ = s_scr[b_, p]
                sfin_ref[b_, 2 * p] = sp_[:, :n]
                sfin_ref[b_, 2 * p + 1] = sp_[:, n:]


def _wkv(r, lw, kt, v, kap, bvec, s0, bb, c, group):
    b, t, w = r.shape
    h = w // HEAD_DIM
    tok = pl.BlockSpec((bb, c, w), lambda i, j: (i, j, 0))
    st = pl.BlockSpec((bb, h, HEAD_DIM, HEAD_DIM), lambda i, j: (i, 0, 0, 0))
    return pl.pallas_call(
        functools.partial(_wkv_kernel, group=group),
        grid=(b // bb, t // c),
        in_specs=[tok] * 6 + [st],
        out_specs=[tok, st],
        out_shape=[jax.ShapeDtypeStruct((b, t, w), F32),
                   jax.ShapeDtypeStruct((b, h, HEAD_DIM, HEAD_DIM), F32)],
        scratch_shapes=[pltpu.VMEM((bb, h // 2, HEAD_DIM, 2 * HEAD_DIM), F32)],
        compiler_params=pltpu.CompilerParams(dimension_semantics=("arbitrary", "arbitrary")),
        name="wkv_scan",
    )(r, lw, kt, v, kap, bvec, s0)


def _rel_bucket_np(dist):
    max_exact = N_BUCKETS // 2
    n = np.maximum(dist, 0)
    n_f = np.maximum(n, max_exact).astype(np.float32)
    large = max_exact + (np.log(n_f / np.float32(max_exact)) / np.float32(math.log(MAX_DISTANCE / max_exact))
                         * np.float32(N_BUCKETS - max_exact)).astype(np.int32)
    return np.where(n < max_exact, n, np.minimum(large, N_BUCKETS - 1)).astype(np.int32)


def _bias_table(idx, mask, rb_ref, h):
    acc = jnp.zeros(idx.shape, F32)
    for bk in range(N_BUCKETS):
        acc = jnp.where(idx == bk, rb_ref[bk, h], acc)
    return jnp.where(mask, acc, NEG_INF)


def _softmax_sink_pv(s_heads, v_heads, sinks):
    ms = []
    for parts, sink in zip(s_heads, sinks):
        m = sink
        for s in parts:
            m = jnp.maximum(m, jnp.max(s, axis=-1, keepdims=True))
        ms.append(m)
    es = [[jnp.exp(s - m) for s in parts] for parts, m in zip(s_heads, ms)]
    dens = []
    for parts, sink, m in zip(es, sinks, ms):
        den = jnp.exp(sink - m)
        for e in parts:
            den = den + jnp.sum(e, axis=-1, keepdims=True)
        dens.append(den)
    outs = []
    for parts, vparts in zip(es, v_heads):
        o = None
        for e, (v, dot) in zip(parts, vparts):
            pv = dot(e.astype(BF16), v)
            o = pv if o is None else o + pv
        outs.append(o)
    return [o / den for o, den in zip(outs, dens)]


def _attn_prompt_kernel(q_ref, kvc_ref, kvp_ref, idx_ref, rb_ref, sink_ref, o_ref, tb_ref):
    n = HEAD_DIM
    wq = WINDOW
    wk = 2 * wq
    nblk = q_ref.shape[1] // wq
    first = (pl.program_id(0) == 0) & (pl.program_id(1) == 0)

    @pl.when(first)
    def _():
        qi = lax.broadcasted_iota(jnp.int32, (wq, wk), 0)
        kj = lax.broadcasted_iota(jnp.int32, (wq, wk), 1)
        dist = qi + wq - kj
        mask = (dist >= 0) & (dist < WINDOW)
        idx = idx_ref[...]
        for h in range(ATT_HEADS):
            tb_ref[h] = _bias_table(idx, mask, rb_ref, h)

    kj = lax.broadcasted_iota(jnp.int32, (wq, wk), 1)
    dead = (pl.program_id(1) == 0) & (kj < wq)
    q = q_ref[0]
    kv = jnp.concatenate([kvp_ref[0], kvc_ref[0]], axis=0).astype(BF16)
    s_heads, v_heads, sinks, where = [], [], [], []
    for blk in range(nblk):
        rows = slice(blk * wq, blk * wq + wk)
        for h2 in range(ATT_KV_HEADS):
            k_all = kv[rows, h2 * n:(h2 + 1) * n]
            v_all = kv[rows, ATT_KV_WIDTH + h2 * n:ATT_KV_WIDTH + (h2 + 1) * n]
            for g in range(ATT_GROUPS):
                h = h2 * ATT_GROUPS + g
                qh = q[blk * wq:(blk + 1) * wq, h * n:(h + 1) * n].astype(BF16)
                s = _dot_nt(qh, k_all) * (HEAD_DIM ** -0.5) + tb_ref[h]
                s_heads.append([jnp.where(dead, NEG_INF, s) if blk == 0 else s])
                v_heads.append([(v_all, _dot)])
                sinks.append(sink_ref[0, h])
                where.append((blk, h))
    outs = _softmax_sink_pv(s_heads, v_heads, sinks)
    for (blk, h), o in zip(where, outs):
        o_ref[0, blk * wq:(blk + 1) * wq, h * n:(h + 1) * n] = o


def _attn_prompt(q, kv, rel_bias, sinks):
    b, t, _ = q.shape
    wq = WINDOW
    dist = np.arange(wq)[:, None] + wq - np.arange(2 * wq)[None, :]
    idx = jnp.asarray(_rel_bucket_np(dist))
    smem = pl.BlockSpec(memory_space=pltpu.SMEM)
    nblk = ATT_QBLOCKS
    tq = nblk * wq
    return pl.pallas_call(
        _attn_prompt_kernel,
        grid=(b, t // tq),
        in_specs=[pl.BlockSpec((1, tq, ATT_WIDTH), lambda i, j: (i, j, 0)),
                  pl.BlockSpec((1, tq, 2 * ATT_KV_WIDTH), lambda i, j: (i, j, 0)),
                  pl.BlockSpec((1, wq, 2 * ATT_KV_WIDTH), lambda i, j: (i, jnp.maximum(j * nblk - 1, 0), 0)),
                  pl.BlockSpec(idx.shape, lambda i, j: (0, 0)),
                  smem, smem],
        out_specs=pl.BlockSpec((1, tq, ATT_WIDTH), lambda i, j: (i, j, 0)),
        out_shape=jax.ShapeDtypeStruct((b, t, ATT_WIDTH), F32),
        scratch_shapes=[pltpu.VMEM((ATT_HEADS, wq, 2 * wq), F32)],
        compiler_params=pltpu.CompilerParams(dimension_semantics=("arbitrary", "arbitrary")),
        name="attn_prompt",
    )(q, kv, kv, idx, rel_bias, sinks.reshape(1, ATT_HEADS))


def _attn_sample_kernel(q_ref, kvn_ref, kb_ref, vb_ref, idxb_ref, idxn_ref, rb_ref, sink_ref,
                        o_ref, kbo_ref, vbo_ref, tbb_ref, tbn_ref):
    n = HEAD_DIM
    bb, t, _ = q_ref.shape
    wb = kb_ref.shape[1]

    @pl.when(pl.program_id(0) == 0)
    def _():
        qi = lax.broadcasted_iota(jnp.int32, (t, wb), 0)
        kj = lax.broadcasted_iota(jnp.int32, (t, wb), 1)
        dist_b = qi + wb - kj
        mask_b = (dist_b >= 0) & (dist_b < WINDOW)
        qn = lax.broadcasted_iota(jnp.int32, (t, t), 0)
        kn = lax.broadcasted_iota(jnp.int32, (t, t), 1)
        dist_n = qn - kn
        mask_n = (dist_n >= 0) & (dist_n < WINDOW)
        for h in range(ATT_HEADS):
            tbb_ref[h] = _bias_table(idxb_ref[...], mask_b, rb_ref, h)
            tbn_ref[h] = _bias_table(idxn_ref[...], mask_n, rb_ref, h)

    q = q_ref[...]
    kvn = kvn_ref[...]
    kb = kb_ref[...]
    vb = vb_ref[...]
    kbo_ref[:, :wb - t, :] = kb[:, t:, :]
    kbo_ref[:, wb - t:, :] = kvn[:, :, :ATT_KV_WIDTH]
    vbo_ref[:, :wb - t, :] = vb[:, t:, :]
    vbo_ref[:, wb - t:, :] = kvn[:, :, ATT_KV_WIDTH:]
    bnt = lambda a, b_: jnp.einsum("bqd,bkd->bqk", a, b_, preferred_element_type=F32)
    bnn = lambda a, b_: jnp.einsum("bqk,bkd->bqd", a, b_, preferred_element_type=F32)
    s_heads, v_heads = [], []
    for h2 in range(ATT_KV_HEADS):
        ks = slice(h2 * n, (h2 + 1) * n)
        vs = slice(ATT_KV_WIDTH + h2 * n, ATT_KV_WIDTH + (h2 + 1) * n)
        k_buf = kb[:, :, ks].astype(BF16)
        v_buf = vb[:, :, ks].astype(BF16)
        k_new = kvn[:, :, ks].astype(BF16)
        v_new = kvn[:, :, vs].astype(BF16)
        for g in range(ATT_GROUPS):
            h = h2 * ATT_GROUPS + g
            qh = q[:, :, h * n:(h + 1) * n].astype(BF16)
            s_heads.append([bnt(qh, k_buf) * (HEAD_DIM ** -0.5) + tbb_ref[h][None],
                            bnt(qh, k_new) * (HEAD_DIM ** -0.5) + tbn_ref[h][None]])
            v_heads.append([(v_buf, bnn), (v_new, bnn)])
    outs = _softmax_sink_pv(s_heads, v_heads, [sink_ref[0, h] for h in range(ATT_HEADS)])
    for h, o in enumerate(outs):
        o_ref[:, :, h * n:(h + 1) * n] = o


def _attn_sample(q, kv, kbuf, vbuf, rel_bias, sinks, bb):
    b, t, _ = q.shape
    wb = kbuf.shape[1]
    dist_b = np.arange(t)[:, None] + wb - np.arange(wb)[None, :]
    dist_n = np.arange(t)[:, None] - np.arange(t)[None, :]
    idx_b = jnp.asarray(_rel_bucket_np(dist_b))
    idx_n = jnp.asarray(_rel_bucket_np(dist_n))
    smem = pl.BlockSpec(memory_space=pltpu.SMEM)
    tok = lambda w: pl.BlockSpec((bb, t, w), lambda i: (i, 0, 0))
    buf = pl.BlockSpec((bb, wb, ATT_KV_WIDTH), lambda i: (i, 0, 0))
    return pl.pallas_call(
        _attn_sample_kernel,
        grid=(b // bb,),
        in_specs=[tok(ATT_WIDTH), tok(2 * ATT_KV_WIDTH), buf, buf,
                  pl.BlockSpec(idx_b.shape, lambda i: (0, 0)),
                  pl.BlockSpec(idx_n.shape, lambda i: (0, 0)),
                  smem, smem],
        out_specs=[tok(ATT_WIDTH), buf, buf],
        out_shape=[jax.ShapeDtypeStruct((b, t, ATT_WIDTH), F32),
                   jax.ShapeDtypeStruct(kbuf.shape, F32),
                   jax.ShapeDtypeStruct(vbuf.shape, F32)],
        scratch_shapes=[pltpu.VMEM((ATT_HEADS, t, wb), F32), pltpu.VMEM((ATT_HEADS, t, t), F32)],
        compiler_params=pltpu.CompilerParams(dimension_semantics=("arbitrary",)),
        name="attn_sample",
    )(q, kv, kbuf, vbuf, idx_b, idx_n, rel_bias, sinks.reshape(1, ATT_HEADS))


def _out_kernel(x1_ref, mod_ref, y_ref, g_ref, bonus_ref, att_ref, gnw_ref, gnb_ref,
                wo_ref, w1_ref, w2_ref, lng_ref, lnb_ref, o_ref, *, d_ff):
    bb, tt, d = x1_ref.shape
    w = RWKV_WIDTH
    for bs, rs, _ in _sub_tiles(bb, tt, SUB_TILES):
        x1 = x1_ref[bs, rs, :]
        sb, st, _ = x1.shape
        m = sb * st
        mod = lambda k: mod_ref[bs, k:k + 1, :]
        y = y_ref[bs, rs, :].reshape(m, w)
        mu = _head_sums(y) * (1.0 / HEAD_DIM)
        dy = y - mu
        var = _head_sums(dy * dy) * (1.0 / HEAD_DIM)
        yn = dy * lax.rsqrt(var + GN_EPS) * gnw_ref[...] + gnb_ref[...]
        y_rwkv = (yn + bonus_ref[bs, rs, :].reshape(m, w)) * g_ref[bs, rs, :].reshape(m, w)
        mix = (_dot(y_rwkv.astype(BF16), wo_ref[:w, :])
               + _dot(att_ref[bs, rs, :].reshape(m, ATT_WIDTH).astype(BF16), wo_ref[w:, :])).reshape(sb, st, d)
        x2 = _post_ln(ALPHA * x1 + (1.0 + mod(5)) * mix, lng_ref[1:2, :], lnb_ref[1:2, :])
        u = (x2 * (1.0 + mod(7)) + mod(6)).reshape(m, d).astype(BF16)
        f2 = _swiglu(u, w1_ref, w2_ref, d_ff).reshape(sb, st, d)
        o_ref[bs, rs, :] = _post_ln(ALPHA * x2 + 0.5 * (1.0 + mod(8)) * f2, lng_ref[2:3, :], lnb_ref[2:3, :])


def _out(x1, mod, y, g, bonus, att, gn_w, gn_b, wo, w1, w2, ln_g, ln_b, bb, tt):
    b, t, d = x1.shape
    d_ff = w2.shape[0]
    tok = lambda w: pl.BlockSpec((bb, tt, w), lambda i, j: (i, j, 0))
    return pl.pallas_call(
        functools.partial(_out_kernel, d_ff=d_ff),
        grid=(b // bb, t // tt),
        in_specs=[tok(d),
                  pl.BlockSpec((bb, N_MOD, d), lambda i, j: (i, 0, 0)),
                  tok(RWKV_WIDTH), tok(RWKV_WIDTH), tok(RWKV_WIDTH), tok(ATT_WIDTH),
                  _const_spec(gn_w.shape), _const_spec(gn_b.shape),
                  _const_spec(wo.shape), _const_spec(w1.shape), _const_spec(w2.shape),
                  _const_spec(ln_g.shape), _const_spec(ln_b.shape)],
        out_specs=tok(d),
        out_shape=jax.ShapeDtypeStruct((b, t, d), F32),
        compiler_params=pltpu.CompilerParams(
            dimension_semantics=("arbitrary", "arbitrary"), vmem_limit_bytes=VMEM_LIMIT_BYTES),
        name="mix_ln_ffn2",
    )(x1, mod, y, g, bonus, att, gn_w, gn_b, wo, w1, w2, ln_g, ln_b)


def _layer(x, mod, wkv0, shift0, kbuf, vbuf, wts, bb, tt, wkv_bb, wkv_c, wkv_group):
    b, t, d = x.shape
    x1, q, kv, shift_new, r, lw, kt, v, kap, bvec, g, bonus = _ffn1(
        x, mod, shift0[:, None, :], wts["w1a"], wts["w2a"], wts["win"], wts["ln_g"], wts["ln_b"],
        wts["mu"], wts["w0"], wts["a0"], wts["k_k"], wts["k_a"], wts["r_k"], wts["w_lora"], bb, tt)
    y, s_fin = _wkv(r, lw, kt, v, kap, bvec, wkv0, wkv_bb, wkv_c, wkv_group)
    shift_new = shift_new[:, 0, :]
    if kbuf is None:
        att = _attn_prompt(q, kv, wts["rel_bias"], wts["sinks"])
        wb = WINDOW
        kb_new = kv[:, t - wb:, :ATT_KV_WIDTH]
        vb_new = kv[:, t - wb:, ATT_KV_WIDTH:]
    else:
        att, kb_new, vb_new = _attn_sample(q, kv, kbuf, vbuf, wts["rel_bias"], wts["sinks"], bb)
    out = _out(x1, mod, y, g, bonus, att, wts["gn_w"], wts["gn_b"], wts["wo"],
               wts["w1b"], wts["w2b"], wts["ln_g"], wts["ln_b"], bb, tt)
    return out, s_fin, shift_new, kb_new, vb_new


def kernel(x_prompt, x_sample, state_wkv, state_shift, cache_win_k, cache_win_v, c_prompt, c_sample, rel_bias, w_ada, b_ada, ln_g, ln_b, w_ffn1_in, w_ffn1_out, w_in, mu_shift, w0, w_decay, a0, w_iclr, w_gate, k_k, k_a, r_k, gn_w, gn_b, sinks, w_out, w_ffn2_in, w_ffn2_out):
    bp, tp, d = x_prompt.shape
    bs, ts, _ = x_sample.shape
    depth = w_ada.shape[0]
    assert depth == 1
    l = 0
    w = RWKV_WIDTH
    wb = cache_win_k.shape[2]

    mod = _ada(jnp.concatenate([c_prompt, c_sample], axis=0), w_ada[l], b_ada[l])
    mod = mod.reshape(bp + bs, N_MOD, d)
    mod_p, mod_s = mod[:bp], mod[bp:]

    w_lora = jnp.zeros((LORA_WIDTH, 3 * w), F32)
    w_lora = w_lora.at[:DECAY_LORA, :w].set(w_decay[l])
    w_lora = w_lora.at[DECAY_LORA:DECAY_LORA + ICLR_LORA, w:2 * w].set(w_iclr[l])
    w_lora = w_lora.at[DECAY_LORA + ICLR_LORA:, 2 * w:].set(w_gate[l])
    row = lambda z: z.reshape(1, -1)
    wts = dict(
        w1a=w_ffn1_in[l].astype(BF16), w2a=w_ffn1_out[l].astype(BF16), win=w_in[l].astype(BF16),
        w1b=w_ffn2_in[l].astype(BF16), w2b=w_ffn2_out[l].astype(BF16), wo=w_out[l].astype(BF16),
        ln_g=ln_g[l], ln_b=ln_b[l], mu=row(mu_shift[l]), w0=row(w0[l]), a0=row(a0[l]),
        k_k=row(k_k[l]), k_a=row(k_a[l]), r_k=row(r_k[l]), w_lora=w_lora.astype(BF16),
        gn_w=row(gn_w[l]), gn_b=row(gn_b[l]), rel_bias=rel_bias, sinks=sinks[l])

    wkv0_p = jnp.zeros((bp, RWKV_HEADS, HEAD_DIM, HEAD_DIM), F32)
    shift0_p = jnp.zeros((bp, RWKV_PROJ), F32)
    y_p, wkv_p, sh_p, k_p, v_p = _layer(x_prompt, mod_p, wkv0_p, shift0_p, None, None, wts,
                                        1, ROW_TILE, bp, WKV_CHUNK, bp)
    kbuf = cache_win_k[l].reshape(bs, wb, ATT_KV_WIDTH)
    vbuf = cache_win_v[l].reshape(bs, wb, ATT_KV_WIDTH)
    y_s, wkv_s, sh_s, k_s, v_s = _layer(x_sample, mod_s, state_wkv[l], state_shift[l], kbuf, vbuf, wts,
                                        ROW_TILE // ts, ts, 8, ts, 8)
    kvshape = lambda z: z.reshape(1, z.shape[0], wb, ATT_KV_HEADS, HEAD_DIM)
    return (y_p, y_s, wkv_p[None], sh_p[None], kvshape(k_p), kvshape(v_p),
            wkv_s[None], sh_s[None], kvshape(k_s), kvshape(v_s))
```

```python
import functools
import math

import numpy as np
import jax
import jax.numpy as jnp
from jax import lax
from jax.experimental import pallas as pl
from jax.experimental.pallas import tpu as pltpu

F32 = jnp.float32
BF16 = jnp.bfloat16

HEAD_DIM = 64
RWKV_HEADS = 8
RWKV_WIDTH = RWKV_HEADS * HEAD_DIM
ATT_HEADS = 8
ATT_KV_HEADS = 2
ATT_GROUPS = ATT_HEADS // ATT_KV_HEADS
ATT_WIDTH = ATT_HEADS * HEAD_DIM
ATT_KV_WIDTH = ATT_KV_HEADS * HEAD_DIM
DECAY_LORA = 64
ICLR_LORA = 64
GATE_LORA = 128
LORA_WIDTH = DECAY_LORA + ICLR_LORA + GATE_LORA
RWKV_PROJ = 3 * RWKV_WIDTH + LORA_WIDTH
WINDOW = 128
N_BUCKETS = 32
MAX_DISTANCE = 128
N_MOD = 9
DEPTH = 1
ALPHA = (2 * DEPTH) ** 0.25
LN_EPS = 1e-5
GN_EPS = 64e-5
NEG_INF = -1e30

VMEM_LIMIT_BYTES = 56 * 1024 * 1024
ROW_TILE = 256
WKV_CHUNK = 64
WKV_CHUNKS_PER_STEP = 4
ATT_QBLOCKS = 4


def _dot(a, b):
    return jnp.dot(a, b, preferred_element_type=F32)


def _dot_nt(a, b):
    return lax.dot_general(a, b, (((1,), (1,)), ((), ())), preferred_element_type=F32)


def _dot_tn(a, b):
    return lax.dot_general(a, b, (((0,), (0,)), ((), ())), preferred_element_type=F32)


def _split2(a):
    hi = a.astype(BF16)
    lo = (a - hi.astype(F32)).astype(BF16)
    return hi, lo


def _head_sums(x):
    pair = 2 * HEAD_DIM
    low = lax.broadcasted_iota(jnp.int32, (x.shape[0], pair), 1) < HEAD_DIM
    outs = []
    for p in range(x.shape[1] // pair):
        xs = x[:, p * pair:(p + 1) * pair]
        s_lo = jnp.sum(jnp.where(low, xs, 0.0), axis=-1, keepdims=True)
        s_hi = jnp.sum(jnp.where(low, 0.0, xs), axis=-1, keepdims=True)
        outs.append(jnp.where(low, s_lo, s_hi))
    return jnp.concatenate(outs, axis=-1)


def _silu(x):
    return x * jax.nn.sigmoid(x)


def _swiglu(u, w1_ref, w2_ref, d_ff):
    gate = _dot(u, w1_ref[:, :d_ff])
    up = _dot(u, w1_ref[:, d_ff:])
    return _dot((_silu(gate) * up).astype(BF16), w2_ref[...])


def _post_ln(h, g, b):
    mu = jnp.mean(h, axis=-1, keepdims=True)
    d = h - mu
    var = jnp.mean(d * d, axis=-1, keepdims=True)
    return d * lax.rsqrt(var + LN_EPS) * g + b


def _const_spec(shape):
    nd = len(shape)
    return pl.BlockSpec(shape, lambda *_: (0,) * nd, pipeline_mode=pl.Buffered(1))


def _ada_kernel(c_ref, w_ref, b_ref, o_ref):
    s = _silu(c_ref[...]).astype(BF16)
    o_ref[...] = _dot(s, w_ref[...].astype(BF16)) + b_ref[...]


def _ada(c_all, w_ada, b_ada):
    nb, d = c_all.shape
    n = w_ada.shape[1]
    tn = d
    return pl.pallas_call(
        _ada_kernel,
        grid=(n // tn,),
        in_specs=[pl.BlockSpec((nb, d), lambda j: (0, 0)),
                  pl.BlockSpec((d, tn), lambda j: (0, j)),
                  pl.BlockSpec((1, tn), lambda j: (0, j))],
        out_specs=pl.BlockSpec((nb, tn), lambda j: (0, j)),
        out_shape=jax.ShapeDtypeStruct((nb, n), F32),
        name="ada_mod",
    )(c_all, w_ada, b_ada.reshape(1, n))


def _rwkv_prep(p, first, mu, w0, a0, k_k, k_a, r_k, w_lora):
    bb, tt, pw = p.shape
    w = RWKV_WIDTH
    rolled = pltpu.roll(p, 1, axis=1)
    row = lax.broadcasted_iota(jnp.int32, p.shape, 1)
    prev = jnp.where(row == 0, first, rolled)
    xs = (p + (prev - p) * mu).reshape(bb * tt, pw)
    r, k, v = xs[:, :w], xs[:, w:2 * w], xs[:, 2 * w:3 * w]
    lo = xs[:, 3 * w:]
    lane = lax.broadcasted_iota(jnp.int32, lo.shape, 1)
    nl = jnp.where(lane < DECAY_LORA, jnp.tanh(lo),
                   jnp.where(lane < DECAY_LORA + ICLR_LORA, lo, jax.nn.sigmoid(lo)))
    lora = _dot(nl.astype(BF16), w_lora)
    z = -(w0 + lora[:, :w])
    softplus = jnp.maximum(z, 0.0) + jnp.log(1.0 + jnp.exp(-jnp.abs(z)))
    lw = -jnp.exp(-softplus - 0.5)
    a = jax.nn.sigmoid(a0 + lora[:, w:2 * w])
    g = lora[:, 2 * w:]
    kk = k * k_k
    kap = kk / jnp.maximum(jnp.sqrt(_head_sums(kk * kk)), 1e-12)
    kt = k * (1.0 + (a - 1.0) * k_a)
    bonus = _head_sums(r * kt * r_k) * v
    return r, lw, kt, v, kap, kap * a, g, bonus


def _ffn1_kernel(x_ref, mod_ref, shift0_ref, w1_ref, w2_ref, win_ref, lng_ref, lnb_ref,
                 mu_ref, w0_ref, a0_ref, kk_ref, ka_ref, rk_ref, wl_ref,
                 x1_ref, q_ref, kv_ref, shift_ref, *rest, d_ff):
    prep_refs, carry_ref = rest[:-1], rest[-1]
    bb, tt, d = x_ref.shape

    @pl.when(pl.program_id(1) == 0)
    def _():
        carry_ref[...] = shift0_ref[...]

    x = x_ref[...]
    mod = lambda k: mod_ref[:, k:k + 1, :]
    u = (x * (1.0 + mod(1)) + mod(0)).reshape(bb * tt, d).astype(BF16)
    f1 = _swiglu(u, w1_ref, w2_ref, d_ff).reshape(bb, tt, d)
    x1 = _post_ln(ALPHA * x + 0.5 * (1.0 + mod(2)) * f1, lng_ref[0:1, :], lnb_ref[0:1, :])
    x1_ref[...] = x1
    u2 = (x1 * (1.0 + mod(4)) + mod(3)).reshape(bb * tt, d).astype(BF16)
    p = _dot(u2, win_ref[...])
    q_ref[...] = p[:, RWKV_PROJ:RWKV_PROJ + ATT_WIDTH].reshape(bb, tt, ATT_WIDTH)
    kv_ref[...] = p[:, RWKV_PROJ + ATT_WIDTH:].reshape(bb, tt, 2 * ATT_KV_WIDTH)
    p_rwkv = p[:, :RWKV_PROJ].reshape(bb, tt, RWKV_PROJ)
    first = carry_ref[...]
    last = p_rwkv[:, tt - 1:tt, :]
    carry_ref[...] = last
    shift_ref[...] = last
    outs = _rwkv_prep(p_rwkv, first, mu_ref[...], w0_ref[...], a0_ref[...], kk_ref[...], ka_ref[...],
                      rk_ref[...], wl_ref[...])
    for ref, val in zip(prep_refs, outs):
        ref[...] = val.reshape(bb, tt, RWKV_WIDTH)


def _ffn1(x, mod, shift0, w1, w2, win, ln_g, ln_b, mu, w0, a0, k_k, k_a, r_k, w_lora, bb, tt):
    b, t, d = x.shape
    d_ff = w2.shape[0]
    w = RWKV_WIDTH
    tok = lambda width: pl.BlockSpec((bb, tt, width), lambda i, j: (i, j, 0))
    per_seq = pl.BlockSpec((bb, 1, RWKV_PROJ), lambda i, j: (i, 0, 0))
    consts = [w1, w2, win, ln_g, ln_b, mu, w0, a0, k_k, k_a, r_k, w_lora]
    return pl.pallas_call(
        functools.partial(_ffn1_kernel, d_ff=d_ff),
        grid=(b // bb, t // tt),
        in_specs=[tok(d), pl.BlockSpec((bb, N_MOD, d), lambda i, j: (i, 0, 0)), per_seq]
                 + [_const_spec(c.shape) for c in consts],
        out_specs=[tok(d), tok(ATT_WIDTH), tok(2 * ATT_KV_WIDTH), per_seq] + [tok(w)] * 8,
        out_shape=[jax.ShapeDtypeStruct((b, t, d), F32),
                   jax.ShapeDtypeStruct((b, t, ATT_WIDTH), F32),
                   jax.ShapeDtypeStruct((b, t, 2 * ATT_KV_WIDTH), F32),
                   jax.ShapeDtypeStruct((b, 1, RWKV_PROJ), F32)]
                  + [jax.ShapeDtypeStruct((b, t, w), F32)] * 8,
        scratch_shapes=[pltpu.VMEM((bb, 1, RWKV_PROJ), F32)],
        compiler_params=pltpu.CompilerParams(
            dimension_semantics=("arbitrary", "arbitrary"), vmem_limit_bytes=VMEM_LIMIT_BYTES),
        name="ffn1_ln_inproj_prep",
    )(x, mod, shift0, *consts)


def _pdot(asp, bsp, dot):
    (ah, al), (bh, bl) = asp, bsp
    zero = jnp.zeros_like(bl)
    if dot is _dot_nt:
        lhs = jnp.concatenate([ah, al], axis=1)
        rhs = jnp.concatenate([jnp.concatenate([bh, bh], axis=1), jnp.concatenate([bl, zero], axis=1)], axis=0)
        n = bh.shape[0]
    else:
        lhs = jnp.concatenate([ah, al], axis=1 if dot is _dot else 0)
        rhs = jnp.concatenate([jnp.concatenate([bh, bl], axis=1), jnp.concatenate([bh, zero], axis=1)], axis=0)
        n = bh.shape[1]
    out = dot(lhs, rhs)
    return out[:, :n] + out[:, n:]


def _pair_masks(keep0, keep1, use_bf16):
    if use_bf16:
        return keep0, keep1, jnp.where(keep0, 1.0, 0.0).astype(BF16), jnp.where(keep1, 1.0, 0.0).astype(BF16)
    return keep0, keep1, None, None


def _expand(x, pre, masks):
    keep0, keep1, m0, m1 = masks
    if m0 is not None:
        hi, lo = pre
        return (jnp.concatenate([hi * m0, hi * m1], axis=0), jnp.concatenate([lo * m0, lo * m1], axis=0))
    return _split2(jnp.concatenate([jnp.where(keep0, x, 0.0), jnp.where(keep1, x, 0.0)], axis=0))


def _with_split(x):
    return x, _split2(x)


def _wkv_static(units, tri, half, c):
    strict, incl, first_level, levels, eye = tri
    rng = range(len(units))
    ex = lambda xf, pre: _expand(xf, pre, half(xf.shape))
    gb = [_pdot(u["kr"], ex(*u["bi"]), _dot_nt) for u in units]
    gk = [_pdot(u["kr"], ex(*u["ki"]), _dot_nt) for u in units]
    mab = [_with_split(jnp.where(strict, g[:c], 0.0)) for g in gb]
    arb = [_split2(jnp.where(incl, g[c:], 0.0)) for g in gb]
    mk = [_split2(jnp.concatenate([jnp.where(strict, g[:c], 0.0), jnp.where(incl, g[c:], 0.0)], axis=0))
          for g in gk]
    mv = [_pdot(mk[i], ex(*units[i]["v"]), _dot) for i in rng]
    x = [eye - jnp.where(first_level, m[0], 0.0) for m in mab]
    for lvl in levels:
        xs = [_with_split(xi) for xi in x]
        t = [_split2(_pdot(xs[i][1], _expand(*mab[i], lvl), _dot)) for i in rng]
        x = [x[i] - _pdot(t[i], ex(*xs[i]), _dot) for i in rng]
    return [dict(x=_split2(x[i]), mv=mv[i], arb=arb[i]) for i in rng]


def _wkv_dynamic(units, static, states, half, c):
    rng = range(len(units))
    ex = lambda xf, pre: _expand(xf, pre, half(xf.shape))
    st = [_pdot(units[i]["kr"], ex(*_with_split(states[i])), _dot_nt) for i in rng]
    sa = [_pdot(static[i]["x"], ex(*_with_split(st[i][:c] + static[i]["mv"][:c])), _dot) for i in rng]
    y = [st[i][c:] + static[i]["mv"][c:] - _pdot(static[i]["arb"], ex(*_with_split(sa[i])), _dot) for i in rng]
    vsa = [_split2(jnp.concatenate([units[i]["v"][0], -sa[i]], axis=0)) for i in rng]
    cross = [_pdot(vsa[i], units[i]["keb"], _dot_tn) for i in rng]
    n = HEAD_DIM
    low = half((n, 2 * n))[0]
    s_new = [states[i] * units[i]["decay"] + jnp.where(low, cross[i][:n], cross[i][n:]) for i in rng]
    return y, s_new


def _wkv_kernel(r_ref, lw_ref, kt_ref, v_ref, kap_ref, b_ref, s0_ref, y_ref, sfin_ref, s_scr, *, c, group):
    bb, rows_per_step, w = r_ref.shape
    nchunks = rows_per_step // c
    n = HEAD_DIM
    pair = 2 * n
    npairs = w // pair
    j = pl.program_id(1)
    use_bf16 = c % 16 == 0

    @pl.when(j == 0)
    def _():
        for b_ in range(bb):
            for p in range(npairs):
                s_scr[b_, p] = jnp.concatenate([s0_ref[b_, 2 * p], s0_ref[b_, 2 * p + 1]], axis=1)

    ti = lax.broadcasted_iota(jnp.int32, (c, 2 * c), 0)
    jl = lax.broadcasted_iota(jnp.int32, (c, 2 * c), 1)
    jm = jl & (c - 1)
    low = jl < c
    lvl_masks = []
    s = 1
    while s < c:
        same = ((ti ^ jm) & ~(2 * s - 1)) == 0
        lvl_masks.append(same & ((ti & s) != 0) & ((jm & s) == 0))
        s *= 2
    levels = [_pair_masks(m & low, m & ~low, use_bf16) for m in lvl_masks[1:]]
    tri = (ti > jm, ti >= jm, lvl_masks[0], levels, (ti == jm).astype(F32))
    half_cache = {}

    def half(shape):
        if shape not in half_cache:
            lo_ = lax.broadcasted_iota(jnp.int32, shape, 1) < shape[1] // 2
            half_cache[shape] = _pair_masks(lo_, ~lo_, use_bf16)
        return half_cache[shape]

    for shape in ((c, pair), (c, 2 * c), (n, pair)):
        half(shape)
    row = lax.broadcasted_iota(jnp.int32, (c, w), 0)

    def body(gi, carry):
        units, where = [], []
        for ck in range(nchunks):
            rows = slice(ck * c, (ck + 1) * c)
            for k in range(group):
                bi_ = gi * group + k
                lw = lw_ref[bi_, rows, :]
                cl = lw
                s_ = 1
                while s_ < c:
                    cl = cl + jnp.where(row >= s_, pltpu.roll(cl, s_, axis=0), 0.0)
                    s_ *= 2
                last = cl[c - 1:c, :]
                e_inv = jnp.exp(-cl)
                e_end = jnp.exp(last - cl)
                kt = kt_ref[bi_, rows, :]
                bv = b_ref[bi_, rows, :]
                wide = dict(
                    kr=jnp.concatenate([kap_ref[bi_, rows, :] * jnp.exp(cl - lw),
                                        r_ref[bi_, rows, :] * jnp.exp(cl)], axis=0),
                    bi=bv * e_inv, ki=kt * e_inv, v=v_ref[bi_, rows, :],
                    keb=jnp.concatenate([kt * e_end, bv * e_end], axis=0))
                split = {key: _split2(val) for key, val in wide.items()}
                decay = jnp.exp(last)
                for p in range(npairs):
                    sl = slice(p * pair, (p + 1) * pair)
                    cut = lambda key: (split[key][0][:, sl], split[key][1][:, sl])
                    units.append(dict(kr=cut("kr"), keb=cut("keb"),
                                      bi=(wide["bi"][:, sl], cut("bi")), ki=(wide["ki"][:, sl], cut("ki")),
                                      v=(wide["v"][:, sl], cut("v")), decay=decay[:, sl]))
                    where.append((bi_, p, rows, sl))
        static = _wkv_static(units, tri, half, c)
        per_chunk = group * npairs
        states = [s_scr[bi_, p] for (bi_, p, _, _) in where[:per_chunk]]
        for ck in range(nchunks):
            sel = slice(ck * per_chunk, (ck + 1) * per_chunk)
            ys, states = _wkv_dynamic(units[sel], static[sel], states, half, c)
            for (bi_, p, rows, sl), y in zip(where[sel], ys):
                y_ref[bi_, rows, sl] = y
        for (bi_, p, _, _), sn in zip(where[:per_chunk], states):
            s_scr[bi_, p] = sn
        return carry

    if bb == group:
        body(0, 0)
    else:
        lax.fori_loop(0, bb // group, body, 0)

    @pl.when(j == pl.num_programs(1) - 1)
    def _():
        for b_ in range(bb):
            for p in range(npairs):
                sp_ = s_scr[b_, p]
                sfin_ref[b_, 2 * p] = sp_[:, :n]
                sfin_ref[b_, 2 * p + 1] = sp_[:, n:]


def _wkv(r, lw, kt, v, kap, bvec, s0, bb, c, nchunks, group):
    b, t, w = r.shape
    h = w // HEAD_DIM
    rows = c * nchunks
    tok = pl.BlockSpec((bb, rows, w), lambda i, j: (i, j, 0))
    st = pl.BlockSpec((bb, h, HEAD_DIM, HEAD_DIM), lambda i, j: (i, 0, 0, 0))
    return pl.pallas_call(
        functools.partial(_wkv_kernel, c=c, group=group),
        grid=(b // bb, t // rows),
        in_specs=[tok] * 6 + [st],
        out_specs=[tok, st],
        out_shape=[jax.ShapeDtypeStruct((b, t, w), F32),
                   jax.ShapeDtypeStruct((b, h, HEAD_DIM, HEAD_DIM), F32)],
        scratch_shapes=[pltpu.VMEM((bb, h // 2, HEAD_DIM, 2 * HEAD_DIM), F32)],
        compiler_params=pltpu.CompilerParams(dimension_semantics=("arbitrary", "arbitrary")),
        name="wkv_scan",
    )(r, lw, kt, v, kap, bvec, s0)


def _rel_bucket_np(dist):
    max_exact = N_BUCKETS // 2
    n = np.maximum(dist, 0)
    n_f = np.maximum(n, max_exact).astype(np.float32)
    large = max_exact + (np.log(n_f / np.float32(max_exact)) / np.float32(math.log(MAX_DISTANCE / max_exact))
                         * np.float32(N_BUCKETS - max_exact)).astype(np.int32)
    return np.where(n < max_exact, n, np.minimum(large, N_BUCKETS - 1)).astype(np.int32)


def _bias_table(idx, mask, rb_ref, h):
    acc = jnp.zeros(idx.shape, F32)
    for bk in range(N_BUCKETS):
        acc = jnp.where(idx == bk, rb_ref[bk, h], acc)
    return jnp.where(mask, acc, NEG_INF)


def _softmax_sink_pv(s_heads, v_heads, sinks):
    ms = []
    for parts, sink in zip(s_heads, sinks):
        m = sink
        for s in parts:
            m = jnp.maximum(m, jnp.max(s, axis=-1, keepdims=True))
        ms.append(m)
    es = [[jnp.exp(s - m) for s in parts] for parts, m in zip(s_heads, ms)]
    dens = []
    for parts, sink, m in zip(es, sinks, ms):
        den = jnp.exp(sink - m)
        for e in parts:
            den = den + jnp.sum(e, axis=-1, keepdims=True)
        dens.append(den)
    outs = []
    for parts, vparts in zip(es, v_heads):
        o = None
        for e, (v, dot) in zip(parts, vparts):
            pv = dot(e.astype(BF16), v)
            o = pv if o is None else o + pv
        outs.append(o)
    return [o / den for o, den in zip(outs, dens)]


def _attn_prompt_kernel(q_ref, kvc_ref, kvp_ref, idx_ref, rb_ref, sink_ref, o_ref, tb_ref):
    n = HEAD_DIM
    wq = WINDOW
    wk = 2 * wq
    nblk = q_ref.shape[1] // wq
    first = (pl.program_id(0) == 0) & (pl.program_id(1) == 0)

    @pl.when(first)
    def _():
        qi = lax.broadcasted_iota(jnp.int32, (wq, wk), 0)
        kj = lax.broadcasted_iota(jnp.int32, (wq, wk), 1)
        dist = qi + wq - kj
        mask = (dist >= 0) & (dist < WINDOW)
        idx = idx_ref[...]
        for h in range(ATT_HEADS):
            tb_ref[h] = _bias_table(idx, mask, rb_ref, h)

    kj = lax.broadcasted_iota(jnp.int32, (wq, wk), 1)
    dead = (pl.program_id(1) == 0) & (kj < wq)
    q = q_ref[0]
    kv = jnp.concatenate([kvp_ref[0], kvc_ref[0]], axis=0).astype(BF16)
    s_heads, v_heads, sinks, where = [], [], [], []
    for blk in range(nblk):
        rows = slice(blk * wq, blk * wq + wk)
        for h2 in range(ATT_KV_HEADS):
            k_all = kv[rows, h2 * n:(h2 + 1) * n]
            v_all = kv[rows, ATT_KV_WIDTH + h2 * n:ATT_KV_WIDTH + (h2 + 1) * n]
            for g in range(ATT_GROUPS):
                h = h2 * ATT_GROUPS + g
                qh = q[blk * wq:(blk + 1) * wq, h * n:(h + 1) * n].astype(BF16)
                s = _dot_nt(qh, k_all) * (HEAD_DIM ** -0.5) + tb_ref[h]
                s_heads.append([jnp.where(dead, NEG_INF, s) if blk == 0 else s])
                v_heads.append([(v_all, _dot)])
                sinks.append(sink_ref[0, h])
                where.append((blk, h))
    outs = _softmax_sink_pv(s_heads, v_heads, sinks)
    for (blk, h), o in zip(where, outs):
        o_ref[0, blk * wq:(blk + 1) * wq, h * n:(h + 1) * n] = o


def _attn_prompt(q, kv, rel_bias, sinks):
    b, t, _ = q.shape
    wq = WINDOW
    dist = np.arange(wq)[:, None] + wq - np.arange(2 * wq)[None, :]
    idx = jnp.asarray(_rel_bucket_np(dist))
    smem = pl.BlockSpec(memory_space=pltpu.SMEM)
    nblk = ATT_QBLOCKS
    tq = nblk * wq
    return pl.pallas_call(
        _attn_prompt_kernel,
        grid=(b, t // tq),
        in_specs=[pl.BlockSpec((1, tq, ATT_WIDTH), lambda i, j: (i, j, 0)),
                  pl.BlockSpec((1, tq, 2 * ATT_KV_WIDTH), lambda i, j: (i, j, 0)),
                  pl.BlockSpec((1, wq, 2 * ATT_KV_WIDTH), lambda i, j: (i, jnp.maximum(j * nblk - 1, 0), 0)),
                  pl.BlockSpec(idx.shape, lambda i, j: (0, 0)),
                  smem, smem],
        out_specs=pl.BlockSpec((1, tq, ATT_WIDTH), lambda i, j: (i, j, 0)),
        out_shape=jax.ShapeDtypeStruct((b, t, ATT_WIDTH), F32),
        scratch_shapes=[pltpu.VMEM((ATT_HEADS, wq, 2 * wq), F32)],
        compiler_params=pltpu.CompilerParams(dimension_semantics=("arbitrary", "arbitrary")),
        name="attn_prompt",
    )(q, kv, kv, idx, rel_bias, sinks.reshape(1, ATT_HEADS))


def _attn_sample_kernel(q_ref, kvn_ref, kb_ref, vb_ref, idxb_ref, idxn_ref, rb_ref, sink_ref,
                        o_ref, kbo_ref, vbo_ref, tbb_ref, tbn_ref):
    n = HEAD_DIM
    bb, t, _ = q_ref.shape
    wb = kb_ref.shape[1]

    @pl.when(pl.program_id(0) == 0)
    def _():
        qi = lax.broadcasted_iota(jnp.int32, (t, wb), 0)
        kj = lax.broadcasted_iota(jnp.int32, (t, wb), 1)
        dist_b = qi + wb - kj
        mask_b = (dist_b >= 0) & (dist_b < WINDOW)
        qn = lax.broadcasted_iota(jnp.int32, (t, t), 0)
        kn = lax.broadcasted_iota(jnp.int32, (t, t), 1)
        dist_n = qn - kn
        mask_n = (dist_n >= 0) & (dist_n < WINDOW)
        for h in range(ATT_HEADS):
            tbb_ref[h] = _bias_table(idxb_ref[...], mask_b, rb_ref, h)
            tbn_ref[h] = _bias_table(idxn_ref[...], mask_n, rb_ref, h)

    q = q_ref[...]
    kvn = kvn_ref[...]
    kb = kb_ref[...]
    vb = vb_ref[...]
    kbo_ref[:, :wb - t, :] = kb[:, t:, :]
    kbo_ref[:, wb - t:, :] = kvn[:, :, :ATT_KV_WIDTH]
    vbo_ref[:, :wb - t, :] = vb[:, t:, :]
    vbo_ref[:, wb - t:, :] = kvn[:, :, ATT_KV_WIDTH:]
    bnt = lambda a, b_: jnp.einsum("bqd,bkd->bqk", a, b_, preferred_element_type=F32)
    bnn = lambda a, b_: jnp.einsum("bqk,bkd->bqd", a, b_, preferred_element_type=F32)
    s_heads, v_heads = [], []
    for h2 in range(ATT_KV_HEADS):
        ks = slice(h2 * n, (h2 + 1) * n)
        vs = slice(ATT_KV_WIDTH + h2 * n, ATT_KV_WIDTH + (h2 + 1) * n)
        k_buf = kb[:, :, ks].astype(BF16)
        v_buf = vb[:, :, ks].astype(BF16)
        k_new = kvn[:, :, ks].astype(BF16)
        v_new = kvn[:, :, vs].astype(BF16)
        for g in range(ATT_GROUPS):
            h = h2 * ATT_GROUPS + g
            qh = q[:, :, h * n:(h + 1) * n].astype(BF16)
            s_heads.append([bnt(qh, k_buf) * (HEAD_DIM ** -0.5) + tbb_ref[h][None],
                            bnt(qh, k_new) * (HEAD_DIM ** -0.5) + tbn_ref[h][None]])
            v_heads.append([(v_buf, bnn), (v_new, bnn)])
    outs = _softmax_sink_pv(s_heads, v_heads, [sink_ref[0, h] for h in range(ATT_HEADS)])
    for h, o in enumerate(outs):
        o_ref[:, :, h * n:(h + 1) * n] = o


def _attn_sample(q, kv, kbuf, vbuf, rel_bias, sinks, bb):
    b, t, _ = q.shape
    wb = kbuf.shape[1]
    dist_b = np.arange(t)[:, None] + wb - np.arange(wb)[None, :]
    dist_n = np.arange(t)[:, None] - np.arange(t)[None, :]
    idx_b = jnp.asarray(_rel_bucket_np(dist_b))
    idx_n = jnp.asarray(_rel_bucket_np(dist_n))
    smem = pl.BlockSpec(memory_space=pltpu.SMEM)
    tok = lambda w: pl.BlockSpec((bb, t, w), lambda i: (i, 0, 0))
    buf = pl.BlockSpec((bb, wb, ATT_KV_WIDTH), lambda i: (i, 0, 0))
    return pl.pallas_call(
        _attn_sample_kernel,
        grid=(b // bb,),
        in_specs=[tok(ATT_WIDTH), tok(2 * ATT_KV_WIDTH), buf, buf,
                  pl.BlockSpec(idx_b.shape, lambda i: (0, 0)),
                  pl.BlockSpec(idx_n.shape, lambda i: (0, 0)),
                  smem, smem],
        out_specs=[tok(ATT_WIDTH), buf, buf],
        out_shape=[jax.ShapeDtypeStruct((b, t, ATT_WIDTH), F32),
                   jax.ShapeDtypeStruct(kbuf.shape, F32),
                   jax.ShapeDtypeStruct(vbuf.shape, F32)],
        scratch_shapes=[pltpu.VMEM((ATT_HEADS, t, wb), F32), pltpu.VMEM((ATT_HEADS, t, t), F32)],
        compiler_params=pltpu.CompilerParams(dimension_semantics=("arbitrary",)),
        name="attn_sample",
    )(q, kv, kbuf, vbuf, idx_b, idx_n, rel_bias, sinks.reshape(1, ATT_HEADS))


def _out_kernel(x1_ref, mod_ref, y_ref, g_ref, bonus_ref, att_ref, gnw_ref, gnb_ref,
                wo_ref, w1_ref, w2_ref, lng_ref, lnb_ref, o_ref, *, d_ff):
    bb, tt, d = x1_ref.shape
    w = RWKV_WIDTH
    m = bb * tt
    x1 = x1_ref[...]
    mod = lambda k: mod_ref[:, k:k + 1, :]
    y = y_ref[...].reshape(m, w)
    mu = _head_sums(y) * (1.0 / HEAD_DIM)
    dy = y - mu
    var = _head_sums(dy * dy) * (1.0 / HEAD_DIM)
    yn = dy * lax.rsqrt(var + GN_EPS) * gnw_ref[...] + gnb_ref[...]
    y_rwkv = (yn + bonus_ref[...].reshape(m, w)) * g_ref[...].reshape(m, w)
    mix = (_dot(y_rwkv.astype(BF16), wo_ref[:w, :])
           + _dot(att_ref[...].reshape(m, ATT_WIDTH).astype(BF16), wo_ref[w:, :])).reshape(bb, tt, d)
    x2 = _post_ln(ALPHA * x1 + (1.0 + mod(5)) * mix, lng_ref[1:2, :], lnb_ref[1:2, :])
    u = (x2 * (1.0 + mod(7)) + mod(6)).reshape(m, d).astype(BF16)
    f2 = _swiglu(u, w1_ref, w2_ref, d_ff).reshape(bb, tt, d)
    o_ref[...] = _post_ln(ALPHA * x2 + 0.5 * (1.0 + mod(8)) * f2, lng_ref[2:3, :], lnb_ref[2:3, :])


def _out(x1, mod, y, g, bonus, att, gn_w, gn_b, wo, w1, w2, ln_g, ln_b, bb, tt):
    b, t, d = x1.shape
    d_ff = w2.shape[0]
    tok = lambda w: pl.BlockSpec((bb, tt, w), lambda i, j: (i, j, 0))
    consts = [gn_w, gn_b, wo, w1, w2, ln_g, ln_b]
    return pl.pallas_call(
        functools.partial(_out_kernel, d_ff=d_ff),
        grid=(b // bb, t // tt),
        in_specs=[tok(d), pl.BlockSpec((bb, N_MOD, d), lambda i, j: (i, 0, 0)),
                  tok(RWKV_WIDTH), tok(RWKV_WIDTH), tok(RWKV_WIDTH), tok(ATT_WIDTH)]
                 + [_const_spec(c.shape) for c in consts],
        out_specs=tok(d),
        out_shape=jax.ShapeDtypeStruct((b, t, d), F32),
        compiler_params=pltpu.CompilerParams(
            dimension_semantics=("arbitrary", "arbitrary"), vmem_limit_bytes=VMEM_LIMIT_BYTES),
        name="mix_ln_ffn2",
    )(x1, mod, y, g, bonus, att, *consts)


def _layer(x, mod, wkv0, shift0, kbuf, vbuf, wts, bb, tt, wkv_bb, wkv_c, wkv_chunks, wkv_group):
    b, t, d = x.shape
    x1, q, kv, shift_new, r, lw, kt, v, kap, bvec, g, bonus = _ffn1(
        x, mod, shift0[:, None, :], wts["w1a"], wts["w2a"], wts["win"], wts["ln_g"], wts["ln_b"],
        wts["mu"], wts["w0"], wts["a0"], wts["k_k"], wts["k_a"], wts["r_k"], wts["w_lora"], bb, tt)
    y, s_fin = _wkv(r, lw, kt, v, kap, bvec, wkv0, wkv_bb, wkv_c, wkv_chunks, wkv_group)
    shift_new = shift_new[:, 0, :]
    if kbuf is None:
        att = _attn_prompt(q, kv, wts["rel_bias"], wts["sinks"])
        wb = WINDOW
        kb_new = kv[:, t - wb:, :ATT_KV_WIDTH]
        vb_new = kv[:, t - wb:, ATT_KV_WIDTH:]
    else:
        att, kb_new, vb_new = _attn_sample(q, kv, kbuf, vbuf, wts["rel_bias"], wts["sinks"], bb)
    out = _out(x1, mod, y, g, bonus, att, wts["gn_w"], wts["gn_b"], wts["wo"],
               wts["w1b"], wts["w2b"], wts["ln_g"], wts["ln_b"], bb, tt)
    return out, s_fin, shift_new, kb_new, vb_new


def kernel(x_prompt, x_sample, state_wkv, state_shift, cache_win_k, cache_win_v, c_prompt, c_sample, rel_bias, w_ada, b_ada, ln_g, ln_b, w_ffn1_in, w_ffn1_out, w_in, mu_shift, w0, w_decay, a0, w_iclr, w_gate, k_k, k_a, r_k, gn_w, gn_b, sinks, w_out, w_ffn2_in, w_ffn2_out):
    bp, tp, d = x_prompt.shape
    bs, ts, _ = x_sample.shape
    depth = w_ada.shape[0]
    assert depth == 1
    l = 0
    w = RWKV_WIDTH
    wb = cache_win_k.shape[2]

    mod = _ada(jnp.concatenate([c_prompt, c_sample], axis=0), w_ada[l], b_ada[l])
    mod = mod.reshape(bp + bs, N_MOD, d)
    mod_p, mod_s = mod[:bp], mod[bp:]

    w_lora = jnp.zeros((LORA_WIDTH, 3 * w), F32)
    w_lora = w_lora.at[:DECAY_LORA, :w].set(w_decay[l])
    w_lora = w_lora.at[DECAY_LORA:DECAY_LORA + ICLR_LORA, w:2 * w].set(w_iclr[l])
    w_lora = w_lora.at[DECAY_LORA + ICLR_LORA:, 2 * w:].set(w_gate[l])
    row = lambda z: z.reshape(1, -1)
    wts = dict(
        w1a=w_ffn1_in[l].astype(BF16), w2a=w_ffn1_out[l].astype(BF16), win=w_in[l].astype(BF16),
        w1b=w_ffn2_in[l].astype(BF16), w2b=w_ffn2_out[l].astype(BF16), wo=w_out[l].astype(BF16),
        ln_g=ln_g[l], ln_b=ln_b[l], mu=row(mu_shift[l]), w0=row(w0[l]), a0=row(a0[l]),
        k_k=row(k_k[l]), k_a=row(k_a[l]), r_k=row(r_k[l]), w_lora=w_lora.astype(BF16),
        gn_w=row(gn_w[l]), gn_b=row(gn_b[l]), rel_bias=rel_bias, sinks=sinks[l])

    wkv0_p = jnp.zeros((bp, RWKV_HEADS, HEAD_DIM, HEAD_DIM), F32)
    shift0_p = jnp.zeros((bp, RWKV_PROJ), F32)
    y_p, wkv_p, sh_p, k_p, v_p = _layer(x_prompt, mod_p, wkv0_p, shift0_p, None, None, wts,
                                        1, ROW_TILE, bp, WKV_CHUNK, WKV_CHUNKS_PER_STEP, bp)
    kbuf = cache_win_k[l].reshape(bs, wb, ATT_KV_WIDTH)
    vbuf = cache_win_v[l].reshape(bs, wb, ATT_KV_WIDTH)
    y_s, wkv_s, sh_s, k_s, v_s = _layer(x_sample, mod_s, state_wkv[l], state_shift[l], kbuf, vbuf, wts,
                                        ROW_TILE // ts, ts, 8, ts, 1, 8)
    kvshape = lambda z: z.reshape(1, z.shape[0], wb, ATT_KV_HEADS, HEAD_DIM)
    return (y_p, y_s, wkv_p[None], sh_p[None], kvshape(k_p), kvshape(v_p),
            wkv_s[None], sh_s[None], kvshape(k_s), kvshape(v_s))
```

```python
import functools
import math

import numpy as np
import jax
import jax.numpy as jnp
from jax import lax
from jax.experimental import pallas as pl
from jax.experimental.pallas import tpu as pltpu

F32 = jnp.float32
BF16 = jnp.bfloat16

HEAD_DIM = 64
RWKV_HEADS = 8
RWKV_WIDTH = RWKV_HEADS * HEAD_DIM
ATT_HEADS = 8
ATT_KV_HEADS = 2
ATT_GROUPS = ATT_HEADS // ATT_KV_HEADS
ATT_WIDTH = ATT_HEADS * HEAD_DIM
ATT_KV_WIDTH = ATT_KV_HEADS * HEAD_DIM
DECAY_LORA = 64
ICLR_LORA = 64
GATE_LORA = 128
LORA_WIDTH = DECAY_LORA + ICLR_LORA + GATE_LORA
RWKV_PROJ = 3 * RWKV_WIDTH + LORA_WIDTH
WINDOW = 128
N_BUCKETS = 32
MAX_DISTANCE = 128
N_MOD = 9
DEPTH = 1
ALPHA = (2 * DEPTH) ** 0.25
LN_EPS = 1e-5
GN_EPS = 64e-5
NEG_INF = -1e30

VMEM_LIMIT_BYTES = 56 * 1024 * 1024
ROW_TILE = 256
OUT_TILE_FACTOR = 2
WKV_CHUNK = 64
WKV_CHUNKS_PER_STEP = 4
ATT_QBLOCKS = 4


def _dot(a, b):
    return jnp.dot(a, b, preferred_element_type=F32)


def _dot_nt(a, b):
    return lax.dot_general(a, b, (((1,), (1,)), ((), ())), preferred_element_type=F32)


def _dot_tn(a, b):
    return lax.dot_general(a, b, (((0,), (0,)), ((), ())), preferred_element_type=F32)


def _split2(a):
    hi = a.astype(BF16)
    lo = (a - hi.astype(F32)).astype(BF16)
    return hi, lo


def _head_sums(x):
    pair = 2 * HEAD_DIM
    low = lax.broadcasted_iota(jnp.int32, (x.shape[0], pair), 1) < HEAD_DIM
    outs = []
    for p in range(x.shape[1] // pair):
        xs = x[:, p * pair:(p + 1) * pair]
        s_lo = jnp.sum(jnp.where(low, xs, 0.0), axis=-1, keepdims=True)
        s_hi = jnp.sum(jnp.where(low, 0.0, xs), axis=-1, keepdims=True)
        outs.append(jnp.where(low, s_lo, s_hi))
    return jnp.concatenate(outs, axis=-1)


def _silu(x):
    return x * jax.nn.sigmoid(x)


def _swiglu(u, w1_ref, w2_ref, d_ff):
    gate = _dot(u, w1_ref[:, :d_ff])
    up = _dot(u, w1_ref[:, d_ff:])
    return _dot((_silu(gate) * up).astype(BF16), w2_ref[...])


def _post_ln(h, g, b):
    mu = jnp.mean(h, axis=-1, keepdims=True)
    d = h - mu
    var = jnp.mean(d * d, axis=-1, keepdims=True)
    return d * lax.rsqrt(var + LN_EPS) * g + b


def _const_spec(shape):
    nd = len(shape)
    return pl.BlockSpec(shape, lambda *_: (0,) * nd, pipeline_mode=pl.Buffered(1))


def _ada_kernel(c_ref, w_ref, b_ref, o_ref):
    s = _silu(c_ref[...]).astype(BF16)
    o_ref[...] = _dot(s, w_ref[...].astype(BF16)) + b_ref[...]


def _ada(c_all, w_ada, b_ada):
    nb, d = c_all.shape
    n = w_ada.shape[1]
    tn = d
    return pl.pallas_call(
        _ada_kernel,
        grid=(n // tn,),
        in_specs=[pl.BlockSpec((nb, d), lambda j: (0, 0)),
                  pl.BlockSpec((d, tn), lambda j: (0, j)),
                  pl.BlockSpec((1, tn), lambda j: (0, j))],
        out_specs=pl.BlockSpec((nb, tn), lambda j: (0, j)),
        out_shape=jax.ShapeDtypeStruct((nb, n), F32),
        name="ada_mod",
    )(c_all, w_ada, b_ada.reshape(1, n))


def _rwkv_prep(p, first, mu, w0, a0, k_k, k_a, r_k, w_lora):
    bb, tt, pw = p.shape
    w = RWKV_WIDTH
    rolled = pltpu.roll(p, 1, axis=1)
    row = lax.broadcasted_iota(jnp.int32, p.shape, 1)
    prev = jnp.where(row == 0, first, rolled)
    xs = (p + (prev - p) * mu).reshape(bb * tt, pw)
    r, k, v = xs[:, :w], xs[:, w:2 * w], xs[:, 2 * w:3 * w]
    lo = xs[:, 3 * w:]
    lane = lax.broadcasted_iota(jnp.int32, lo.shape, 1)
    nl = jnp.where(lane < DECAY_LORA, jnp.tanh(lo),
                   jnp.where(lane < DECAY_LORA + ICLR_LORA, lo, jax.nn.sigmoid(lo)))
    lora = _dot(nl.astype(BF16), w_lora)
    lw = -math.exp(-0.5) * jax.nn.sigmoid(w0 + lora[:, :w])
    a = jax.nn.sigmoid(a0 + lora[:, w:2 * w])
    g = lora[:, 2 * w:]
    kk = k * k_k
    kap = kk * lax.rsqrt(jnp.maximum(_head_sums(kk * kk), 1e-24))
    kt = k * (1.0 + (a - 1.0) * k_a)
    bonus = _head_sums(r * kt * r_k) * v
    return r, lw, kt, v, kap, kap * a, g, bonus


def _ffn1_kernel(x_ref, mod_ref, shift0_ref, w1_ref, w2_ref, win_ref, lng_ref, lnb_ref,
                 mu_ref, w0_ref, a0_ref, kk_ref, ka_ref, rk_ref, wl_ref,
                 x1_ref, q_ref, kv_ref, shift_ref, *rest, d_ff):
    prep_refs, carry_ref = rest[:-1], rest[-1]
    bb, tt, d = x_ref.shape

    @pl.when(pl.program_id(1) == 0)
    def _():
        carry_ref[...] = shift0_ref[...]

    x = x_ref[...]
    mod = lambda k: mod_ref[:, k:k + 1, :]
    u = (x * (1.0 + mod(1)) + mod(0)).reshape(bb * tt, d).astype(BF16)
    f1 = _swiglu(u, w1_ref, w2_ref, d_ff).reshape(bb, tt, d)
    x1 = _post_ln(ALPHA * x + 0.5 * (1.0 + mod(2)) * f1, lng_ref[0:1, :], lnb_ref[0:1, :])
    x1_ref[...] = x1
    u2 = (x1 * (1.0 + mod(4)) + mod(3)).reshape(bb * tt, d).astype(BF16)
    p = _dot(u2, win_ref[...])
    q_ref[...] = p[:, RWKV_PROJ:RWKV_PROJ + ATT_WIDTH].reshape(bb, tt, ATT_WIDTH)
    kv_ref[...] = p[:, RWKV_PROJ + ATT_WIDTH:].reshape(bb, tt, 2 * ATT_KV_WIDTH)
    p_rwkv = p[:, :RWKV_PROJ].reshape(bb, tt, RWKV_PROJ)
    first = carry_ref[...]
    last = p_rwkv[:, tt - 1:tt, :]
    carry_ref[...] = last
    shift_ref[...] = last
    outs = _rwkv_prep(p_rwkv, first, mu_ref[...], w0_ref[...], a0_ref[...], kk_ref[...], ka_ref[...],
                      rk_ref[...], wl_ref[...])
    for ref, val in zip(prep_refs, outs):
        ref[...] = val.reshape(bb, tt, RWKV_WIDTH)


def _ffn1(x, mod, shift0, w1, w2, win, ln_g, ln_b, mu, w0, a0, k_k, k_a, r_k, w_lora, bb, tt):
    b, t, d = x.shape
    d_ff = w2.shape[0]
    w = RWKV_WIDTH
    tok = lambda width: pl.BlockSpec((bb, tt, width), lambda i, j: (i, j, 0))
    per_seq = pl.BlockSpec((bb, 1, RWKV_PROJ), lambda i, j: (i, 0, 0))
    consts = [w1, w2, win, ln_g, ln_b, mu, w0, a0, k_k, k_a, r_k, w_lora]
    return pl.pallas_call(
        functools.partial(_ffn1_kernel, d_ff=d_ff),
        grid=(b // bb, t // tt),
        in_specs=[tok(d), pl.BlockSpec((bb, N_MOD, d), lambda i, j: (i, 0, 0)), per_seq]
                 + [_const_spec(c.shape) for c in consts],
        out_specs=[tok(d), tok(ATT_WIDTH), tok(2 * ATT_KV_WIDTH), per_seq] + [tok(w)] * 8,
        out_shape=[jax.ShapeDtypeStruct((b, t, d), F32),
                   jax.ShapeDtypeStruct((b, t, ATT_WIDTH), F32),
                   jax.ShapeDtypeStruct((b, t, 2 * ATT_KV_WIDTH), F32),
                   jax.ShapeDtypeStruct((b, 1, RWKV_PROJ), F32)]
                  + [jax.ShapeDtypeStruct((b, t, w), F32)] * 8,
        scratch_shapes=[pltpu.VMEM((bb, 1, RWKV_PROJ), F32)],
        compiler_params=pltpu.CompilerParams(
            dimension_semantics=("arbitrary", "arbitrary"), vmem_limit_bytes=VMEM_LIMIT_BYTES),
        name="ffn1_ln_inproj_prep",
    )(x, mod, shift0, *consts)


def _pdot(asp, bsp, dot):
    (ah, al), (bh, bl) = asp, bsp
    zero = jnp.zeros_like(bl)
    if dot is _dot_nt:
        lhs = jnp.concatenate([ah, al], axis=1)
        rhs = jnp.concatenate([jnp.concatenate([bh, bh], axis=1), jnp.concatenate([bl, zero], axis=1)], axis=0)
        n = bh.shape[0]
    else:
        lhs = jnp.concatenate([ah, al], axis=1 if dot is _dot else 0)
        rhs = jnp.concatenate([jnp.concatenate([bh, bl], axis=1), jnp.concatenate([bh, zero], axis=1)], axis=0)
        n = bh.shape[1]
    out = dot(lhs, rhs)
    return out[:, :n] + out[:, n:]


def _pair_masks(keep0, keep1, use_bf16):
    if use_bf16:
        return keep0, keep1, jnp.where(keep0, 1.0, 0.0).astype(BF16), jnp.where(keep1, 1.0, 0.0).astype(BF16)
    return keep0, keep1, None, None


def _expand(x, pre, masks):
    keep0, keep1, m0, m1 = masks
    if m0 is not None:
        hi, lo = pre
        return (jnp.concatenate([hi * m0, hi * m1], axis=0), jnp.concatenate([lo * m0, lo * m1], axis=0))
    return _split2(jnp.concatenate([jnp.where(keep0, x, 0.0), jnp.where(keep1, x, 0.0)], axis=0))


def _with_split(x):
    return x, _split2(x)


def _wkv_static(units, tri, half, c):
    strict, incl, first_level, levels, eye = tri
    rng = range(len(units))
    ex = lambda xf, pre: _expand(xf, pre, half(xf.shape))
    gb = [_pdot(u["kr"], ex(*u["bi"]), _dot_nt) for u in units]
    gk = [_pdot(u["kr"], ex(*u["ki"]), _dot_nt) for u in units]
    mab = [_with_split(jnp.where(strict, g[:c], 0.0)) for g in gb]
    arb = [_split2(jnp.where(incl, g[c:], 0.0)) for g in gb]
    mk = [_split2(jnp.concatenate([jnp.where(strict, g[:c], 0.0), jnp.where(incl, g[c:], 0.0)], axis=0))
          for g in gk]
    mv = [_pdot(mk[i], ex(*units[i]["v"]), _dot) for i in rng]
    x = [eye - jnp.where(first_level, m[0], 0.0) for m in mab]
    for lvl in levels:
        xs = [_with_split(xi) for xi in x]
        t = [_split2(_pdot(xs[i][1], _expand(*mab[i], lvl), _dot)) for i in rng]
        x = [x[i] - _pdot(t[i], ex(*xs[i]), _dot) for i in rng]
    return [dict(x=_split2(x[i]), mv=mv[i], arb=arb[i]) for i in rng]


def _wkv_dynamic(units, static, states, half, c):
    rng = range(len(units))
    ex = lambda xf, pre: _expand(xf, pre, half(xf.shape))
    st = [_pdot(units[i]["kr"], ex(*_with_split(states[i])), _dot_nt) for i in rng]
    sa = [_pdot(static[i]["x"], ex(*_with_split(st[i][:c] + static[i]["mv"][:c])), _dot) for i in rng]
    y = [st[i][c:] + static[i]["mv"][c:] - _pdot(static[i]["arb"], ex(*_with_split(sa[i])), _dot) for i in rng]
    vsa = [_split2(jnp.concatenate([units[i]["v"][0], -sa[i]], axis=0)) for i in rng]
    cross = [_pdot(vsa[i], units[i]["keb"], _dot_tn) for i in rng]
    n = HEAD_DIM
    low = half((n, 2 * n))[0]
    s_new = [states[i] * units[i]["decay"] + jnp.where(low, cross[i][:n], cross[i][n:]) for i in rng]
    return y, s_new


def _wkv_kernel(r_ref, lw_ref, kt_ref, v_ref, kap_ref, b_ref, s0_ref, y_ref, sfin_ref, s_scr, *, c, group):
    bb, rows_per_step, w = r_ref.shape
    nchunks = rows_per_step // c
    n = HEAD_DIM
    pair = 2 * n
    npairs = w // pair
    j = pl.program_id(1)
    use_bf16 = c % 16 == 0

    @pl.when(j == 0)
    def _():
        for b_ in range(bb):
            for p in range(npairs):
                s_scr[b_, p] = jnp.concatenate([s0_ref[b_, 2 * p], s0_ref[b_, 2 * p + 1]], axis=1)

    ti = lax.broadcasted_iota(jnp.int32, (c, 2 * c), 0)
    jl = lax.broadcasted_iota(jnp.int32, (c, 2 * c), 1)
    jm = jl & (c - 1)
    low = jl < c
    lvl_masks = []
    s = 1
    while s < c:
        same = ((ti ^ jm) & ~(2 * s - 1)) == 0
        lvl_masks.append(same & ((ti & s) != 0) & ((jm & s) == 0))
        s *= 2
    levels = [_pair_masks(m & low, m & ~low, use_bf16) for m in lvl_masks[1:]]
    tri = (ti > jm, ti >= jm, lvl_masks[0], levels, (ti == jm).astype(F32))
    half_cache = {}

    def half(shape):
        if shape not in half_cache:
            lo_ = lax.broadcasted_iota(jnp.int32, shape, 1) < shape[1] // 2
            half_cache[shape] = _pair_masks(lo_, ~lo_, use_bf16)
        return half_cache[shape]

    for shape in ((c, pair), (c, 2 * c), (n, pair)):
        half(shape)
    row = lax.broadcasted_iota(jnp.int32, (c, w), 0)

    def body(gi, carry):
        units, where = [], []
        for ck in range(nchunks):
            rows = slice(ck * c, (ck + 1) * c)
            for k in range(group):
                bi_ = gi * group + k
                lw = lw_ref[bi_, rows, :]
                cl = lw
                s_ = 1
                while s_ < c:
                    cl = cl + jnp.where(row >= s_, pltpu.roll(cl, s_, axis=0), 0.0)
                    s_ *= 2
                last = cl[c - 1:c, :]
                e_inv = jnp.exp(-cl)
                e_end = jnp.exp(last - cl)
                kt = kt_ref[bi_, rows, :]
                bv = b_ref[bi_, rows, :]
                wide = dict(
                    kr=jnp.concatenate([kap_ref[bi_, rows, :] * jnp.exp(cl - lw),
                                        r_ref[bi_, rows, :] * jnp.exp(cl)], axis=0),
                    bi=bv * e_inv, ki=kt * e_inv, v=v_ref[bi_, rows, :],
                    keb=jnp.concatenate([kt * e_end, bv * e_end], axis=0))
                split = {key: _split2(val) for key, val in wide.items()}
                decay = jnp.exp(last)
                for p in range(npairs):
                    sl = slice(p * pair, (p + 1) * pair)
                    cut = lambda key: (split[key][0][:, sl], split[key][1][:, sl])
                    units.append(dict(kr=cut("kr"), keb=cut("keb"),
                                      bi=(wide["bi"][:, sl], cut("bi")), ki=(wide["ki"][:, sl], cut("ki")),
                                      v=(wide["v"][:, sl], cut("v")), decay=decay[:, sl]))
                    where.append((bi_, p, rows, sl))
        static = _wkv_static(units, tri, half, c)
        per_chunk = group * npairs
        states = [s_scr[bi_, p] for (bi_, p, _, _) in where[:per_chunk]]
        for ck in range(nchunks):
            sel = slice(ck * per_chunk, (ck + 1) * per_chunk)
            ys, states = _wkv_dynamic(units[sel], static[sel], states, half, c)
            for (bi_, p, rows, sl), y in zip(where[sel], ys):
                y_ref[bi_, rows, sl] = y
        for (bi_, p, _, _), sn in zip(where[:per_chunk], states):
            s_scr[bi_, p] = sn
        return carry

    if bb == group:
        body(0, 0)
    else:
        lax.fori_loop(0, bb // group, body, 0)

    @pl.when(j == pl.num_programs(1) - 1)
    def _():
        for b_ in range(bb):
            for p in range(npairs):
                sp_ = s_scr[b_, p]
                sfin_ref[b_, 2 * p] = sp_[:, :n]
                sfin_ref[b_, 2 * p + 1] = sp_[:, n:]


def _wkv(r, lw, kt, v, kap, bvec, s0, bb, c, nchunks, group):
    b, t, w = r.shape
    h = w // HEAD_DIM
    rows = c * nchunks
    tok = pl.BlockSpec((bb, rows, w), lambda i, j: (i, j, 0))
    st = pl.BlockSpec((bb, h, HEAD_DIM, HEAD_DIM), lambda i, j: (i, 0, 0, 0))
    return pl.pallas_call(
        functools.partial(_wkv_kernel, c=c, group=group),
        grid=(b // bb, t // rows),
        in_specs=[tok] * 6 + [st],
        out_specs=[tok, st],
        out_shape=[jax.ShapeDtypeStruct((b, t, w), F32),
                   jax.ShapeDtypeStruct((b, h, HEAD_DIM, HEAD_DIM), F32)],
        scratch_shapes=[pltpu.VMEM((bb, h // 2, HEAD_DIM, 2 * HEAD_DIM), F32)],
        compiler_params=pltpu.CompilerParams(dimension_semantics=("arbitrary", "arbitrary")),
        name="wkv_scan",
    )(r, lw, kt, v, kap, bvec, s0)


def _rel_bucket_np(dist):
    max_exact = N_BUCKETS // 2
    n = np.maximum(dist, 0)
    n_f = np.maximum(n, max_exact).astype(np.float32)
    large = max_exact + (np.log(n_f / np.float32(max_exact)) / np.float32(math.log(MAX_DISTANCE / max_exact))
                         * np.float32(N_BUCKETS - max_exact)).astype(np.int32)
    return np.where(n < max_exact, n, np.minimum(large, N_BUCKETS - 1)).astype(np.int32)


def _bias_table(idx, mask, rb_ref, h):
    acc = jnp.zeros(idx.shape, F32)
    for bk in range(N_BUCKETS):
        acc = jnp.where(idx == bk, rb_ref[bk, h], acc)
    return jnp.where(mask, acc, NEG_INF)


def _softmax_sink_pv(s_heads, v_heads, sinks):
    ms = []
    for parts, sink in zip(s_heads, sinks):
        m = sink
        for s in parts:
            m = jnp.maximum(m, jnp.max(s, axis=-1, keepdims=True))
        ms.append(m)
    es = [[jnp.exp(s - m) for s in parts] for parts, m in zip(s_heads, ms)]
    dens = []
    for parts, sink, m in zip(es, sinks, ms):
        den = jnp.exp(sink - m)
        for e in parts:
            den = den + jnp.sum(e, axis=-1, keepdims=True)
        dens.append(den)
    outs = []
    for parts, vparts in zip(es, v_heads):
        o = None
        for e, (v, dot) in zip(parts, vparts):
            pv = dot(e.astype(BF16), v)
            o = pv if o is None else o + pv
        outs.append(o)
    return [o / den for o, den in zip(outs, dens)]


def _attn_prompt_kernel(q_ref, kvc_ref, kvp_ref, idx_ref, rb_ref, sink_ref, o_ref, tb_ref):
    n = HEAD_DIM
    wq = WINDOW
    wk = 2 * wq
    nblk = q_ref.shape[1] // wq
    first = (pl.program_id(0) == 0) & (pl.program_id(1) == 0)

    @pl.when(first)
    def _():
        qi = lax.broadcasted_iota(jnp.int32, (wq, wk), 0)
        kj = lax.broadcasted_iota(jnp.int32, (wq, wk), 1)
        dist = qi + wq - kj
        mask = (dist >= 0) & (dist < WINDOW)
        idx = idx_ref[...]
        for h in range(ATT_HEADS):
            tb_ref[h] = _bias_table(idx, mask, rb_ref, h)

    kj = lax.broadcasted_iota(jnp.int32, (wq, wk), 1)
    dead = (pl.program_id(1) == 0) & (kj < wq)
    q = q_ref[0]
    kv = jnp.concatenate([kvp_ref[0], kvc_ref[0]], axis=0).astype(BF16)
    s_heads, v_heads, sinks, where = [], [], [], []
    for blk in range(nblk):
        rows = slice(blk * wq, blk * wq + wk)
        for h2 in range(ATT_KV_HEADS):
            k_all = kv[rows, h2 * n:(h2 + 1) * n]
            v_all = kv[rows, ATT_KV_WIDTH + h2 * n:ATT_KV_WIDTH + (h2 + 1) * n]
            for g in range(ATT_GROUPS):
                h = h2 * ATT_GROUPS + g
                qh = q[blk * wq:(blk + 1) * wq, h * n:(h + 1) * n].astype(BF16)
                s = _dot_nt(qh, k_all) * (HEAD_DIM ** -0.5) + tb_ref[h]
                s_heads.append([jnp.where(dead, NEG_INF, s) if blk == 0 else s])
                v_heads.append([(v_all, _dot)])
                sinks.append(sink_ref[0, h])
                where.append((blk, h))
    outs = _softmax_sink_pv(s_heads, v_heads, sinks)
    for (blk, h), o in zip(where, outs):
        o_ref[0, blk * wq:(blk + 1) * wq, h * n:(h + 1) * n] = o


def _attn_prompt(q, kv, rel_bias, sinks):
    b, t, _ = q.shape
    wq = WINDOW
    dist = np.arange(wq)[:, None] + wq - np.arange(2 * wq)[None, :]
    idx = jnp.asarray(_rel_bucket_np(dist))
    smem = pl.BlockSpec(memory_space=pltpu.SMEM)
    nblk = ATT_QBLOCKS
    tq = nblk * wq
    return pl.pallas_call(
        _attn_prompt_kernel,
        grid=(b, t // tq),
        in_specs=[pl.BlockSpec((1, tq, ATT_WIDTH), lambda i, j: (i, j, 0)),
                  pl.BlockSpec((1, tq, 2 * ATT_KV_WIDTH), lambda i, j: (i, j, 0)),
                  pl.BlockSpec((1, wq, 2 * ATT_KV_WIDTH), lambda i, j: (i, jnp.maximum(j * nblk - 1, 0), 0)),
                  pl.BlockSpec(idx.shape, lambda i, j: (0, 0)),
                  smem, smem],
        out_specs=pl.BlockSpec((1, tq, ATT_WIDTH), lambda i, j: (i, j, 0)),
        out_shape=jax.ShapeDtypeStruct((b, t, ATT_WIDTH), F32),
        scratch_shapes=[pltpu.VMEM((ATT_HEADS, wq, 2 * wq), F32)],
        compiler_params=pltpu.CompilerParams(dimension_semantics=("arbitrary", "arbitrary")),
        name="attn_prompt",
    )(q, kv, kv, idx, rel_bias, sinks.reshape(1, ATT_HEADS))


def _attn_sample_kernel(q_ref, kvn_ref, kb_ref, vb_ref, idxb_ref, idxn_ref, rb_ref, sink_ref,
                        o_ref, kbo_ref, vbo_ref, tbb_ref, tbn_ref):
    n = HEAD_DIM
    bb, t, _ = q_ref.shape
    wb = kb_ref.shape[1]

    @pl.when(pl.program_id(0) == 0)
    def _():
        qi = lax.broadcasted_iota(jnp.int32, (t, wb), 0)
        kj = lax.broadcasted_iota(jnp.int32, (t, wb), 1)
        dist_b = qi + wb - kj
        mask_b = (dist_b >= 0) & (dist_b < WINDOW)
        qn = lax.broadcasted_iota(jnp.int32, (t, t), 0)
        kn = lax.broadcasted_iota(jnp.int32, (t, t), 1)
        dist_n = qn - kn
        mask_n = (dist_n >= 0) & (dist_n < WINDOW)
        for h in range(ATT_HEADS):
            tbb_ref[h] = _bias_table(idxb_ref[...], mask_b, rb_ref, h)
            tbn_ref[h] = _bias_table(idxn_ref[...], mask_n, rb_ref, h)

    q = q_ref[...]
    kvn = kvn_ref[...]
    kb = kb_ref[...]
    vb = vb_ref[...]
    kbo_ref[:, :wb - t, :] = kb[:, t:, :]
    kbo_ref[:, wb - t:, :] = kvn[:, :, :ATT_KV_WIDTH]
    vbo_ref[:, :wb - t, :] = vb[:, t:, :]
    vbo_ref[:, wb - t:, :] = kvn[:, :, ATT_KV_WIDTH:]
    bnt = lambda a, b_: jnp.einsum("bqd,bkd->bqk", a, b_, preferred_element_type=F32)
    bnn = lambda a, b_: jnp.einsum("bqk,bkd->bqd", a, b_, preferred_element_type=F32)
    s_heads, v_heads = [], []
    for h2 in range(ATT_KV_HEADS):
        ks = slice(h2 * n, (h2 + 1) * n)
        vs = slice(ATT_KV_WIDTH + h2 * n, ATT_KV_WIDTH + (h2 + 1) * n)
        k_buf = kb[:, :, ks].astype(BF16)
        v_buf = vb[:, :, ks].astype(BF16)
        k_new = kvn[:, :, ks].astype(BF16)
        v_new = kvn[:, :, vs].astype(BF16)
        for g in range(ATT_GROUPS):
            h = h2 * ATT_GROUPS + g
            qh = q[:, :, h * n:(h + 1) * n].astype(BF16)
            s_heads.append([bnt(qh, k_buf) * (HEAD_DIM ** -0.5) + tbb_ref[h][None],
                            bnt(qh, k_new) * (HEAD_DIM ** -0.5) + tbn_ref[h][None]])
            v_heads.append([(v_buf, bnn), (v_new, bnn)])
    outs = _softmax_sink_pv(s_heads, v_heads, [sink_ref[0, h] for h in range(ATT_HEADS)])
    for h, o in enumerate(outs):
        o_ref[:, :, h * n:(h + 1) * n] = o


def _attn_sample(q, kv, kbuf, vbuf, rel_bias, sinks, bb):
    b, t, _ = q.shape
    wb = kbuf.shape[1]
    dist_b = np.arange(t)[:, None] + wb - np.arange(wb)[None, :]
    dist_n = np.arange(t)[:, None] - np.arange(t)[None, :]
    idx_b = jnp.asarray(_rel_bucket_np(dist_b))
    idx_n = jnp.asarray(_rel_bucket_np(dist_n))
    smem = pl.BlockSpec(memory_space=pltpu.SMEM)
    tok = lambda w: pl.BlockSpec((bb, t, w), lambda i: (i, 0, 0))
    buf = pl.BlockSpec((bb, wb, ATT_KV_WIDTH), lambda i: (i, 0, 0))
    return pl.pallas_call(
        _attn_sample_kernel,
        grid=(b // bb,),
        in_specs=[tok(ATT_WIDTH), tok(2 * ATT_KV_WIDTH), buf, buf,
                  pl.BlockSpec(idx_b.shape, lambda i: (0, 0)),
                  pl.BlockSpec(idx_n.shape, lambda i: (0, 0)),
                  smem, smem],
        out_specs=[tok(ATT_WIDTH), buf, buf],
        out_shape=[jax.ShapeDtypeStruct((b, t, ATT_WIDTH), F32),
                   jax.ShapeDtypeStruct(kbuf.shape, F32),
                   jax.ShapeDtypeStruct(vbuf.shape, F32)],
        scratch_shapes=[pltpu.VMEM((ATT_HEADS, t, wb), F32), pltpu.VMEM((ATT_HEADS, t, t), F32)],
        compiler_params=pltpu.CompilerParams(dimension_semantics=("arbitrary",)),
        name="attn_sample",
    )(q, kv, kbuf, vbuf, idx_b, idx_n, rel_bias, sinks.reshape(1, ATT_HEADS))


def _out_kernel(x1_ref, mod_ref, y_ref, g_ref, bonus_ref, att_ref, gnw_ref, gnb_ref,
                wo_ref, w1_ref, w2_ref, lng_ref, lnb_ref, o_ref, *, d_ff):
    bb, tt, d = x1_ref.shape
    w = RWKV_WIDTH
    m = bb * tt
    x1 = x1_ref[...]
    mod = lambda k: mod_ref[:, k:k + 1, :]
    y = y_ref[...].reshape(m, w)
    mu = _head_sums(y) * (1.0 / HEAD_DIM)
    dy = y - mu
    var = _head_sums(dy * dy) * (1.0 / HEAD_DIM)
    yn = dy * lax.rsqrt(var + GN_EPS) * gnw_ref[...] + gnb_ref[...]
    y_rwkv = (yn + bonus_ref[...].reshape(m, w)) * g_ref[...].reshape(m, w)
    mix = (_dot(y_rwkv.astype(BF16), wo_ref[:w, :])
           + _dot(att_ref[...].reshape(m, ATT_WIDTH).astype(BF16), wo_ref[w:, :])).reshape(bb, tt, d)
    x2 = _post_ln(ALPHA * x1 + (1.0 + mod(5)) * mix, lng_ref[1:2, :], lnb_ref[1:2, :])
    u = (x2 * (1.0 + mod(7)) + mod(6)).reshape(m, d).astype(BF16)
    f2 = _swiglu(u, w1_ref, w2_ref, d_ff).reshape(bb, tt, d)
    o_ref[...] = _post_ln(ALPHA * x2 + 0.5 * (1.0 + mod(8)) * f2, lng_ref[2:3, :], lnb_ref[2:3, :])


def _out(x1, mod, y, g, bonus, att, gn_w, gn_b, wo, w1, w2, ln_g, ln_b, bb, tt):
    b, t, d = x1.shape
    d_ff = w2.shape[0]
    tok = lambda w: pl.BlockSpec((bb, tt, w), lambda i, j: (i, j, 0))
    consts = [gn_w, gn_b, wo, w1, w2, ln_g, ln_b]
    return pl.pallas_call(
        functools.partial(_out_kernel, d_ff=d_ff),
        grid=(b // bb, t // tt),
        in_specs=[tok(d), pl.BlockSpec((bb, N_MOD, d), lambda i, j: (i, 0, 0)),
                  tok(RWKV_WIDTH), tok(RWKV_WIDTH), tok(RWKV_WIDTH), tok(ATT_WIDTH)]
                 + [_const_spec(c.shape) for c in consts],
        out_specs=tok(d),
        out_shape=jax.ShapeDtypeStruct((b, t, d), F32),
        compiler_params=pltpu.CompilerParams(
            dimension_semantics=("arbitrary", "arbitrary"), vmem_limit_bytes=VMEM_LIMIT_BYTES),
        name="mix_ln_ffn2",
    )(x1, mod, y, g, bonus, att, *consts)


def _layer(x, mod, wkv0, shift0, kbuf, vbuf, wts, bb, tt, wkv_bb, wkv_c, wkv_chunks, wkv_group):
    b, t, d = x.shape
    x1, q, kv, shift_new, r, lw, kt, v, kap, bvec, g, bonus = _ffn1(
        x, mod, shift0[:, None, :], wts["w1a"], wts["w2a"], wts["win"], wts["ln_g"], wts["ln_b"],
        wts["mu"], wts["w0"], wts["a0"], wts["k_k"], wts["k_a"], wts["r_k"], wts["w_lora"], bb, tt)
    y, s_fin = _wkv(r, lw, kt, v, kap, bvec, wkv0, wkv_bb, wkv_c, wkv_chunks, wkv_group)
    shift_new = shift_new[:, 0, :]
    if kbuf is None:
        att = _attn_prompt(q, kv, wts["rel_bias"], wts["sinks"])
        wb = WINDOW
        kb_new = kv[:, t - wb:, :ATT_KV_WIDTH]
        vb_new = kv[:, t - wb:, ATT_KV_WIDTH:]
    else:
        att, kb_new, vb_new = _attn_sample(q, kv, kbuf, vbuf, wts["rel_bias"], wts["sinks"], bb)
    out = _out(x1, mod, y, g, bonus, att, wts["gn_w"], wts["gn_b"], wts["wo"],
               wts["w1b"], wts["w2b"], wts["ln_g"], wts["ln_b"],
               bb * OUT_TILE_FACTOR if tt < ROW_TILE else bb, tt * OUT_TILE_FACTOR if tt >= ROW_TILE else tt)
    return out, s_fin, shift_new, kb_new, vb_new


def kernel(x_prompt, x_sample, state_wkv, state_shift, cache_win_k, cache_win_v, c_prompt, c_sample, rel_bias, w_ada, b_ada, ln_g, ln_b, w_ffn1_in, w_ffn1_out, w_in, mu_shift, w0, w_decay, a0, w_iclr, w_gate, k_k, k_a, r_k, gn_w, gn_b, sinks, w_out, w_ffn2_in, w_ffn2_out):
    bp, tp, d = x_prompt.shape
    bs, ts, _ = x_sample.shape
    depth = w_ada.shape[0]
    assert depth == 1
    l = 0
    w = RWKV_WIDTH
    wb = cache_win_k.shape[2]

    mod = _ada(jnp.concatenate([c_prompt, c_sample], axis=0), w_ada[l], b_ada[l])
    mod = mod.reshape(bp + bs, N_MOD, d)
    mod_p, mod_s = mod[:bp], mod[bp:]

    w_lora = jnp.zeros((LORA_WIDTH, 3 * w), F32)
    w_lora = w_lora.at[:DECAY_LORA, :w].set(w_decay[l])
    w_lora = w_lora.at[DECAY_LORA:DECAY_LORA + ICLR_LORA, w:2 * w].set(w_iclr[l])
    w_lora = w_lora.at[DECAY_LORA + ICLR_LORA:, 2 * w:].set(w_gate[l])
    row = lambda z: z.reshape(1, -1)
    wts = dict(
        w1a=w_ffn1_in[l].astype(BF16), w2a=w_ffn1_out[l].astype(BF16), win=w_in[l].astype(BF16),
        w1b=w_ffn2_in[l].astype(BF16), w2b=w_ffn2_out[l].astype(BF16), wo=w_out[l].astype(BF16),
        ln_g=ln_g[l], ln_b=ln_b[l], mu=row(mu_shift[l]), w0=row(w0[l]), a0=row(a0[l]),
        k_k=row(k_k[l]), k_a=row(k_a[l]), r_k=row(r_k[l]), w_lora=w_lora.astype(BF16),
        gn_w=row(gn_w[l]), gn_b=row(gn_b[l]), rel_bias=rel_bias, sinks=sinks[l])

    wkv0_p = jnp.zeros((bp, RWKV_HEADS, HEAD_DIM, HEAD_DIM), F32)
    shift0_p = jnp.zeros((bp, RWKV_PROJ), F32)
    y_p, wkv_p, sh_p, k_p, v_p = _layer(x_prompt, mod_p, wkv0_p, shift0_p, None, None, wts,
                                        1, ROW_TILE, bp, WKV_CHUNK, WKV_CHUNKS_PER_STEP, bp)
    kbuf = cache_win_k[l].reshape(bs, wb, ATT_KV_WIDTH)
    vbuf = cache_win_v[l].reshape(bs, wb, ATT_KV_WIDTH)
    y_s, wkv_s, sh_s, k_s, v_s = _layer(x_sample, mod_s, state_wkv[l], state_shift[l], kbuf, vbuf, wts,
                                        ROW_TILE // ts, ts, 8, ts, 1, 8)
    kvshape = lambda z: z.reshape(1, z.shape[0], wb, ATT_KV_HEADS, HEAD_DIM)
    return (y_p, y_s, wkv_p[None], sh_p[None], kvshape(k_p), kvshape(v_p),
            wkv_s[None], sh_s[None], kvshape(k_s), kvshape(v_s))
```

```python
import functools
import math

import numpy as np
import jax
import jax.numpy as jnp
from jax import lax
from jax.experimental import pallas as pl
from jax.experimental.pallas import tpu as pltpu

F32 = jnp.float32
BF16 = jnp.bfloat16

HEAD_DIM = 64
RWKV_HEADS = 8
RWKV_WIDTH = RWKV_HEADS * HEAD_DIM
ATT_HEADS = 8
ATT_KV_HEADS = 2
ATT_GROUPS = ATT_HEADS // ATT_KV_HEADS
ATT_WIDTH = ATT_HEADS * HEAD_DIM
ATT_KV_WIDTH = ATT_KV_HEADS * HEAD_DIM
DECAY_LORA = 64
ICLR_LORA = 64
GATE_LORA = 128
LORA_WIDTH = DECAY_LORA + ICLR_LORA + GATE_LORA
RWKV_PROJ = 3 * RWKV_WIDTH + LORA_WIDTH
WINDOW = 128
N_BUCKETS = 32
MAX_DISTANCE = 128
N_MOD = 9
DEPTH = 1
ALPHA = (2 * DEPTH) ** 0.25
LN_EPS = 1e-5
GN_EPS = 64e-5
NEG_INF = -1e30

VMEM_LIMIT_BYTES = 56 * 1024 * 1024
ROW_TILE = 256
OUT_TILE_FACTOR = 2
WKV_CHUNK = 64
WKV_CHUNKS_PER_STEP = 2
ATT_QBLOCKS = 8


def _dot(a, b):
    return jnp.dot(a, b, preferred_element_type=F32)


def _dot_nt(a, b):
    return lax.dot_general(a, b, (((1,), (1,)), ((), ())), preferred_element_type=F32)


def _dot_tn(a, b):
    return lax.dot_general(a, b, (((0,), (0,)), ((), ())), preferred_element_type=F32)


def _split2(a):
    hi = a.astype(BF16)
    lo = (a - hi.astype(F32)).astype(BF16)
    return hi, lo


def _head_sums(x):
    pair = 2 * HEAD_DIM
    low = lax.broadcasted_iota(jnp.int32, (x.shape[0], pair), 1) < HEAD_DIM
    outs = []
    for p in range(x.shape[1] // pair):
        xs = x[:, p * pair:(p + 1) * pair]
        s_lo = jnp.sum(jnp.where(low, xs, 0.0), axis=-1, keepdims=True)
        s_hi = jnp.sum(jnp.where(low, 0.0, xs), axis=-1, keepdims=True)
        outs.append(jnp.where(low, s_lo, s_hi))
    return jnp.concatenate(outs, axis=-1)


def _silu(x):
    return x * jax.nn.sigmoid(x)


def _swiglu(u, w1_ref, w2_ref, d_ff):
    gate = _dot(u, w1_ref[:, :d_ff])
    up = _dot(u, w1_ref[:, d_ff:])
    return _dot((_silu(gate) * up).astype(BF16), w2_ref[...])


def _post_ln(h, g, b):
    mu = jnp.mean(h, axis=-1, keepdims=True)
    d = h - mu
    var = jnp.mean(d * d, axis=-1, keepdims=True)
    return d * lax.rsqrt(var + LN_EPS) * g + b


def _const_spec(shape):
    nd = len(shape)
    return pl.BlockSpec(shape, lambda *_: (0,) * nd, pipeline_mode=pl.Buffered(1))


def _ada_kernel(cp_ref, cs_ref, w_ref, b_ref, op_ref, os_ref):
    w = w_ref[...].astype(BF16)
    for c_ref, o_ref in ((cp_ref, op_ref), (cs_ref, os_ref)):
        o_ref[...] = _dot(_silu(c_ref[...]).astype(BF16), w) + b_ref[...]


def _ada(c_prompt, c_sample, w_ada, b_ada):
    d, n = w_ada.shape
    tn = d
    rows = lambda c: pl.BlockSpec(c.shape, lambda j: (0, 0))
    cols = lambda c: pl.BlockSpec((c.shape[0], tn), lambda j: (0, j))
    return pl.pallas_call(
        _ada_kernel,
        grid=(n // tn,),
        in_specs=[rows(c_prompt), rows(c_sample),
                  pl.BlockSpec((d, tn), lambda j: (0, j)),
                  pl.BlockSpec((1, tn), lambda j: (0, j))],
        out_specs=[cols(c_prompt), cols(c_sample)],
        out_shape=[jax.ShapeDtypeStruct((c.shape[0], n), F32) for c in (c_prompt, c_sample)],
        name="ada_mod",
    )(c_prompt, c_sample, w_ada, b_ada.reshape(1, n))


def _rwkv_prep(p, first, mu, w0, a0, k_k, k_a, r_k, w_lora):
    bb, tt, pw = p.shape
    w = RWKV_WIDTH
    rolled = pltpu.roll(p, 1, axis=1)
    row = lax.broadcasted_iota(jnp.int32, p.shape, 1)
    prev = jnp.where(row == 0, first, rolled)
    xs = (p + (prev - p) * mu).reshape(bb * tt, pw)
    r, k, v = xs[:, :w], xs[:, w:2 * w], xs[:, 2 * w:3 * w]
    lo = xs[:, 3 * w:]
    lane = lax.broadcasted_iota(jnp.int32, lo.shape, 1)
    nl = jnp.where(lane < DECAY_LORA, jnp.tanh(lo),
                   jnp.where(lane < DECAY_LORA + ICLR_LORA, lo, jax.nn.sigmoid(lo)))
    lora = _dot(nl.astype(BF16), w_lora)
    lw = -math.exp(-0.5) * jax.nn.sigmoid(w0 + lora[:, :w])
    a = jax.nn.sigmoid(a0 + lora[:, w:2 * w])
    g = lora[:, 2 * w:]
    kk = k * k_k
    kap = kk * lax.rsqrt(jnp.maximum(_head_sums(kk * kk), 1e-24))
    kt = k * (1.0 + (a - 1.0) * k_a)
    bonus = _head_sums(r * kt * r_k) * v
    return r, lw, kt, v, kap, kap * a, g, bonus


def _ffn1_kernel(x_ref, mod_ref, shift0_ref, w1_ref, w2_ref, win_ref, lng_ref, lnb_ref,
                 mu_ref, w0_ref, a0_ref, kk_ref, ka_ref, rk_ref, wl_ref,
                 x1_ref, q_ref, kv_ref, shift_ref, *rest, d_ff):
    prep_refs, carry_ref = rest[:-1], rest[-1]
    bb, tt, d = x_ref.shape

    @pl.when(pl.program_id(1) == 0)
    def _():
        carry_ref[...] = shift0_ref[...]

    x = x_ref[...]
    mod = lambda k: mod_ref[:, k:k + 1, :]
    u = (x * (1.0 + mod(1)) + mod(0)).reshape(bb * tt, d).astype(BF16)
    f1 = _swiglu(u, w1_ref, w2_ref, d_ff).reshape(bb, tt, d)
    x1 = _post_ln(ALPHA * x + 0.5 * (1.0 + mod(2)) * f1, lng_ref[0:1, :], lnb_ref[0:1, :])
    x1_ref[...] = x1
    u2 = (x1 * (1.0 + mod(4)) + mod(3)).reshape(bb * tt, d).astype(BF16)
    p = _dot(u2, win_ref[...])
    q_ref[...] = p[:, RWKV_PROJ:RWKV_PROJ + ATT_WIDTH].reshape(bb, tt, ATT_WIDTH)
    kv_ref[...] = p[:, RWKV_PROJ + ATT_WIDTH:].reshape(bb, tt, 2 * ATT_KV_WIDTH)
    p_rwkv = p[:, :RWKV_PROJ].reshape(bb, tt, RWKV_PROJ)
    first = carry_ref[...]
    last = p_rwkv[:, tt - 1:tt, :]
    carry_ref[...] = last
    shift_ref[...] = last
    outs = _rwkv_prep(p_rwkv, first, mu_ref[...], w0_ref[...], a0_ref[...], kk_ref[...], ka_ref[...],
                      rk_ref[...], wl_ref[...])
    for ref, val in zip(prep_refs, outs):
        ref[...] = val.reshape(bb, tt, RWKV_WIDTH)


def _ffn1(x, mod, shift0, w1, w2, win, ln_g, ln_b, mu, w0, a0, k_k, k_a, r_k, w_lora, bb, tt):
    b, t, d = x.shape
    d_ff = w2.shape[0]
    w = RWKV_WIDTH
    tok = lambda width: pl.BlockSpec((bb, tt, width), lambda i, j: (i, j, 0))
    per_seq = pl.BlockSpec((bb, 1, RWKV_PROJ), lambda i, j: (i, 0, 0))
    consts = [w1, w2, win, ln_g, ln_b, mu, w0, a0, k_k, k_a, r_k, w_lora]
    return pl.pallas_call(
        functools.partial(_ffn1_kernel, d_ff=d_ff),
        grid=(b // bb, t // tt),
        in_specs=[tok(d), pl.BlockSpec((bb, N_MOD, d), lambda i, j: (i, 0, 0)), per_seq]
                 + [_const_spec(c.shape) for c in consts],
        out_specs=[tok(d), tok(ATT_WIDTH), tok(2 * ATT_KV_WIDTH), per_seq] + [tok(w)] * 8,
        out_shape=[jax.ShapeDtypeStruct((b, t, d), F32),
                   jax.ShapeDtypeStruct((b, t, ATT_WIDTH), F32),
                   jax.ShapeDtypeStruct((b, t, 2 * ATT_KV_WIDTH), F32),
                   jax.ShapeDtypeStruct((b, 1, RWKV_PROJ), F32)]
                  + [jax.ShapeDtypeStruct((b, t, w), F32)] * 8,
        scratch_shapes=[pltpu.VMEM((bb, 1, RWKV_PROJ), F32)],
        compiler_params=pltpu.CompilerParams(
            dimension_semantics=("arbitrary", "arbitrary"), vmem_limit_bytes=VMEM_LIMIT_BYTES),
        name="ffn1_ln_inproj_prep",
    )(x, mod, shift0, *consts)


def _pdot(asp, bsp, dot):
    (ah, al), (bh, bl) = asp, bsp
    zero = jnp.zeros_like(bl)
    if dot is _dot_nt:
        lhs = jnp.concatenate([ah, al], axis=1)
        rhs = jnp.concatenate([jnp.concatenate([bh, bh], axis=1), jnp.concatenate([bl, zero], axis=1)], axis=0)
        n = bh.shape[0]
    else:
        lhs = jnp.concatenate([ah, al], axis=1 if dot is _dot else 0)
        rhs = jnp.concatenate([jnp.concatenate([bh, bl], axis=1), jnp.concatenate([bh, zero], axis=1)], axis=0)
        n = bh.shape[1]
    out = dot(lhs, rhs)
    return out[:, :n] + out[:, n:]


def _pair_masks(keep0, keep1, use_bf16):
    if use_bf16:
        return keep0, keep1, jnp.where(keep0, 1.0, 0.0).astype(BF16), jnp.where(keep1, 1.0, 0.0).astype(BF16)
    return keep0, keep1, None, None


def _expand(x, pre, masks):
    keep0, keep1, m0, m1 = masks
    if m0 is not None:
        hi, lo = pre
        return (jnp.concatenate([hi * m0, hi * m1], axis=0), jnp.concatenate([lo * m0, lo * m1], axis=0))
    return _split2(jnp.concatenate([jnp.where(keep0, x, 0.0), jnp.where(keep1, x, 0.0)], axis=0))


def _with_split(x):
    return x, _split2(x)


def _wkv_static(units, tri, half, c):
    strict, incl, first_level, levels, eye = tri
    rng = range(len(units))
    ex = lambda xf, pre: _expand(xf, pre, half(xf.shape))
    gb = [_pdot(u["kr"], ex(*u["bi"]), _dot_nt) for u in units]
    gk = [_pdot(u["kr"], ex(*u["ki"]), _dot_nt) for u in units]
    mab = [_with_split(jnp.where(strict, g[:c], 0.0)) for g in gb]
    arb = [_split2(jnp.where(incl, g[c:], 0.0)) for g in gb]
    mk = [_split2(jnp.concatenate([jnp.where(strict, g[:c], 0.0), jnp.where(incl, g[c:], 0.0)], axis=0))
          for g in gk]
    mv = [_pdot(mk[i], ex(*units[i]["v"]), _dot) for i in rng]
    x = [eye - jnp.where(first_level, m[0], 0.0) for m in mab]
    for lvl in levels:
        xs = [_with_split(xi) for xi in x]
        t = [_split2(_pdot(xs[i][1], _expand(*mab[i], lvl), _dot)) for i in rng]
        x = [x[i] - _pdot(t[i], ex(*xs[i]), _dot) for i in rng]
    return [dict(x=_split2(x[i]), mv=mv[i], arb=arb[i]) for i in rng]


def _wkv_dynamic(units, static, states, half, c):
    rng = range(len(units))
    ex = lambda xf, pre: _expand(xf, pre, half(xf.shape))
    st = [_pdot(units[i]["kr"], ex(*_with_split(states[i])), _dot_nt) for i in rng]
    sa = [_pdot(static[i]["x"], ex(*_with_split(st[i][:c] + static[i]["mv"][:c])), _dot) for i in rng]
    y = [st[i][c:] + static[i]["mv"][c:] - _pdot(static[i]["arb"], ex(*_with_split(sa[i])), _dot) for i in rng]
    vsa = [_split2(jnp.concatenate([units[i]["v"][0], -sa[i]], axis=0)) for i in rng]
    cross = [_pdot(vsa[i], units[i]["keb"], _dot_tn) for i in rng]
    n = HEAD_DIM
    low = half((n, 2 * n))[0]
    s_new = [states[i] * units[i]["decay"] + jnp.where(low, cross[i][:n], cross[i][n:]) for i in rng]
    return y, s_new


def _wkv_kernel(r_ref, lw_ref, kt_ref, v_ref, kap_ref, b_ref, s0_ref, y_ref, sfin_ref, s_scr, *, c, group):
    bb, rows_per_step, w = r_ref.shape
    nchunks = rows_per_step // c
    n = HEAD_DIM
    pair = 2 * n
    npairs = w // pair
    j = pl.program_id(1)
    use_bf16 = c % 16 == 0

    @pl.when(j == 0)
    def _():
        for b_ in range(bb):
            for p in range(npairs):
                s_scr[b_, p] = jnp.concatenate([s0_ref[b_, 2 * p], s0_ref[b_, 2 * p + 1]], axis=1)

    ti = lax.broadcasted_iota(jnp.int32, (c, 2 * c), 0)
    jl = lax.broadcasted_iota(jnp.int32, (c, 2 * c), 1)
    jm = jl & (c - 1)
    low = jl < c
    lvl_masks = []
    s = 1
    while s < c:
        same = ((ti ^ jm) & ~(2 * s - 1)) == 0
        lvl_masks.append(same & ((ti & s) != 0) & ((jm & s) == 0))
        s *= 2
    levels = [_pair_masks(m & low, m & ~low, use_bf16) for m in lvl_masks[1:]]
    tri = (ti > jm, ti >= jm, lvl_masks[0], levels, (ti == jm).astype(F32))
    half_cache = {}

    def half(shape):
        if shape not in half_cache:
            lo_ = lax.broadcasted_iota(jnp.int32, shape, 1) < shape[1] // 2
            half_cache[shape] = _pair_masks(lo_, ~lo_, use_bf16)
        return half_cache[shape]

    for shape in ((c, pair), (c, 2 * c), (n, pair)):
        half(shape)
    row = lax.broadcasted_iota(jnp.int32, (c, w), 0)

    def body(gi, carry):
        units, where = [], []
        for ck in range(nchunks):
            rows = slice(ck * c, (ck + 1) * c)
            for k in range(group):
                bi_ = gi * group + k
                lw = lw_ref[bi_, rows, :]
                cl = lw
                s_ = 1
                while s_ < c:
                    cl = cl + jnp.where(row >= s_, pltpu.roll(cl, s_, axis=0), 0.0)
                    s_ *= 2
                last = cl[c - 1:c, :]
                e_inv = jnp.exp(-cl)
                e_end = jnp.exp(last - cl)
                kt = kt_ref[bi_, rows, :]
                bv = b_ref[bi_, rows, :]
                wide = dict(
                    kr=jnp.concatenate([kap_ref[bi_, rows, :] * jnp.exp(cl - lw),
                                        r_ref[bi_, rows, :] * jnp.exp(cl)], axis=0),
                    bi=bv * e_inv, ki=kt * e_inv, v=v_ref[bi_, rows, :],
                    keb=jnp.concatenate([kt * e_end, bv * e_end], axis=0))
                split = {key: _split2(val) for key, val in wide.items()}
                decay = jnp.exp(last)
                for p in range(npairs):
                    sl = slice(p * pair, (p + 1) * pair)
                    cut = lambda key: (split[key][0][:, sl], split[key][1][:, sl])
                    units.append(dict(kr=cut("kr"), keb=cut("keb"),
                                      bi=(wide["bi"][:, sl], cut("bi")), ki=(wide["ki"][:, sl], cut("ki")),
                                      v=(wide["v"][:, sl], cut("v")), decay=decay[:, sl]))
                    where.append((bi_, p, rows, sl))
        static = _wkv_static(units, tri, half, c)
        per_chunk = group * npairs
        states = [s_scr[bi_, p] for (bi_, p, _, _) in where[:per_chunk]]
        for ck in range(nchunks):
            sel = slice(ck * per_chunk, (ck + 1) * per_chunk)
            ys, states = _wkv_dynamic(units[sel], static[sel], states, half, c)
            for (bi_, p, rows, sl), y in zip(where[sel], ys):
                y_ref[bi_, rows, sl] = y
        for (bi_, p, _, _), sn in zip(where[:per_chunk], states):
            s_scr[bi_, p] = sn
        return carry

    if bb == group:
        body(0, 0)
    else:
        lax.fori_loop(0, bb // group, body, 0)

    @pl.when(j == pl.num_programs(1) - 1)
    def _():
        for b_ in range(bb):
            for p in range(npairs):
                sp_ = s_scr[b_, p]
                sfin_ref[b_, 2 * p] = sp_[:, :n]
                sfin_ref[b_, 2 * p + 1] = sp_[:, n:]


def _wkv(r, lw, kt, v, kap, bvec, s0, bb, c, nchunks, group):
    b, t, w = r.shape
    h = w // HEAD_DIM
    rows = c * nchunks
    tok = pl.BlockSpec((bb, rows, w), lambda i, j: (i, j, 0))
    st = pl.BlockSpec((bb, h, HEAD_DIM, HEAD_DIM), lambda i, j: (i, 0, 0, 0))
    return pl.pallas_call(
        functools.partial(_wkv_kernel, c=c, group=group),
        grid=(b // bb, t // rows),
        in_specs=[tok] * 6 + [st],
        out_specs=[tok, st],
        out_shape=[jax.ShapeDtypeStruct((b, t, w), F32),
                   jax.ShapeDtypeStruct((b, h, HEAD_DIM, HEAD_DIM), F32)],
        scratch_shapes=[pltpu.VMEM((bb, h // 2, HEAD_DIM, 2 * HEAD_DIM), F32)],
        compiler_params=pltpu.CompilerParams(dimension_semantics=("arbitrary", "arbitrary")),
        name="wkv_scan",
    )(r, lw, kt, v, kap, bvec, s0)


def _rel_bucket_np(dist):
    max_exact = N_BUCKETS // 2
    n = np.maximum(dist, 0)
    n_f = np.maximum(n, max_exact).astype(np.float32)
    large = max_exact + (np.log(n_f / np.float32(max_exact)) / np.float32(math.log(MAX_DISTANCE / max_exact))
                         * np.float32(N_BUCKETS - max_exact)).astype(np.int32)
    return np.where(n < max_exact, n, np.minimum(large, N_BUCKETS - 1)).astype(np.int32)


def _bias_table(idx, mask, rb_ref, h):
    acc = jnp.zeros(idx.shape, F32)
    for bk in range(N_BUCKETS):
        acc = jnp.where(idx == bk, rb_ref[bk, h], acc)
    return jnp.where(mask, acc, NEG_INF)


def _softmax_sink_pv(s_heads, v_heads, sinks):
    ms = []
    for parts, sink in zip(s_heads, sinks):
        m = sink
        for s in parts:
            m = jnp.maximum(m, jnp.max(s, axis=-1, keepdims=True))
        ms.append(m)
    es = [[jnp.exp(s - m) for s in parts] for parts, m in zip(s_heads, ms)]
    dens = []
    for parts, sink, m in zip(es, sinks, ms):
        den = jnp.exp(sink - m)
        for e in parts:
            den = den + jnp.sum(e, axis=-1, keepdims=True)
        dens.append(den)
    outs = []
    for parts, vparts in zip(es, v_heads):
        o = None
        for e, (v, dot) in zip(parts, vparts):
            pv = dot(e.astype(BF16), v)
            o = pv if o is None else o + pv
        outs.append(o)
    return [o / den for o, den in zip(outs, dens)]


def _attn_prompt_kernel(q_ref, kvc_ref, kvp_ref, idx_ref, rb_ref, sink_ref, o_ref, tb_ref):
    n = HEAD_DIM
    wq = WINDOW
    wk = 2 * wq
    nblk = q_ref.shape[1] // wq
    first = (pl.program_id(0) == 0) & (pl.program_id(1) == 0)

    @pl.when(first)
    def _():
        qi = lax.broadcasted_iota(jnp.int32, (wq, wk), 0)
        kj = lax.broadcasted_iota(jnp.int32, (wq, wk), 1)
        dist = qi + wq - kj
        mask = (dist >= 0) & (dist < WINDOW)
        idx = idx_ref[...]
        for h in range(ATT_HEADS):
            tb_ref[h] = _bias_table(idx, mask, rb_ref, h)

    kj = lax.broadcasted_iota(jnp.int32, (wq, wk), 1)
    dead = (pl.program_id(1) == 0) & (kj < wq)
    q = q_ref[0]
    kv = jnp.concatenate([kvp_ref[0], kvc_ref[0]], axis=0).astype(BF16)
    s_heads, v_heads, sinks, where = [], [], [], []
    for blk in range(nblk):
        rows = slice(blk * wq, blk * wq + wk)
        for h2 in range(ATT_KV_HEADS):
            k_all = kv[rows, h2 * n:(h2 + 1) * n]
            v_all = kv[rows, ATT_KV_WIDTH + h2 * n:ATT_KV_WIDTH + (h2 + 1) * n]
            for g in range(ATT_GROUPS):
                h = h2 * ATT_GROUPS + g
                qh = q[blk * wq:(blk + 1) * wq, h * n:(h + 1) * n].astype(BF16)
                s = _dot_nt(qh, k_all) * (HEAD_DIM ** -0.5) + tb_ref[h]
                s_heads.append([jnp.where(dead, NEG_INF, s) if blk == 0 else s])
                v_heads.append([(v_all, _dot)])
                sinks.append(sink_ref[0, h])
                where.append((blk, h))
    outs = _softmax_sink_pv(s_heads, v_heads, sinks)
    for (blk, h), o in zip(where, outs):
        o_ref[0, blk * wq:(blk + 1) * wq, h * n:(h + 1) * n] = o


def _attn_prompt(q, kv, rel_bias, sinks):
    b, t, _ = q.shape
    wq = WINDOW
    dist = np.arange(wq)[:, None] + wq - np.arange(2 * wq)[None, :]
    idx = jnp.asarray(_rel_bucket_np(dist))
    smem = pl.BlockSpec(memory_space=pltpu.SMEM)
    nblk = ATT_QBLOCKS
    tq = nblk * wq
    return pl.pallas_call(
        _attn_prompt_kernel,
        grid=(b, t // tq),
        in_specs=[pl.BlockSpec((1, tq, ATT_WIDTH), lambda i, j: (i, j, 0)),
                  pl.BlockSpec((1, tq, 2 * ATT_KV_WIDTH), lambda i, j: (i, j, 0)),
                  pl.BlockSpec((1, wq, 2 * ATT_KV_WIDTH), lambda i, j: (i, jnp.maximum(j * nblk - 1, 0), 0)),
                  pl.BlockSpec(idx.shape, lambda i, j: (0, 0)),
                  smem, smem],
        out_specs=pl.BlockSpec((1, tq, ATT_WIDTH), lambda i, j: (i, j, 0)),
        out_shape=jax.ShapeDtypeStruct((b, t, ATT_WIDTH), F32),
        scratch_shapes=[pltpu.VMEM((ATT_HEADS, wq, 2 * wq), F32)],
        compiler_params=pltpu.CompilerParams(dimension_semantics=("arbitrary", "arbitrary")),
        name="attn_prompt",
    )(q, kv, kv, idx, rel_bias, sinks.reshape(1, ATT_HEADS))


def _attn_sample_kernel(q_ref, kvn_ref, kb_ref, vb_ref, idxb_ref, idxn_ref, rb_ref, sink_ref,
                        o_ref, kbo_ref, vbo_ref, tbb_ref, tbn_ref):
    n = HEAD_DIM
    bb, t, _ = q_ref.shape
    wb = kb_ref.shape[1]

    @pl.when(pl.program_id(0) == 0)
    def _():
        qi = lax.broadcasted_iota(jnp.int32, (t, wb), 0)
        kj = lax.broadcasted_iota(jnp.int32, (t, wb), 1)
        dist_b = qi + wb - kj
        mask_b = (dist_b >= 0) & (dist_b < WINDOW)
        qn = lax.broadcasted_iota(jnp.int32, (t, t), 0)
        kn = lax.broadcasted_iota(jnp.int32, (t, t), 1)
        dist_n = qn - kn
        mask_n = (dist_n >= 0) & (dist_n < WINDOW)
        for h in range(ATT_HEADS):
            tbb_ref[h] = _bias_table(idxb_ref[...], mask_b, rb_ref, h)
            tbn_ref[h] = _bias_table(idxn_ref[...], mask_n, rb_ref, h)

    q = q_ref[...]
    kvn = kvn_ref[...]
    kb = kb_ref[...]
    vb = vb_ref[...]
    kbo_ref[:, :wb - t, :] = kb[:, t:, :]
    kbo_ref[:, wb - t:, :] = kvn[:, :, :ATT_KV_WIDTH]
    vbo_ref[:, :wb - t, :] = vb[:, t:, :]
    vbo_ref[:, wb - t:, :] = kvn[:, :, ATT_KV_WIDTH:]
    bnt = lambda a, b_: jnp.einsum("bqd,bkd->bqk", a, b_, preferred_element_type=F32)
    bnn = lambda a, b_: jnp.einsum("bqk,bkd->bqd", a, b_, preferred_element_type=F32)
    s_heads, v_heads = [], []
    for h2 in range(ATT_KV_HEADS):
        ks = slice(h2 * n, (h2 + 1) * n)
        vs = slice(ATT_KV_WIDTH + h2 * n, ATT_KV_WIDTH + (h2 + 1) * n)
        k_buf = kb[:, :, ks].astype(BF16)
        v_buf = vb[:, :, ks].astype(BF16)
        k_new = kvn[:, :, ks].astype(BF16)
        v_new = kvn[:, :, vs].astype(BF16)
        for g in range(ATT_GROUPS):
            h = h2 * ATT_GROUPS + g
            qh = q[:, :, h * n:(h + 1) * n].astype(BF16)
            s_heads.append([bnt(qh, k_buf) * (HEAD_DIM ** -0.5) + tbb_ref[h][None],
                            bnt(qh, k_new) * (HEAD_DIM ** -0.5) + tbn_ref[h][None]])
            v_heads.append([(v_buf, bnn), (v_new, bnn)])
    outs = _softmax_sink_pv(s_heads, v_heads, [sink_ref[0, h] for h in range(ATT_HEADS)])
    for h, o in enumerate(outs):
        o_ref[:, :, h * n:(h + 1) * n] = o


def _attn_sample(q, kv, kbuf, vbuf, rel_bias, sinks, bb):
    b, t, _ = q.shape
    wb = kbuf.shape[1]
    dist_b = np.arange(t)[:, None] + wb - np.arange(wb)[None, :]
    dist_n = np.arange(t)[:, None] - np.arange(t)[None, :]
    idx_b = jnp.asarray(_rel_bucket_np(dist_b))
    idx_n = jnp.asarray(_rel_bucket_np(dist_n))
    smem = pl.BlockSpec(memory_space=pltpu.SMEM)
    tok = lambda w: pl.BlockSpec((bb, t, w), lambda i: (i, 0, 0))
    buf = pl.BlockSpec((bb, wb, ATT_KV_WIDTH), lambda i: (i, 0, 0))
    return pl.pallas_call(
        _attn_sample_kernel,
        grid=(b // bb,),
        in_specs=[tok(ATT_WIDTH), tok(2 * ATT_KV_WIDTH), buf, buf,
                  pl.BlockSpec(idx_b.shape, lambda i: (0, 0)),
                  pl.BlockSpec(idx_n.shape, lambda i: (0, 0)),
                  smem, smem],
        out_specs=[tok(ATT_WIDTH), buf, buf],
        out_shape=[jax.ShapeDtypeStruct((b, t, ATT_WIDTH), F32),
                   jax.ShapeDtypeStruct(kbuf.shape, F32),
                   jax.ShapeDtypeStruct(vbuf.shape, F32)],
        scratch_shapes=[pltpu.VMEM((ATT_HEADS, t, wb), F32), pltpu.VMEM((ATT_HEADS, t, t), F32)],
        compiler_params=pltpu.CompilerParams(dimension_semantics=("arbitrary",)),
        name="attn_sample",
    )(q, kv, kbuf, vbuf, idx_b, idx_n, rel_bias, sinks.reshape(1, ATT_HEADS))


def _out_kernel(x1_ref, mod_ref, y_ref, g_ref, bonus_ref, att_ref, gnw_ref, gnb_ref,
                wo_ref, w1_ref, w2_ref, lng_ref, lnb_ref, o_ref, *, d_ff):
    bb, tt, d = x1_ref.shape
    w = RWKV_WIDTH
    m = bb * tt
    x1 = x1_ref[...]
    mod = lambda k: mod_ref[:, k:k + 1, :]
    y = y_ref[...].reshape(m, w)
    mu = _head_sums(y) * (1.0 / HEAD_DIM)
    dy = y - mu
    var = _head_sums(dy * dy) * (1.0 / HEAD_DIM)
    yn = dy * lax.rsqrt(var + GN_EPS) * gnw_ref[...] + gnb_ref[...]
    y_rwkv = (yn + bonus_ref[...].reshape(m, w)) * g_ref[...].reshape(m, w)
    mix = (_dot(y_rwkv.astype(BF16), wo_ref[:w, :])
           + _dot(att_ref[...].reshape(m, ATT_WIDTH).astype(BF16), wo_ref[w:, :])).reshape(bb, tt, d)
    x2 = _post_ln(ALPHA * x1 + (1.0 + mod(5)) * mix, lng_ref[1:2, :], lnb_ref[1:2, :])
    u = (x2 * (1.0 + mod(7)) + mod(6)).reshape(m, d).astype(BF16)
    f2 = _swiglu(u, w1_ref, w2_ref, d_ff).reshape(bb, tt, d)
    o_ref[...] = _post_ln(ALPHA * x2 + 0.5 * (1.0 + mod(8)) * f2, lng_ref[2:3, :], lnb_ref[2:3, :])


def _out(x1, mod, y, g, bonus, att, gn_w, gn_b, wo, w1, w2, ln_g, ln_b, bb, tt):
    b, t, d = x1.shape
    d_ff = w2.shape[0]
    tok = lambda w: pl.BlockSpec((bb, tt, w), lambda i, j: (i, j, 0))
    consts = [gn_w, gn_b, wo, w1, w2, ln_g, ln_b]
    return pl.pallas_call(
        functools.partial(_out_kernel, d_ff=d_ff),
        grid=(b // bb, t // tt),
        in_specs=[tok(d), pl.BlockSpec((bb, N_MOD, d), lambda i, j: (i, 0, 0)),
                  tok(RWKV_WIDTH), tok(RWKV_WIDTH), tok(RWKV_WIDTH), tok(ATT_WIDTH)]
                 + [_const_spec(c.shape) for c in consts],
        out_specs=tok(d),
        out_shape=jax.ShapeDtypeStruct((b, t, d), F32),
        compiler_params=pltpu.CompilerParams(
            dimension_semantics=("arbitrary", "arbitrary"), vmem_limit_bytes=VMEM_LIMIT_BYTES),
        name="mix_ln_ffn2",
    )(x1, mod, y, g, bonus, att, *consts)


def _layer(x, mod, wkv0, shift0, kbuf, vbuf, wts, bb, tt, wkv_bb, wkv_c, wkv_chunks, wkv_group):
    b, t, d = x.shape
    x1, q, kv, shift_new, r, lw, kt, v, kap, bvec, g, bonus = _ffn1(
        x, mod, shift0[:, None, :], wts["w1a"], wts["w2a"], wts["win"], wts["ln_g"], wts["ln_b"],
        wts["mu"], wts["w0"], wts["a0"], wts["k_k"], wts["k_a"], wts["r_k"], wts["w_lora"], bb, tt)
    y, s_fin = _wkv(r, lw, kt, v, kap, bvec, wkv0, wkv_bb, wkv_c, wkv_chunks, wkv_group)
    shift_new = shift_new[:, 0, :]
    if kbuf is None:
        att = _attn_prompt(q, kv, wts["rel_bias"], wts["sinks"])
        wb = WINDOW
        kb_new = kv[:, t - wb:, :ATT_KV_WIDTH]
        vb_new = kv[:, t - wb:, ATT_KV_WIDTH:]
    else:
        att, kb_new, vb_new = _attn_sample(q, kv, kbuf, vbuf, wts["rel_bias"], wts["sinks"], bb)
    out = _out(x1, mod, y, g, bonus, att, wts["gn_w"], wts["gn_b"], wts["wo"],
               wts["w1b"], wts["w2b"], wts["ln_g"], wts["ln_b"],
               bb * OUT_TILE_FACTOR if tt < ROW_TILE else bb, tt * OUT_TILE_FACTOR if tt >= ROW_TILE else tt)
    return out, s_fin, shift_new, kb_new, vb_new


def kernel(x_prompt, x_sample, state_wkv, state_shift, cache_win_k, cache_win_v, c_prompt, c_sample, rel_bias, w_ada, b_ada, ln_g, ln_b, w_ffn1_in, w_ffn1_out, w_in, mu_shift, w0, w_decay, a0, w_iclr, w_gate, k_k, k_a, r_k, gn_w, gn_b, sinks, w_out, w_ffn2_in, w_ffn2_out):
    bp, tp, d = x_prompt.shape
    bs, ts, _ = x_sample.shape
    depth = w_ada.shape[0]
    assert depth == 1
    l = 0
    w = RWKV_WIDTH
    wb = cache_win_k.shape[2]

    mod_p, mod_s = _ada(c_prompt, c_sample, w_ada[l], b_ada[l])
    mod_p = mod_p.reshape(bp, N_MOD, d)
    mod_s = mod_s.reshape(bs, N_MOD, d)

    w_lora = jnp.zeros((LORA_WIDTH, 3 * w), F32)
    w_lora = w_lora.at[:DECAY_LORA, :w].set(w_decay[l])
    w_lora = w_lora.at[DECAY_LORA:DECAY_LORA + ICLR_LORA, w:2 * w].set(w_iclr[l])
    w_lora = w_lora.at[DECAY_LORA + ICLR_LORA:, 2 * w:].set(w_gate[l])
    row = lambda z: z.reshape(1, -1)
    wts = dict(
        w1a=w_ffn1_in[l].astype(BF16), w2a=w_ffn1_out[l].astype(BF16), win=w_in[l].astype(BF16),
        w1b=w_ffn2_in[l].astype(BF16), w2b=w_ffn2_out[l].astype(BF16), wo=w_out[l].astype(BF16),
        ln_g=ln_g[l], ln_b=ln_b[l], mu=row(mu_shift[l]), w0=row(w0[l]), a0=row(a0[l]),
        k_k=row(k_k[l]), k_a=row(k_a[l]), r_k=row(r_k[l]), w_lora=w_lora.astype(BF16),
        gn_w=row(gn_w[l]), gn_b=row(gn_b[l]), rel_bias=rel_bias, sinks=sinks[l])

    wkv0_p = jnp.zeros((bp, RWKV_HEADS, HEAD_DIM, HEAD_DIM), F32)
    shift0_p = jnp.zeros((bp, RWKV_PROJ), F32)
    y_p, wkv_p, sh_p, k_p, v_p = _layer(x_prompt, mod_p, wkv0_p, shift0_p, None, None, wts,
                                        1, ROW_TILE, bp, WKV_CHUNK, WKV_CHUNKS_PER_STEP, bp)
    kbuf = cache_win_k[l].reshape(bs, wb, ATT_KV_WIDTH)
    vbuf = cache_win_v[l].reshape(bs, wb, ATT_KV_WIDTH)
    y_s, wkv_s, sh_s, k_s, v_s = _layer(x_sample, mod_s, state_wkv[l], state_shift[l], kbuf, vbuf, wts,
                                        ROW_TILE // ts, ts, 8, ts, 1, 8)
    kvshape = lambda z: z.reshape(1, z.shape[0], wb, ATT_KV_HEADS, HEAD_DIM)
    return (y_p, y_s, wkv_p[None], sh_p[None], kvshape(k_p), kvshape(v_p),
            wkv_s[None], sh_s[None], kvshape(k_s), kvshape(v_s))
```

```python
import functools
import math

import numpy as np
import jax
import jax.numpy as jnp
from jax import lax
from jax.experimental import pallas as pl
from jax.experimental.pallas import tpu as pltpu

F32 = jnp.float32
BF16 = jnp.bfloat16

HEAD_DIM = 64
RWKV_HEADS = 8
RWKV_WIDTH = RWKV_HEADS * HEAD_DIM
ATT_HEADS = 8
ATT_KV_HEADS = 2
ATT_GROUPS = ATT_HEADS // ATT_KV_HEADS
ATT_WIDTH = ATT_HEADS * HEAD_DIM
ATT_KV_WIDTH = ATT_KV_HEADS * HEAD_DIM
DECAY_LORA = 64
ICLR_LORA = 64
GATE_LORA = 128
LORA_WIDTH = DECAY_LORA + ICLR_LORA + GATE_LORA
RWKV_PROJ = 3 * RWKV_WIDTH + LORA_WIDTH
WINDOW = 128
N_BUCKETS = 32
MAX_DISTANCE = 128
N_MOD = 9
DEPTH = 1
ALPHA = (2 * DEPTH) ** 0.25
LN_EPS = 1e-5
GN_EPS = 64e-5
NEG_INF = -1e30

VMEM_LIMIT_BYTES = 56 * 1024 * 1024
ROW_TILE = 256
OUT_TILE_FACTOR = 2
WKV_CHUNK = 64
WKV_CHUNKS_PER_STEP = 4
ATT_QBLOCKS = 8


def _dot(a, b):
    return jnp.dot(a, b, preferred_element_type=F32)


def _dot_nt(a, b):
    return lax.dot_general(a, b, (((1,), (1,)), ((), ())), preferred_element_type=F32)


def _dot_tn(a, b):
    return lax.dot_general(a, b, (((0,), (0,)), ((), ())), preferred_element_type=F32)


def _split2(a):
    hi = a.astype(BF16)
    lo = (a - hi.astype(F32)).astype(BF16)
    return hi, lo


def _head_sums(x):
    pair = 2 * HEAD_DIM
    low = lax.broadcasted_iota(jnp.int32, (x.shape[0], pair), 1) < HEAD_DIM
    outs = []
    for p in range(x.shape[1] // pair):
        xs = x[:, p * pair:(p + 1) * pair]
        s_lo = jnp.sum(jnp.where(low, xs, 0.0), axis=-1, keepdims=True)
        s_hi = jnp.sum(jnp.where(low, 0.0, xs), axis=-1, keepdims=True)
        outs.append(jnp.where(low, s_lo, s_hi))
    return jnp.concatenate(outs, axis=-1)


def _silu(x):
    return x * jax.nn.sigmoid(x)


def _swiglu(u, w1_ref, w2_ref, d_ff):
    gate = _dot(u, w1_ref[:, :d_ff])
    up = _dot(u, w1_ref[:, d_ff:])
    return _dot((_silu(gate) * up).astype(BF16), w2_ref[...])


def _post_ln(h, g, b):
    mu = jnp.mean(h, axis=-1, keepdims=True)
    d = h - mu
    var = jnp.mean(d * d, axis=-1, keepdims=True)
    return d * lax.rsqrt(var + LN_EPS) * g + b


def _const_spec(shape):
    nd = len(shape)
    return pl.BlockSpec(shape, lambda *_: (0,) * nd, pipeline_mode=pl.Buffered(1))


def _ada_kernel(cp_ref, cs_ref, w_ref, b_ref, op_ref, os_ref):
    w = w_ref[...].astype(BF16)
    for c_ref, o_ref in ((cp_ref, op_ref), (cs_ref, os_ref)):
        o_ref[...] = _dot(_silu(c_ref[...]).astype(BF16), w) + b_ref[...]


def _ada(c_prompt, c_sample, w_ada, b_ada):
    d, n = w_ada.shape
    tn = d
    rows = lambda c: pl.BlockSpec(c.shape, lambda j: (0, 0))
    cols = lambda c: pl.BlockSpec((c.shape[0], tn), lambda j: (0, j))
    return pl.pallas_call(
        _ada_kernel,
        grid=(n // tn,),
        in_specs=[rows(c_prompt), rows(c_sample),
                  pl.BlockSpec((d, tn), lambda j: (0, j)),
                  pl.BlockSpec((1, tn), lambda j: (0, j))],
        out_specs=[cols(c_prompt), cols(c_sample)],
        out_shape=[jax.ShapeDtypeStruct((c.shape[0], n), F32) for c in (c_prompt, c_sample)],
        name="ada_mod",
    )(c_prompt, c_sample, w_ada, b_ada.reshape(1, n))


def _rwkv_prep(p, first, mu, w0, a0, k_k, k_a, r_k, w_lora):
    bb, tt, pw = p.shape
    w = RWKV_WIDTH
    rolled = pltpu.roll(p, 1, axis=1)
    row = lax.broadcasted_iota(jnp.int32, p.shape, 1)
    prev = jnp.where(row == 0, first, rolled)
    xs = (p + (prev - p) * mu).reshape(bb * tt, pw)
    r, k, v = xs[:, :w], xs[:, w:2 * w], xs[:, 2 * w:3 * w]
    lo = xs[:, 3 * w:]
    lane = lax.broadcasted_iota(jnp.int32, lo.shape, 1)
    nl = jnp.where(lane < DECAY_LORA, jnp.tanh(lo),
                   jnp.where(lane < DECAY_LORA + ICLR_LORA, lo, jax.nn.sigmoid(lo)))
    lora = _dot(nl.astype(BF16), w_lora)
    lw = -math.exp(-0.5) * jax.nn.sigmoid(w0 + lora[:, :w])
    a = jax.nn.sigmoid(a0 + lora[:, w:2 * w])
    g = lora[:, 2 * w:]
    kk = k * k_k
    kap = kk * lax.rsqrt(jnp.maximum(_head_sums(kk * kk), 1e-24))
    kt = k * (1.0 + (a - 1.0) * k_a)
    bonus = _head_sums(r * kt * r_k) * v
    return r, lw, kt, v, kap, kap * a, g, bonus


def _ffn1_kernel(x_ref, mod_ref, shift0_ref, w1_ref, w2_ref, win_ref, lng_ref, lnb_ref,
                 mu_ref, w0_ref, a0_ref, kk_ref, ka_ref, rk_ref, wl_ref,
                 x1_ref, q_ref, kv_ref, shift_ref, *rest, d_ff):
    prep_refs, carry_ref = rest[:-1], rest[-1]
    bb, tt, d = x_ref.shape

    @pl.when(pl.program_id(1) == 0)
    def _():
        carry_ref[...] = shift0_ref[...]

    x = x_ref[...]
    mod = lambda k: mod_ref[:, k:k + 1, :]
    u = (x * (1.0 + mod(1)) + mod(0)).reshape(bb * tt, d).astype(BF16)
    f1 = _swiglu(u, w1_ref, w2_ref, d_ff).reshape(bb, tt, d)
    x1 = _post_ln(ALPHA * x + 0.5 * (1.0 + mod(2)) * f1, lng_ref[0:1, :], lnb_ref[0:1, :])
    x1_ref[...] = x1
    u2 = (x1 * (1.0 + mod(4)) + mod(3)).reshape(bb * tt, d).astype(BF16)
    p = _dot(u2, win_ref[...])
    q_ref[...] = p[:, RWKV_PROJ:RWKV_PROJ + ATT_WIDTH].reshape(bb, tt, ATT_WIDTH)
    kv_ref[...] = p[:, RWKV_PROJ + ATT_WIDTH:].reshape(bb, tt, 2 * ATT_KV_WIDTH)
    p_rwkv = p[:, :RWKV_PROJ].reshape(bb, tt, RWKV_PROJ)
    first = carry_ref[...]
    last = p_rwkv[:, tt - 1:tt, :]
    carry_ref[...] = last
    shift_ref[...] = last
    outs = _rwkv_prep(p_rwkv, first, mu_ref[...], w0_ref[...], a0_ref[...], kk_ref[...], ka_ref[...],
                      rk_ref[...], wl_ref[...])
    for ref, val in zip(prep_refs, outs):
        ref[...] = val.reshape(bb, tt, RWKV_WIDTH)


def _ffn1(x, mod, shift0, w1, w2, win, ln_g, ln_b, mu, w0, a0, k_k, k_a, r_k, w_lora, bb, tt):
    b, t, d = x.shape
    d_ff = w2.shape[0]
    w = RWKV_WIDTH
    tok = lambda width: pl.BlockSpec((bb, tt, width), lambda i, j: (i, j, 0))
    per_seq = pl.BlockSpec((bb, 1, RWKV_PROJ), lambda i, j: (i, 0, 0))
    consts = [w1, w2, win, ln_g, ln_b, mu, w0, a0, k_k, k_a, r_k, w_lora]
    return pl.pallas_call(
        functools.partial(_ffn1_kernel, d_ff=d_ff),
        grid=(b // bb, t // tt),
        in_specs=[tok(d), pl.BlockSpec((bb, N_MOD, d), lambda i, j: (i, 0, 0)), per_seq]
                 + [_const_spec(c.shape) for c in consts],
        out_specs=[tok(d), tok(ATT_WIDTH), tok(2 * ATT_KV_WIDTH), per_seq] + [tok(w)] * 8,
        out_shape=[jax.ShapeDtypeStruct((b, t, d), F32),
                   jax.ShapeDtypeStruct((b, t, ATT_WIDTH), F32),
                   jax.ShapeDtypeStruct((b, t, 2 * ATT_KV_WIDTH), F32),
                   jax.ShapeDtypeStruct((b, 1, RWKV_PROJ), F32)]
                  + [jax.ShapeDtypeStruct((b, t, w), F32)] * 8,
        scratch_shapes=[pltpu.VMEM((bb, 1, RWKV_PROJ), F32)],
        compiler_params=pltpu.CompilerParams(
            dimension_semantics=("arbitrary", "arbitrary"), vmem_limit_bytes=VMEM_LIMIT_BYTES),
        name="ffn1_ln_inproj_prep",
    )(x, mod, shift0, *consts)


def _pdot(asp, bsp, dot):
    (ah, al), (bh, bl) = asp, bsp
    zero = jnp.zeros_like(bl)
    if dot is _dot_nt:
        lhs = jnp.concatenate([ah, al], axis=1)
        rhs = jnp.concatenate([jnp.concatenate([bh, bh], axis=1), jnp.concatenate([bl, zero], axis=1)], axis=0)
        n = bh.shape[0]
    else:
        lhs = jnp.concatenate([ah, al], axis=1 if dot is _dot else 0)
        rhs = jnp.concatenate([jnp.concatenate([bh, bl], axis=1), jnp.concatenate([bh, zero], axis=1)], axis=0)
        n = bh.shape[1]
    out = dot(lhs, rhs)
    return out[:, :n] + out[:, n:]


def _pair_masks(keep0, keep1, use_bf16):
    if use_bf16:
        return keep0, keep1, jnp.where(keep0, 1.0, 0.0).astype(BF16), jnp.where(keep1, 1.0, 0.0).astype(BF16)
    return keep0, keep1, None, None


def _expand(x, pre, masks):
    keep0, keep1, m0, m1 = masks
    if m0 is not None:
        hi, lo = pre
        return (jnp.concatenate([hi * m0, hi * m1], axis=0), jnp.concatenate([lo * m0, lo * m1], axis=0))
    return _split2(jnp.concatenate([jnp.where(keep0, x, 0.0), jnp.where(keep1, x, 0.0)], axis=0))


def _with_split(x):
    return x, _split2(x)


def _wkv_static(units, tri, half, c):
    strict, incl, first_level, levels, eye = tri
    rng = range(len(units))
    ex = lambda xf, pre: _expand(xf, pre, half(xf.shape))
    gb = [_pdot(u["kr"], ex(*u["bi"]), _dot_nt) for u in units]
    gk = [_pdot(u["kr"], ex(*u["ki"]), _dot_nt) for u in units]
    mab = [_with_split(jnp.where(strict, g[:c], 0.0)) for g in gb]
    arb = [_split2(jnp.where(incl, g[c:], 0.0)) for g in gb]
    mk = [_split2(jnp.concatenate([jnp.where(strict, g[:c], 0.0), jnp.where(incl, g[c:], 0.0)], axis=0))
          for g in gk]
    mv = [_pdot(mk[i], ex(*units[i]["v"]), _dot) for i in rng]
    x = [eye - jnp.where(first_level, m[0], 0.0) for m in mab]
    for lvl in levels:
        xs = [_with_split(xi) for xi in x]
        t = [_split2(_pdot(xs[i][1], _expand(*mab[i], lvl), _dot)) for i in rng]
        x = [x[i] - _pdot(t[i], ex(*xs[i]), _dot) for i in rng]
    return [dict(x=_split2(x[i]), mv=mv[i], arb=arb[i]) for i in rng]


def _wkv_dynamic(units, static, states, half, c):
    rng = range(len(units))
    ex = lambda xf, pre: _expand(xf, pre, half(xf.shape))
    st = [_pdot(units[i]["kr"], ex(*_with_split(states[i])), _dot_nt) for i in rng]
    sa = [_pdot(static[i]["x"], ex(*_with_split(st[i][:c] + static[i]["mv"][:c])), _dot) for i in rng]
    y = [st[i][c:] + static[i]["mv"][c:] - _pdot(static[i]["arb"], ex(*_with_split(sa[i])), _dot) for i in rng]
    vsa = [_split2(jnp.concatenate([units[i]["v"][0], -sa[i]], axis=0)) for i in rng]
    cross = [_pdot(vsa[i], units[i]["keb"], _dot_tn) for i in rng]
    n = HEAD_DIM
    low = half((n, 2 * n))[0]
    s_new = [states[i] * units[i]["decay"] + jnp.where(low, cross[i][:n], cross[i][n:]) for i in rng]
    return y, s_new


def _wkv_kernel(r_ref, lw_ref, kt_ref, v_ref, kap_ref, b_ref, s0_ref, y_ref, sfin_ref, s_scr, *, c, group):
    bb, rows_per_step, w = r_ref.shape
    nchunks = rows_per_step // c
    n = HEAD_DIM
    pair = 2 * n
    npairs = w // pair
    j = pl.program_id(1)
    use_bf16 = c % 16 == 0

    @pl.when(j == 0)
    def _():
        for b_ in range(bb):
            for p in range(npairs):
                s_scr[b_, p] = jnp.concatenate([s0_ref[b_, 2 * p], s0_ref[b_, 2 * p + 1]], axis=1)

    ti = lax.broadcasted_iota(jnp.int32, (c, 2 * c), 0)
    jl = lax.broadcasted_iota(jnp.int32, (c, 2 * c), 1)
    jm = jl & (c - 1)
    low = jl < c
    lvl_masks = []
    s = 1
    while s < c:
        same = ((ti ^ jm) & ~(2 * s - 1)) == 0
        lvl_masks.append(same & ((ti & s) != 0) & ((jm & s) == 0))
        s *= 2
    levels = [_pair_masks(m & low, m & ~low, use_bf16) for m in lvl_masks[1:]]
    tri = (ti > jm, ti >= jm, lvl_masks[0], levels, (ti == jm).astype(F32))
    half_cache = {}

    def half(shape):
        if shape not in half_cache:
            lo_ = lax.broadcasted_iota(jnp.int32, shape, 1) < shape[1] // 2
            half_cache[shape] = _pair_masks(lo_, ~lo_, use_bf16)
        return half_cache[shape]

    for shape in ((c, pair), (c, 2 * c), (n, pair)):
        half(shape)
    row = lax.broadcasted_iota(jnp.int32, (c, w), 0)

    def body(gi, carry):
        units, where = [], []
        for ck in range(nchunks):
            rows = slice(ck * c, (ck + 1) * c)
            for k in range(group):
                bi_ = gi * group + k
                lw = lw_ref[bi_, rows, :]
                cl = lw
                s_ = 1
                while s_ < c:
                    cl = cl + jnp.where(row >= s_, pltpu.roll(cl, s_, axis=0), 0.0)
                    s_ *= 2
                last = cl[c - 1:c, :]
                e_inv = jnp.exp(-cl)
                e_end = jnp.exp(last - cl)
                kt = kt_ref[bi_, rows, :]
                bv = b_ref[bi_, rows, :]
                wide = dict(
                    kr=jnp.concatenate([kap_ref[bi_, rows, :] * jnp.exp(cl - lw),
                                        r_ref[bi_, rows, :] * jnp.exp(cl)], axis=0),
                    bi=bv * e_inv, ki=kt * e_inv, v=v_ref[bi_, rows, :],
                    keb=jnp.concatenate([kt * e_end, bv * e_end], axis=0))
                split = {key: _split2(val) for key, val in wide.items()}
                decay = jnp.exp(last)
                for p in range(npairs):
                    sl = slice(p * pair, (p + 1) * pair)
                    cut = lambda key: (split[key][0][:, sl], split[key][1][:, sl])
                    units.append(dict(kr=cut("kr"), keb=cut("keb"),
                                      bi=(wide["bi"][:, sl], cut("bi")), ki=(wide["ki"][:, sl], cut("ki")),
                                      v=(wide["v"][:, sl], cut("v")), decay=decay[:, sl]))
                    where.append((bi_, p, rows, sl))
        static = _wkv_static(units, tri, half, c)
        per_chunk = group * npairs
        states = [s_scr[bi_, p] for (bi_, p, _, _) in where[:per_chunk]]
        for ck in range(nchunks):
            sel = slice(ck * per_chunk, (ck + 1) * per_chunk)
            ys, states = _wkv_dynamic(units[sel], static[sel], states, half, c)
            for (bi_, p, rows, sl), y in zip(where[sel], ys):
                y_ref[bi_, rows, sl] = y
        for (bi_, p, _, _), sn in zip(where[:per_chunk], states):
            s_scr[bi_, p] = sn
        return carry

    if bb == group:
        body(0, 0)
    else:
        lax.fori_loop(0, bb // group, body, 0)

    @pl.when(j == pl.num_programs(1) - 1)
    def _():
        for b_ in range(bb):
            for p in range(npairs):
                sp_ = s_scr[b_, p]
                sfin_ref[b_, 2 * p] = sp_[:, :n]
                sfin_ref[b_, 2 * p + 1] = sp_[:, n:]


def _wkv(r, lw, kt, v, kap, bvec, s0, bb, c, nchunks, group):
    b, t, w = r.shape
    h = w // HEAD_DIM
    rows = c * nchunks
    tok = pl.BlockSpec((bb, rows, w), lambda i, j: (i, j, 0))
    st = pl.BlockSpec((bb, h, HEAD_DIM, HEAD_DIM), lambda i, j: (i, 0, 0, 0))
    return pl.pallas_call(
        functools.partial(_wkv_kernel, c=c, group=group),
        grid=(b // bb, t // rows),
        in_specs=[tok] * 6 + [st],
        out_specs=[tok, st],
        out_shape=[jax.ShapeDtypeStruct((b, t, w), F32),
                   jax.ShapeDtypeStruct((b, h, HEAD_DIM, HEAD_DIM), F32)],
        scratch_shapes=[pltpu.VMEM((bb, h // 2, HEAD_DIM, 2 * HEAD_DIM), F32)],
        compiler_params=pltpu.CompilerParams(dimension_semantics=("arbitrary", "arbitrary")),
        name="wkv_scan",
    )(r, lw, kt, v, kap, bvec, s0)


def _rel_bucket_np(dist):
    max_exact = N_BUCKETS // 2
    n = np.maximum(dist, 0)
    n_f = np.maximum(n, max_exact).astype(np.float32)
    large = max_exact + (np.log(n_f / np.float32(max_exact)) / np.float32(math.log(MAX_DISTANCE / max_exact))
                         * np.float32(N_BUCKETS - max_exact)).astype(np.int32)
    return np.where(n < max_exact, n, np.minimum(large, N_BUCKETS - 1)).astype(np.int32)


def _bias_table(idx, mask, rb_ref, h):
    acc = jnp.zeros(idx.shape, F32)
    for bk in range(N_BUCKETS):
        acc = jnp.where(idx == bk, rb_ref[bk, h], acc)
    return jnp.where(mask, acc, NEG_INF)


def _softmax_sink_pv(s_heads, v_heads, sinks):
    ms = []
    for parts, sink in zip(s_heads, sinks):
        m = sink
        for s in parts:
            m = jnp.maximum(m, jnp.max(s, axis=-1, keepdims=True))
        ms.append(m)
    es = [[jnp.exp(s - m) for s in parts] for parts, m in zip(s_heads, ms)]
    dens = []
    for parts, sink, m in zip(es, sinks, ms):
        den = jnp.exp(sink - m)
        for e in parts:
            den = den + jnp.sum(e, axis=-1, keepdims=True)
        dens.append(den)
    outs = []
    for parts, vparts in zip(es, v_heads):
        o = None
        for e, (v, dot) in zip(parts, vparts):
            pv = dot(e.astype(BF16), v)
            o = pv if o is None else o + pv
        outs.append(o)
    return [o / den for o, den in zip(outs, dens)]


def _attn_prompt_kernel(q_ref, kvc_ref, kvp_ref, idx_ref, rb_ref, sink_ref, o_ref, tb_ref):
    n = HEAD_DIM
    wq = WINDOW
    wk = 2 * wq
    nblk = q_ref.shape[1] // wq
    first = (pl.program_id(0) == 0) & (pl.program_id(1) == 0)

    @pl.when(first)
    def _():
        qi = lax.broadcasted_iota(jnp.int32, (wq, wk), 0)
        kj = lax.broadcasted_iota(jnp.int32, (wq, wk), 1)
        dist = qi + wq - kj
        mask = (dist >= 0) & (dist < WINDOW)
        idx = idx_ref[...]
        for h in range(ATT_HEADS):
            tb_ref[h] = _bias_table(idx, mask, rb_ref, h)

    kj = lax.broadcasted_iota(jnp.int32, (wq, wk), 1)
    dead = (pl.program_id(1) == 0) & (kj < wq)
    q = q_ref[0]
    kv = jnp.concatenate([kvp_ref[0], kvc_ref[0]], axis=0).astype(BF16)
    s_heads, v_heads, sinks, where = [], [], [], []
    for blk in range(nblk):
        rows = slice(blk * wq, blk * wq + wk)
        for h2 in range(ATT_KV_HEADS):
            k_all = kv[rows, h2 * n:(h2 + 1) * n]
            v_all = kv[rows, ATT_KV_WIDTH + h2 * n:ATT_KV_WIDTH + (h2 + 1) * n]
            for g in range(ATT_GROUPS):
                h = h2 * ATT_GROUPS + g
                qh = q[blk * wq:(blk + 1) * wq, h * n:(h + 1) * n].astype(BF16)
                s = _dot_nt(qh, k_all) * (HEAD_DIM ** -0.5) + tb_ref[h]
                s_heads.append([jnp.where(dead, NEG_INF, s) if blk == 0 else s])
                v_heads.append([(v_all, _dot)])
                sinks.append(sink_ref[0, h])
                where.append((blk, h))
    outs = _softmax_sink_pv(s_heads, v_heads, sinks)
    for (blk, h), o in zip(where, outs):
        o_ref[0, blk * wq:(blk + 1) * wq, h * n:(h + 1) * n] = o


def _attn_prompt(q, kv, rel_bias, sinks):
    b, t, _ = q.shape
    wq = WINDOW
    dist = np.arange(wq)[:, None] + wq - np.arange(2 * wq)[None, :]
    idx = jnp.asarray(_rel_bucket_np(dist))
    smem = pl.BlockSpec(memory_space=pltpu.SMEM)
    nblk = ATT_QBLOCKS
    tq = nblk * wq
    return pl.pallas_call(
        _attn_prompt_kernel,
        grid=(b, t // tq),
        in_specs=[pl.BlockSpec((1, tq, ATT_WIDTH), lambda i, j: (i, j, 0)),
                  pl.BlockSpec((1, tq, 2 * ATT_KV_WIDTH), lambda i, j: (i, j, 0)),
                  pl.BlockSpec((1, wq, 2 * ATT_KV_WIDTH), lambda i, j: (i, jnp.maximum(j * nblk - 1, 0), 0)),
                  pl.BlockSpec(idx.shape, lambda i, j: (0, 0)),
                  smem, smem],
        out_specs=pl.BlockSpec((1, tq, ATT_WIDTH), lambda i, j: (i, j, 0)),
        out_shape=jax.ShapeDtypeStruct((b, t, ATT_WIDTH), F32),
        scratch_shapes=[pltpu.VMEM((ATT_HEADS, wq, 2 * wq), F32)],
        compiler_params=pltpu.CompilerParams(dimension_semantics=("arbitrary", "arbitrary")),
        name="attn_prompt",
    )(q, kv, kv, idx, rel_bias, sinks.reshape(1, ATT_HEADS))


def _attn_sample_kernel(q_ref, kvn_ref, kb_ref, vb_ref, idxb_ref, idxn_ref, rb_ref, sink_ref,
                        o_ref, kbo_ref, vbo_ref, tbb_ref, tbn_ref):
    n = HEAD_DIM
    bb, t, _ = q_ref.shape
    wb = kb_ref.shape[1]

    @pl.when(pl.program_id(0) == 0)
    def _():
        qi = lax.broadcasted_iota(jnp.int32, (t, wb), 0)
        kj = lax.broadcasted_iota(jnp.int32, (t, wb), 1)
        dist_b = qi + wb - kj
        mask_b = (dist_b >= 0) & (dist_b < WINDOW)
        qn = lax.broadcasted_iota(jnp.int32, (t, t), 0)
        kn = lax.broadcasted_iota(jnp.int32, (t, t), 1)
        dist_n = qn - kn
        mask_n = (dist_n >= 0) & (dist_n < WINDOW)
        for h in range(ATT_HEADS):
            tbb_ref[h] = _bias_table(idxb_ref[...], mask_b, rb_ref, h)
            tbn_ref[h] = _bias_table(idxn_ref[...], mask_n, rb_ref, h)

    q = q_ref[...]
    kvn = kvn_ref[...]
    kb = kb_ref[...]
    vb = vb_ref[...]
    kbo_ref[:, :wb - t, :] = kb[:, t:, :]
    kbo_ref[:, wb - t:, :] = kvn[:, :, :ATT_KV_WIDTH]
    vbo_ref[:, :wb - t, :] = vb[:, t:, :]
    vbo_ref[:, wb - t:, :] = kvn[:, :, ATT_KV_WIDTH:]
    bnt = lambda a, b_: jnp.einsum("bqd,bkd->bqk", a, b_, preferred_element_type=F32)
    bnn = lambda a, b_: jnp.einsum("bqk,bkd->bqd", a, b_, preferred_element_type=F32)
    s_heads, v_heads = [], []
    for h2 in range(ATT_KV_HEADS):
        ks = slice(h2 * n, (h2 + 1) * n)
        vs = slice(ATT_KV_WIDTH + h2 * n, ATT_KV_WIDTH + (h2 + 1) * n)
        k_buf = kb[:, :, ks].astype(BF16)
        v_buf = vb[:, :, ks].astype(BF16)
        k_new = kvn[:, :, ks].astype(BF16)
        v_new = kvn[:, :, vs].astype(BF16)
        for g in range(ATT_GROUPS):
            h = h2 * ATT_GROUPS + g
            qh = q[:, :, h * n:(h + 1) * n].astype(BF16)
            s_heads.append([bnt(qh, k_buf) * (HEAD_DIM ** -0.5) + tbb_ref[h][None],
                            bnt(qh, k_new) * (HEAD_DIM ** -0.5) + tbn_ref[h][None]])
            v_heads.append([(v_buf, bnn), (v_new, bnn)])
    outs = _softmax_sink_pv(s_heads, v_heads, [sink_ref[0, h] for h in range(ATT_HEADS)])
    for h, o in enumerate(outs):
        o_ref[:, :, h * n:(h + 1) * n] = o


def _attn_sample(q, kv, kbuf, vbuf, rel_bias, sinks, bb):
    b, t, _ = q.shape
    wb = kbuf.shape[1]
    dist_b = np.arange(t)[:, None] + wb - np.arange(wb)[None, :]
    dist_n = np.arange(t)[:, None] - np.arange(t)[None, :]
    idx_b = jnp.asarray(_rel_bucket_np(dist_b))
    idx_n = jnp.asarray(_rel_bucket_np(dist_n))
    smem = pl.BlockSpec(memory_space=pltpu.SMEM)
    tok = lambda w: pl.BlockSpec((bb, t, w), lambda i: (i, 0, 0))
    buf = pl.BlockSpec((bb, wb, ATT_KV_WIDTH), lambda i: (i, 0, 0))
    return pl.pallas_call(
        _attn_sample_kernel,
        grid=(b // bb,),
        in_specs=[tok(ATT_WIDTH), tok(2 * ATT_KV_WIDTH), buf, buf,
                  pl.BlockSpec(idx_b.shape, lambda i: (0, 0)),
                  pl.BlockSpec(idx_n.shape, lambda i: (0, 0)),
                  smem, smem],
        out_specs=[tok(ATT_WIDTH), buf, buf],
        out_shape=[jax.ShapeDtypeStruct((b, t, ATT_WIDTH), F32),
                   jax.ShapeDtypeStruct(kbuf.shape, F32),
                   jax.ShapeDtypeStruct(vbuf.shape, F32)],
        scratch_shapes=[pltpu.VMEM((ATT_HEADS, t, wb), F32), pltpu.VMEM((ATT_HEADS, t, t), F32)],
        compiler_params=pltpu.CompilerParams(dimension_semantics=("arbitrary",)),
        name="attn_sample",
    )(q, kv, kbuf, vbuf, idx_b, idx_n, rel_bias, sinks.reshape(1, ATT_HEADS))


def _out_kernel(x1_ref, mod_ref, y_ref, g_ref, bonus_ref, att_ref, gnw_ref, gnb_ref,
                wo_ref, w1_ref, w2_ref, lng_ref, lnb_ref, o_ref, *, d_ff):
    bb, tt, d = x1_ref.shape
    w = RWKV_WIDTH
    m = bb * tt
    x1 = x1_ref[...]
    mod = lambda k: mod_ref[:, k:k + 1, :]
    y = y_ref[...].reshape(m, w)
    mu = _head_sums(y) * (1.0 / HEAD_DIM)
    dy = y - mu
    var = _head_sums(dy * dy) * (1.0 / HEAD_DIM)
    yn = dy * lax.rsqrt(var + GN_EPS) * gnw_ref[...] + gnb_ref[...]
    y_rwkv = (yn + bonus_ref[...].reshape(m, w)) * g_ref[...].reshape(m, w)
    mix = (_dot(y_rwkv.astype(BF16), wo_ref[:w, :])
           + _dot(att_ref[...].reshape(m, ATT_WIDTH).astype(BF16), wo_ref[w:, :])).reshape(bb, tt, d)
    x2 = _post_ln(ALPHA * x1 + (1.0 + mod(5)) * mix, lng_ref[1:2, :], lnb_ref[1:2, :])
    u = (x2 * (1.0 + mod(7)) + mod(6)).reshape(m, d).astype(BF16)
    f2 = _swiglu(u, w1_ref, w2_ref, d_ff).reshape(bb, tt, d)
    o_ref[...] = _post_ln(ALPHA * x2 + 0.5 * (1.0 + mod(8)) * f2, lng_ref[2:3, :], lnb_ref[2:3, :])


def _out(x1, mod, y, g, bonus, att, gn_w, gn_b, wo, w1, w2, ln_g, ln_b, bb, tt):
    b, t, d = x1.shape
    d_ff = w2.shape[0]
    tok = lambda w: pl.BlockSpec((bb, tt, w), lambda i, j: (i, j, 0))
    consts = [gn_w, gn_b, wo, w1, w2, ln_g, ln_b]
    return pl.pallas_call(
        functools.partial(_out_kernel, d_ff=d_ff),
        grid=(b // bb, t // tt),
        in_specs=[tok(d), pl.BlockSpec((bb, N_MOD, d), lambda i, j: (i, 0, 0)),
                  tok(RWKV_WIDTH), tok(RWKV_WIDTH), tok(RWKV_WIDTH), tok(ATT_WIDTH)]
                 + [_const_spec(c.shape) for c in consts],
        out_specs=tok(d),
        out_shape=jax.ShapeDtypeStruct((b, t, d), F32),
        compiler_params=pltpu.CompilerParams(
            dimension_semantics=("arbitrary", "arbitrary"), vmem_limit_bytes=VMEM_LIMIT_BYTES),
        name="mix_ln_ffn2",
    )(x1, mod, y, g, bonus, att, *consts)


def _layer(x, mod, wkv0, shift0, kbuf, vbuf, wts, bb, tt, wkv_bb, wkv_c, wkv_chunks, wkv_group):
    b, t, d = x.shape
    x1, q, kv, shift_new, r, lw, kt, v, kap, bvec, g, bonus = _ffn1(
        x, mod, shift0[:, None, :], wts["w1a"], wts["w2a"], wts["win"], wts["ln_g"], wts["ln_b"],
        wts["mu"], wts["w0"], wts["a0"], wts["k_k"], wts["k_a"], wts["r_k"], wts["w_lora"], bb, tt)
    y, s_fin = _wkv(r, lw, kt, v, kap, bvec, wkv0, wkv_bb, wkv_c, wkv_chunks, wkv_group)
    shift_new = shift_new[:, 0, :]
    if kbuf is None:
        att = _attn_prompt(q, kv, wts["rel_bias"], wts["sinks"])
        wb = WINDOW
        kb_new = kv[:, t - wb:, :ATT_KV_WIDTH]
        vb_new = kv[:, t - wb:, ATT_KV_WIDTH:]
    else:
        att, kb_new, vb_new = _attn_sample(q, kv, kbuf, vbuf, wts["rel_bias"], wts["sinks"], bb)
    out = _out(x1, mod, y, g, bonus, att, wts["gn_w"], wts["gn_b"], wts["wo"],
               wts["w1b"], wts["w2b"], wts["ln_g"], wts["ln_b"],
               bb * OUT_TILE_FACTOR if tt < ROW_TILE else bb, tt * OUT_TILE_FACTOR if tt >= ROW_TILE else tt)
    return out, s_fin, shift_new, kb_new, vb_new


def kernel(x_prompt, x_sample, state_wkv, state_shift, cache_win_k, cache_win_v, c_prompt, c_sample, rel_bias, w_ada, b_ada, ln_g, ln_b, w_ffn1_in, w_ffn1_out, w_in, mu_shift, w0, w_decay, a0, w_iclr, w_gate, k_k, k_a, r_k, gn_w, gn_b, sinks, w_out, w_ffn2_in, w_ffn2_out):
    bp, tp, d = x_prompt.shape
    bs, ts, _ = x_sample.shape
    depth = w_ada.shape[0]
    assert depth == 1
    l = 0
    w = RWKV_WIDTH
    wb = cache_win_k.shape[2]

    mod_p, mod_s = _ada(c_prompt, c_sample, w_ada[l], b_ada[l])
    mod_p = mod_p.reshape(bp, N_MOD, d)
    mod_s = mod_s.reshape(bs, N_MOD, d)

    w_lora = jnp.zeros((LORA_WIDTH, 3 * w), F32)
    w_lora = w_lora.at[:DECAY_LORA, :w].set(w_decay[l])
    w_lora = w_lora.at[DECAY_LORA:DECAY_LORA + ICLR_LORA, w:2 * w].set(w_iclr[l])
    w_lora = w_lora.at[DECAY_LORA + ICLR_LORA:, 2 * w:].set(w_gate[l])
    row = lambda z: z.reshape(1, -1)
    wts = dict(
        w1a=w_ffn1_in[l].astype(BF16), w2a=w_ffn1_out[l].astype(BF16), win=w_in[l].astype(BF16),
        w1b=w_ffn2_in[l].astype(BF16), w2b=w_ffn2_out[l].astype(BF16), wo=w_out[l].astype(BF16),
        ln_g=ln_g[l], ln_b=ln_b[l], mu=row(mu_shift[l]), w0=row(w0[l]), a0=row(a0[l]),
        k_k=row(k_k[l]), k_a=row(k_a[l]), r_k=row(r_k[l]), w_lora=w_lora.astype(BF16),
        gn_w=row(gn_w[l]), gn_b=row(gn_b[l]), rel_bias=rel_bias, sinks=sinks[l])

    wkv0_p = jnp.zeros((bp, RWKV_HEADS, HEAD_DIM, HEAD_DIM), F32)
    shift0_p = jnp.zeros((bp, RWKV_PROJ), F32)
    y_p, wkv_p, sh_p, k_p, v_p = _layer(x_prompt, mod_p, wkv0_p, shift0_p, None, None, wts,
                                        1, ROW_TILE, bp, WKV_CHUNK, WKV_CHUNKS_PER_STEP, bp)
    kbuf = cache_win_k[l].reshape(bs, wb, ATT_KV_WIDTH)
    vbuf = cache_win_v[l].reshape(bs, wb, ATT_KV_WIDTH)
    y_s, wkv_s, sh_s, k_s, v_s = _layer(x_sample, mod_s, state_wkv[l], state_shift[l], kbuf, vbuf, wts,
                                        ROW_TILE // ts, ts, 8, ts, 1, 8)
    kvshape = lambda z: z.reshape(1, z.shape[0], wb, ATT_KV_HEADS, HEAD_DIM)
    return (y_p, y_s, wkv_p[None], sh_p[None], kvshape(k_p), kvshape(v_p),
            wkv_s[None], sh_s[None], kvshape(k_s), kvshape(v_s))
```
